```python
import math
import jax, jax.numpy as jnp
from jax import lax
import numpy as np

D_MODEL = 2048
BATCH = 4
SEQ = 2048
DEPTH = 1
DEC_BATCH = 32
DEC_SEQ = 4
PAST_LEN = 8192
PAGE_SIZE = 128

D_MIX = D_MODEL
D_SSD = D_MIX // 2
SSD_HEAD_DIM = 64
SSD_HEADS = D_SSD // SSD_HEAD_DIM
SSD_GROUPS = 2
SSD_STATE = 128
SSD_CONV = 4
SSD_CHUNK = 128
CONV_DIM = D_SSD + 2 * SSD_GROUPS * SSD_STATE
D_ATT = D_MIX - D_SSD
ATT_HEAD_DIM = 128
ATT_HEADS = D_ATT // ATT_HEAD_DIM
Q_BLOCK = 128
SB_BIAS_INIT = -7.0
D_FF = 5632
FFN_CONV = 3
IN_DIM = D_SSD + CONV_DIM + SSD_HEADS + 3 * D_ATT
EPS = 1e-6

kernel_name = 'hybrid_ssd_stickbreak_convffn_step'


def rmsnorm(x, g):
    x32 = x.astype(jnp.float32)
    y = x32 * lax.rsqrt(jnp.mean(x32 * x32, axis=-1, keepdims=True) + EPS)
    return (y * g.astype(jnp.float32)).astype(x.dtype)


def causal_dwconv(u, w, b, ctx):
    L = u.shape[1]
    K = w.shape[0]
    up = jnp.concatenate([ctx.astype(u.dtype), u], axis=1)
    y = up[:, 0:L, :] * w[0]
    for k in range(1, K):
        y = y + up[:, k:k + L, :] * w[k]
    return y + b, up[:, up.shape[1] - (K - 1):, :]


def ssd_chunked(xs, dt, a, bm, cm, h0, chunk):
    b, L, H, P = xs.shape
    G, N = bm.shape[2], bm.shape[3]
    nc = L // chunk
    f32 = jnp.float32
    xdt = (xs.astype(f32) * dt[..., None]).reshape(b, nc, chunk, H, P)
    bh = jnp.repeat(bm.astype(f32), H // G, axis=2).reshape(b, nc, chunk, H, N)
    ch = jnp.repeat(cm.astype(f32), H // G, axis=2).reshape(b, nc, chunk, H, N)
    acum = jnp.cumsum((dt * a).reshape(b, nc, chunk, H), axis=2)
    causal = jnp.tril(jnp.ones((chunk, chunk), bool))[None, None, :, :, None]
    seg = acum[:, :, :, None, :] - acum[:, :, None, :, :]
    decay = jnp.where(causal, jnp.exp(jnp.where(causal, seg, 0.0)), 0.0)
    scores = jnp.einsum('bclhn,bcshn->bclsh', ch, bh) * decay
    y_diag = jnp.einsum('bclsh,bcshp->bclhp', scores, xdt)
    decay_end = jnp.exp(acum[:, :, -1:, :] - acum)
    states = jnp.einsum('bclhn,bclh,bclhp->bchpn', bh, decay_end, xdt)
    chunk_decay = jnp.exp(acum[:, :, -1, :])

    def step(hc, inp):
        s, dcy = inp
        return dcy[:, :, None, None] * hc + s, hc

    h_final, h_prev = lax.scan(step, h0.astype(f32),
                               (jnp.moveaxis(states, 1, 0), jnp.moveaxis(chunk_decay, 1, 0)))
    h_prev = jnp.moveaxis(h_prev, 0, 1)
    y_off = jnp.einsum('bclhn,bchpn,bclh->bclhp', ch, h_prev, jnp.exp(acum))
    return (y_diag + y_off).reshape(b, L, H, P), h_final


def sb_block(q, qpos, k, v, kpos, bias):
    z = jnp.einsum('bhqd,bhkd->bhqk', q, k, preferred_element_type=jnp.float32) * (ATT_HEAD_DIM ** -0.5)
    z = z + bias.astype(jnp.float32)[None, :, None, None]
    mask = (kpos[None, :] < qpos[:, None])[None, None]
    log_1m = jnp.where(mask, jax.nn.log_sigmoid(-z), 0.0)
    tail = lax.cumsum(log_1m, axis=3, reverse=True) - log_1m
    w = jnp.where(mask, jnp.exp(jax.nn.log_sigmoid(z) + tail), 0.0)
    return jnp.einsum('bhqk,bhkd->bhqd', w.astype(v.dtype), v)


def sb_attention(q, k, v, q_pos0, bias):
    b, Lq, H, d = q.shape
    qt = q.transpose(0, 2, 1, 3)
    kt = k.transpose(0, 2, 1, 3)
    vt = v.transpose(0, 2, 1, 3)
    kpos = jnp.arange(k.shape[1])
    qpos = q_pos0 + jnp.arange(Lq)
    if Lq % Q_BLOCK == 0 and Lq > Q_BLOCK:
        nb = Lq // Q_BLOCK
        qb = qt.reshape(b, H, nb, Q_BLOCK, d).transpose(2, 0, 1, 3, 4)
        ob = lax.map(lambda a: sb_block(a[0], a[1], kt, vt, kpos, bias), (qb, qpos.reshape(nb, Q_BLOCK)))
        o = ob.transpose(1, 2, 0, 3, 4).reshape(b, H, Lq, d)
    else:
        o = sb_block(qt, qpos, kt, vt, kpos, bias)
    return o.transpose(0, 2, 1, 3).reshape(b, Lq, H * d)


def mixer(h, lw, conv_ctx, ssm_h0, k_past, v_past):
    b, L, _ = h.shape
    proj = h @ lw['w_in']
    o1 = D_SSD
    o2 = o1 + CONV_DIM
    o3 = o2 + SSD_HEADS
    o4 = o3 + D_ATT
    o5 = o4 + D_ATT
    z, xbc, dt_raw, q, k, v = jnp.split(proj, [o1, o2, o3, o4, o5], axis=-1)
    xbc, conv_new = causal_dwconv(xbc, lw['conv_w'], lw['conv_b'], conv_ctx)
    xbc = jax.nn.silu(xbc)
    xs, bm, cm = jnp.split(xbc, [D_SSD, D_SSD + SSD_GROUPS * SSD_STATE], axis=-1)
    xs = xs.reshape(b, L, SSD_HEADS, SSD_HEAD_DIM)
    dt = jax.nn.softplus(dt_raw.astype(jnp.float32) + lw['dt_bias'].astype(jnp.float32))
    a = -jnp.exp(lw['a_log'].astype(jnp.float32))
    chunk = SSD_CHUNK if L % SSD_CHUNK == 0 else L
    y, ssm_new = ssd_chunked(xs, dt, a,
                             bm.reshape(b, L, SSD_GROUPS, SSD_STATE),
                             cm.reshape(b, L, SSD_GROUPS, SSD_STATE), ssm_h0, chunk)
    y = y.astype(h.dtype) + lw['d_skip'][:, None] * xs
    y = y.reshape(b, L, D_SSD) * jax.nn.silu(z)
    y = rmsnorm(y.reshape(b, L, SSD_GROUPS, D_SSD // SSD_GROUPS),
                lw['g_ssd_norm'].reshape(SSD_GROUPS, D_SSD // SSD_GROUPS)).reshape(b, L, D_SSD)
    kh = k.reshape(b, L, ATT_HEADS, ATT_HEAD_DIM)
    vh = v.reshape(b, L, ATT_HEADS, ATT_HEAD_DIM)
    if k_past is None:
        k_all, v_all, pos0 = kh, vh, 0
    else:
        k_all = jnp.concatenate([k_past.astype(kh.dtype), kh], axis=1)
        v_all = jnp.concatenate([v_past.astype(vh.dtype), vh], axis=1)
        pos0 = k_past.shape[1]
    o = sb_attention(q.reshape(b, L, ATT_HEADS, ATT_HEAD_DIM), k_all, v_all, pos0, lw['sb_bias'])
    out = jnp.concatenate([y, o], axis=-1) @ lw['w_out']
    return out, kh, vh, ssm_new, conv_new


def conv_ffn(h, lw, ctx):
    u = h @ lw['w_ffn_in']
    uc, new_ctx = causal_dwconv(u, lw['ffn_conv_w'], lw['ffn_conv_b'], ctx)
    gate, val = jnp.split(uc, 2, axis=-1)
    return (jax.nn.silu(gate) * val) @ lw['w_ffn_out'], new_ctx


def decoder_layer(x, c, lw, conv_ctx, ssm_h0, ffn_ctx, k_past, v_past):
    mod = jax.nn.silu(c) @ lw['w_ada'] + lw['b_ada']
    sh1, sc1, g1, sh2, sc2, g2 = jnp.split(mod[:, None, :], 6, axis=-1)
    h = rmsnorm(x, lw['g_pre_mix']) * (1 + sc1) + sh1
    m, k_new, v_new, ssm_new, conv_new = mixer(h, lw, conv_ctx, ssm_h0, k_past, v_past)
    x = x + g1 * rmsnorm(m, lw['g_post_mix'])
    h = rmsnorm(x, lw['g_pre_ffn']) * (1 + sc2) + sh2
    f, ffn_new = conv_ffn(h, lw, ffn_ctx)
    x = x + g2 * rmsnorm(f, lw['g_post_ffn'])
    return x, (k_new, v_new, ssm_new, conv_new, ffn_new)


def setup_inputs(seed: int = 0) -> dict:
    key = jax.random.key(seed)
    ks = jax.random.split(key, 32)
    f32 = jnp.float32
    n_pages = PAST_LEN // PAGE_SIZE
    n_pool = (DEC_BATCH * n_pages * 5) // 4

    def nrm(k, shape, scale):
        return jax.random.normal(k, shape, f32) * scale

    def gain(k, shape):
        return 1.0 + 0.05 * jax.random.normal(k, shape, f32)

    dt0 = jnp.exp(jax.random.uniform(ks[24], (DEPTH, SSD_HEADS), f32, math.log(1e-3), math.log(1e-1)))
    return {
        'x_prompt': nrm(ks[0], (BATCH, SEQ, D_MODEL), 1.0),
        'x_sample': nrm(ks[1], (DEC_BATCH, DEC_SEQ, D_MODEL), 1.0),
        'c_prompt': nrm(ks[2], (BATCH, D_MODEL), 1.0),
        'c_sample': nrm(ks[3], (DEC_BATCH, D_MODEL), 1.0),
        'cache_k': nrm(ks[4], (DEPTH, n_pool, PAGE_SIZE, ATT_HEADS, ATT_HEAD_DIM), 1.0),
        'cache_v': nrm(ks[5], (DEPTH, n_pool, PAGE_SIZE, ATT_HEADS, ATT_HEAD_DIM), 1.0),
        'page_table': jax.random.permutation(ks[6], n_pool)[: DEC_BATCH * n_pages]
                      .reshape(DEC_BATCH, n_pages).astype(jnp.int32),
        'state_ssm': nrm(ks[7], (DEPTH, DEC_BATCH, SSD_HEADS, SSD_HEAD_DIM, SSD_STATE), 0.5),
        'state_conv': nrm(ks[8], (DEPTH, DEC_BATCH, SSD_CONV - 1, CONV_DIM), 1.0),
        'state_ffn_conv': nrm(ks[9], (DEPTH, DEC_BATCH, FFN_CONV - 1, 2 * D_FF), 1.0),
        'w_ada': nrm(ks[10], (DEPTH, D_MODEL, 6 * D_MODEL), D_MODEL ** -0.5),
        'b_ada': nrm(ks[11], (DEPTH, 6 * D_MODEL), 0.01),
        'g_pre_mix': gain(ks[12], (DEPTH, D_MODEL)),
        'g_post_mix': gain(ks[13], (DEPTH, D_MODEL)),
        'g_pre_ffn': gain(ks[14], (DEPTH, D_MODEL)),
        'g_post_ffn': gain(ks[15], (DEPTH, D_MODEL)),
        'w_in': nrm(ks[16], (DEPTH, D_MODEL, IN_DIM), D_MODEL ** -0.5),
        'conv_w': nrm(ks[17], (DEPTH, SSD_CONV, CONV_DIM), SSD_CONV ** -0.5),
        'conv_b': nrm(ks[18], (DEPTH, CONV_DIM), 0.01),
        'dt_bias': dt0 + jnp.log(-jnp.expm1(-dt0)),
        'a_log': jnp.log(jax.random.uniform(ks[19], (DEPTH, SSD_HEADS), f32, 1.0, 16.0)),
        'd_skip': gain(ks[20], (DEPTH, SSD_HEADS)),
        'g_ssd_norm': gain(ks[21], (DEPTH, D_SSD)),
        'sb_bias': SB_BIAS_INIT + 0.3 * jax.random.normal(ks[28], (DEPTH, ATT_HEADS), f32),
        'w_out': nrm(ks[22], (DEPTH, D_MIX, D_MODEL), D_MIX ** -0.5),
        'w_ffn_in': nrm(ks[23], (DEPTH, D_MODEL, 2 * D_FF), D_MODEL ** -0.5),
        'ffn_conv_w': nrm(ks[25], (DEPTH, FFN_CONV, 2 * D_FF), FFN_CONV ** -0.5),
        'ffn_conv_b': nrm(ks[26], (DEPTH, 2 * D_FF), 0.01),
        'w_ffn_out': nrm(ks[27], (DEPTH, D_FF, D_MODEL), D_FF ** -0.5),
    }


def reference(x_prompt, x_sample, c_prompt, c_sample, cache_k, cache_v, page_table,
              state_ssm, state_conv, state_ffn_conv, w_ada, b_ada, g_pre_mix, g_post_mix,
              g_pre_ffn, g_post_ffn, w_in, conv_w, conv_b, dt_bias, a_log, d_skip,
              g_ssd_norm, sb_bias, w_out, w_ffn_in, ffn_conv_w, ffn_conv_b, w_ffn_out):
    n_seq, n_pages = page_table.shape
    past = n_pages * PAGE_SIZE
    bp = x_prompt.shape[0]
    yp, ys = x_prompt, x_sample
    p_states, s_states = [], []
    for l in range(DEPTH):
        lw = dict(w_ada=w_ada[l], b_ada=b_ada[l], g_pre_mix=g_pre_mix[l], g_post_mix=g_post_mix[l],
                  g_pre_ffn=g_pre_ffn[l], g_post_ffn=g_post_ffn[l], w_in=w_in[l], conv_w=conv_w[l],
                  conv_b=conv_b[l], dt_bias=dt_bias[l], a_log=a_log[l], d_skip=d_skip[l],
                  g_ssd_norm=g_ssd_norm[l], sb_bias=sb_bias[l], w_out=w_out[l], w_ffn_in=w_ffn_in[l],
                  ffn_conv_w=ffn_conv_w[l], ffn_conv_b=ffn_conv_b[l], w_ffn_out=w_ffn_out[l])
        conv0 = jnp.zeros((bp, SSD_CONV - 1, CONV_DIM), yp.dtype)
        ssm0 = jnp.zeros((bp, SSD_HEADS, SSD_HEAD_DIM, SSD_STATE), jnp.float32)
        ffn0 = jnp.zeros((bp, FFN_CONV - 1, 2 * D_FF), yp.dtype)
        yp, sp = decoder_layer(yp, c_prompt, lw, conv0, ssm0, ffn0, None, None)
        p_states.append(sp)
        k_past = cache_k[l][page_table].reshape(n_seq, past, ATT_HEADS, ATT_HEAD_DIM)
        v_past = cache_v[l][page_table].reshape(n_seq, past, ATT_HEADS, ATT_HEAD_DIM)
        ys, ss = decoder_layer(ys, c_sample, lw, state_conv[l], state_ssm[l], state_ffn_conv[l],
                               k_past, v_past)
        s_states.append(ss)
    k_prompt = jnp.stack([s[0] for s in p_states])
    v_prompt = jnp.stack([s[1] for s in p_states])
    ssm_prompt = jnp.stack([s[2] for s in p_states])
    conv_prompt = jnp.stack([s[3] for s in p_states])
    ffn_conv_prompt = jnp.stack([s[4] for s in p_states])
    k_sample = jnp.stack([s[0] for s in s_states])
    v_sample = jnp.stack([s[1] for s in s_states])
    ssm_sample = jnp.stack([s[2] for s in s_states])
    conv_sample = jnp.stack([s[3] for s in s_states])
    ffn_conv_sample = jnp.stack([s[4] for s in s_states])
    return (yp, ys, k_prompt, v_prompt, ssm_prompt, conv_prompt, ffn_conv_prompt,
            k_sample, v_sample, ssm_sample, conv_sample, ffn_conv_sample)
```

```python
import functools

import jax
import jax.numpy as jnp
from jax import lax
from jax.experimental import pallas as pl
from jax.experimental.pallas import tpu as pltpu

F32 = jnp.float32
BF16 = jnp.bfloat16

D_MODEL = 2048
D_SSD = 1024
SSD_HEAD_DIM = 64
SSD_HEADS = 16
SSD_GROUPS = 2
SSD_STATE = 128
SSD_CONV = 4
SSD_CHUNK = 128
BC_DIM = 2 * SSD_GROUPS * SSD_STATE
D_ATT = 1024
ATT_HEAD_DIM = 128
ATT_HEADS = 8
D_FF = 5632
FFN_CONV = 3
PAGE_SIZE = 128
EPS = 1e-6
ATT_SCALE = ATT_HEAD_DIM ** -0.5

COL_Z, COL_XS, COL_Q, COL_K, COL_V, COL_BC, COL_DT = 0, 1024, 2048, 3072, 4096, 5120, 5632
PROJ_W = 5760
DT_PAD = 128

SAMPLE_ROWS = 8
LANES = 128
VMEM_LIMIT = 56 * 1024 * 1024


def _dot(a, b):
    return jnp.dot(a, b, preferred_element_type=F32)


def _dot_nt(a, b):
    return lax.dot_general(a, b, (((1,), (1,)), ((), ())), preferred_element_type=F32)


def _sigmoid(x):
    return 1.0 / (1.0 + jnp.exp(-x))


def _silu(x):
    return x * _sigmoid(x)


def _softplus(x):
    return jnp.maximum(x, 0.0) + jnp.log1p(jnp.exp(-jnp.abs(x)))


def _rms(x, g):
    ms = jnp.mean(x * x, axis=-1, keepdims=True)
    return x * lax.rsqrt(ms + EPS) * g


def _split3(x):
    h = x.astype(BF16)
    r = x - h.astype(F32)
    m = r.astype(BF16)
    l = (r - m.astype(F32)).astype(BF16)
    return h, m, l


def _dot3_right(x, mat):
    h, m, l = _split3(x)
    return _dot(h, mat) + _dot(m, mat) + _dot(l, mat)


def _dot3_left(mat, x):
    h, m, l = _split3(x)
    return _dot(mat, h) + _dot(mat, m) + _dot(mat, l)


def _ada_kernel(c_ref, w_ref, b_ref, o_ref):
    s = _silu(c_ref[...]).astype(BF16)
    o_ref[...] = _dot(s, w_ref[...].astype(BF16)) + b_ref[...]


def _ada(c_all, w_ada, b_ada):
    m, d = c_all.shape
    n = w_ada.shape[1]
    tn = 1024
    return pl.pallas_call(
        _ada_kernel,
        grid=(n // tn,),
        in_specs=[pl.BlockSpec((m, d), lambda j: (0, 0)),
                  pl.BlockSpec((d, tn), lambda j: (0, j)),
                  pl.BlockSpec((1, tn), lambda j: (0, j))],
        out_specs=pl.BlockSpec((m, tn), lambda j: (0, j)),
        out_shape=jax.ShapeDtypeStruct((m, n), F32),
        compiler_params=pltpu.CompilerParams(dimension_semantics=("arbitrary",), vmem_limit_bytes=VMEM_LIMIT),
        name="ada_mod",
    )(c_all, w_ada, b_ada)


def _mod_spec(idx, rows, tiles_per_group):
    return pl.BlockSpec((1, 1, rows, D_MODEL), lambda i, *_: (i // tiles_per_group, idx, 0, 0))


def _inproj_kernel(x_ref, sh_ref, sc_ref, g_ref, w_ref, o_ref, hn_ref):
    @pl.when(pl.program_id(1) == 0)
    def _():
        h = _rms(x_ref[...], g_ref[...]) * (1.0 + sc_ref[0, 0]) + sh_ref[0, 0]
        hn_ref[...] = h.astype(BF16)

    o_ref[...] = _dot(hn_ref[...], w_ref[...])


def _inproj(x, mods, g, w, tm, tiles_per_group):
    m = x.shape[0]
    rows = mods.shape[2]
    tn = 1152
    return pl.pallas_call(
        _inproj_kernel,
        grid=(m // tm, PROJ_W // tn),
        in_specs=[pl.BlockSpec((tm, D_MODEL), lambda i, j: (i, 0)),
                  _mod_spec(0, rows, tiles_per_group),
                  _mod_spec(1, rows, tiles_per_group),
                  pl.BlockSpec((1, D_MODEL), lambda i, j: (0, 0)),
                  pl.BlockSpec((D_MODEL, tn), lambda i, j: (0, j))],
        out_specs=pl.BlockSpec((tm, tn), lambda i, j: (i, j)),
        out_shape=jax.ShapeDtypeStruct((m, PROJ_W), F32),
        scratch_shapes=[pltpu.VMEM((tm, D_MODEL), BF16)],
        compiler_params=pltpu.CompilerParams(dimension_semantics=("arbitrary", "arbitrary"),
                                             vmem_limit_bytes=VMEM_LIMIT),
        name="in_proj",
    )(x, mods, mods, g, w)


def _conv4(buf_ref, w, b, rows):
    y = buf_ref[5:5 + rows, :] * w[0:1]
    y = y + buf_ref[6:6 + rows, :] * w[1:2]
    y = y + buf_ref[7:7 + rows, :] * w[2:3]
    y = y + buf_ref[8:8 + rows, :] * w[3:4]
    return y + b


def _ssd_tile(xs, bc, dtr, row_valid, causal, tri, sel, emat, dtb, alog):
    dt = _softplus(dtr + dtb)
    if row_valid is not None:
        dt = jnp.where(row_valid, dt, 0.0)
    a = -jnp.exp(alog)
    acum = _dot3_left(tri, dt * a)
    alast = _dot3_left(sel, acum)
    acx = _dot3_right(acum, emat)
    acum_x = acx[:, :D_SSD]
    acum_x2 = acx[:, D_SSD:]
    e1 = emat[:, :D_SSD]
    dt_x = _dot3_right(dt, e1)
    alast_x = _dot3_right(alast, e1)
    acum_t = acum.T

    xdt = xs * dt_x
    xdt_bf = xdt.astype(BF16)
    xdtd_bf = (xdt * jnp.exp(alast_x - acum_x)).astype(BF16)
    bm = bc[:, :SSD_GROUPS * SSD_STATE]
    cm = bc[:, SSD_GROUPS * SSD_STATE:]
    lane = lax.broadcasted_iota(jnp.int32, (SSD_CHUNK, LANES), 1)
    first_head = lane < SSD_HEAD_DIM

    pieces = []
    heads_per_group = SSD_HEADS // SSD_GROUPS
    for g in range(SSD_GROUPS):
        bg = bm[:, g * SSD_STATE:(g + 1) * SSD_STATE].astype(BF16)
        cg = cm[:, g * SSD_STATE:(g + 1) * SSD_STATE].astype(BF16)
        cb = _dot_nt(cg, bg)
        for pair in range(heads_per_group // 2):
            h0 = g * heads_per_group + 2 * pair
            ys = []
            for hh in (h0, h0 + 1):
                seg = acum_x2[:, hh * LANES:(hh + 1) * LANES] - acum_t[hh:hh + 1, :]
                decay = jnp.where(causal, jnp.exp(jnp.where(causal, seg, 0.0)), 0.0)
                sc = (cb * decay).astype(BF16)
                ys.append(_dot(sc, xdt_bf[:, (h0 // 2) * LANES:(h0 // 2 + 1) * LANES]))
            pieces.append(jnp.where(first_head, ys[0], ys[1]))
    y_diag = jnp.concatenate(pieces, axis=1)
    return y_diag, xdtd_bf, acum_x, bm, cm


def _ssd_finish(y, xs, z, dskip, gnorm):
    y = y + dskip * xs
    y = y * _silu(z)
    half = D_SSD // SSD_GROUPS
    outs = []
    for g in range(SSD_GROUPS):
        outs.append(_rms(y[:, g * half:(g + 1) * half], gnorm[:, g * half:(g + 1) * half]))
    return jnp.concatenate(outs, axis=1)


def _ssd_prompt_kernel(z_ref, xs_ref, bc_ref, dt_ref, cw_ref, cb_ref, dtb_ref, alog_ref, dskip_ref, gnorm_ref,
                       tri_ref, sel_ref, emat_ref, y_ref, st_ref, ht_ref, xsbuf, bcbuf):
    c = pl.program_id(1)
    rows = SSD_CHUNK

    @pl.when(c == 0)
    def _():
        ht_ref[...] = jnp.zeros_like(ht_ref)
        xsbuf[0:8, :] = jnp.zeros((8, D_SSD), F32)
        bcbuf[0:8, :] = jnp.zeros((8, BC_DIM), F32)

    xsbuf[8:8 + rows, :] = xs_ref[0]
    bcbuf[8:8 + rows, :] = bc_ref[0]
    cw = cw_ref[...]
    cb = cb_ref[...]
    xs = _silu(_conv4(xsbuf, cw[:, :D_SSD], cb[:, :D_SSD], rows))
    bc = _silu(_conv4(bcbuf, cw[:, D_SSD:], cb[:, D_SSD:], rows))
    xsbuf[0:8, :] = xsbuf[rows:rows + 8, :]
    bcbuf[0:8, :] = bcbuf[rows:rows + 8, :]

    r = lax.broadcasted_iota(jnp.int32, (rows, rows), 0)
    s = lax.broadcasted_iota(jnp.int32, (rows, rows), 1)
    causal = s <= r
    y_diag, xdtd_bf, acum_x, bm, cm = _ssd_tile(xs, bc, dt_ref[0], None, causal, tri_ref[...], sel_ref[...],
                                                 emat_ref[...], dtb_ref[...], alog_ref[...])
    ht = ht_ref[...]
    half = D_SSD // SSD_GROUPS
    y_off, states = [], []
    for g in range(SSD_GROUPS):
        bg = bm[:, g * SSD_STATE:(g + 1) * SSD_STATE]
        cg = cm[:, g * SSD_STATE:(g + 1) * SSD_STATE].astype(BF16)
        y_off.append(_dot(cg, ht[:, g * half:(g + 1) * half].astype(BF16)))
        states.append(_dot(bg.T.astype(BF16), xdtd_bf[:, g * half:(g + 1) * half]))
    y = y_diag + jnp.concatenate(y_off, axis=1) * jnp.exp(acum_x)
    ht_new = jnp.exp(acum_x[rows - 1:rows, :]) * ht + jnp.concatenate(states, axis=1)
    ht_ref[...] = ht_new
    y_ref[0] = _ssd_finish(y, xs, z_ref[0], dskip_ref[...], gnorm_ref[...]).astype(BF16)

    @pl.when(c == pl.num_programs(1) - 1)
    def _():
        for k in range(D_SSD // LANES):
            st_ref[0, k * LANES:(k + 1) * LANES, :] = ht_new[:, k * LANES:(k + 1) * LANES].T


def _const_spec(shape):
    nd = len(shape)
    return pl.BlockSpec(shape, lambda *_: (0,) * nd)


def _ssd_prompt(proj3, cw, cb, dtb, alog, dskip, gnorm, tri, sel, emat):
    nb, seq, _ = proj3.shape
    nc = seq // SSD_CHUNK
    return pl.pallas_call(
        _ssd_prompt_kernel,
        grid=(nb, nc),
        in_specs=[pl.BlockSpec((1, SSD_CHUNK, D_SSD), lambda b, c: (b, c, COL_Z // D_SSD)),
                  pl.BlockSpec((1, SSD_CHUNK, D_SSD), lambda b, c: (b, c, COL_XS // D_SSD)),
                  pl.BlockSpec((1, SSD_CHUNK, BC_DIM), lambda b, c: (b, c, COL_BC // BC_DIM)),
                  pl.BlockSpec((1, SSD_CHUNK, DT_PAD), lambda b, c: (b, c, COL_DT // DT_PAD)),
                  _const_spec(cw.shape), _const_spec(cb.shape), _const_spec(dtb.shape), _const_spec(alog.shape),
                  _const_spec(dskip.shape), _const_spec(gnorm.shape), _const_spec(tri.shape),
                  _const_spec(sel.shape), _const_spec(emat.shape)],
        out_specs=[pl.BlockSpec((1, SSD_CHUNK, D_SSD), lambda b, c: (b, c, 0)),
                   pl.BlockSpec((1, D_SSD, SSD_STATE), lambda b, c: (b, 0, 0))],
        out_shape=[jax.ShapeDtypeStruct((nb, seq, D_SSD), BF16),
                   jax.ShapeDtypeStruct((nb, D_SSD, SSD_STATE), F32)],
        scratch_shapes=[pltpu.VMEM((SSD_STATE, D_SSD), F32),
                        pltpu.VMEM((SSD_CHUNK + 8, D_SSD), F32),
                        pltpu.VMEM((SSD_CHUNK + 8, BC_DIM), F32)],
        compiler_params=pltpu.CompilerParams(dimension_semantics=("arbitrary", "arbitrary"),
                                             vmem_limit_bytes=VMEM_LIMIT),
        name="ssd_prompt",
    )(proj3, proj3, proj3, proj3, cw, cb, dtb, alog, dskip, gnorm, tri, sel, emat)


def _ssd_sample_kernel(z_ref, xs_ref, bc_ref, dt_ref, ctx_ref, h0_ref, cw_ref, cb_ref, dtb_ref, alog_ref, dskip_ref,
                       gnorm_ref, tri_ref, sel_ref, emat_ref, y_ref, st_ref,
                       xsbuf, bcbuf, ybase, yoff, xs_s, xdtd_s, acumx_s, bm_s, cm_s):
    s_id = pl.program_id(1)
    rows = SSD_CHUNK
    nseq = rows // SAMPLE_ROWS
    rr = lax.broadcasted_iota(jnp.int32, (rows, 1), 0)

    @pl.when(s_id == 0)
    def _():
        ctx = ctx_ref[0]
        is_ctx = (rr % SAMPLE_ROWS) >= SAMPLE_ROWS - (SSD_CONV - 1)
        xsbuf[0:8, :] = ctx[0:8, :D_SSD]
        bcbuf[0:8, :] = ctx[0:8, D_SSD:]
        xsbuf[8:8 + rows, :] = jnp.where(is_ctx, ctx[8:, :D_SSD], xs_ref[...])
        bcbuf[8:8 + rows, :] = jnp.where(is_ctx, ctx[8:, D_SSD:], bc_ref[...])
        cw = cw_ref[...]
        cb = cb_ref[...]
        xs = _silu(_conv4(xsbuf, cw[:, :D_SSD], cb[:, :D_SSD], rows))
        bc = _silu(_conv4(bcbuf, cw[:, D_SSD:], cb[:, D_SSD:], rows))
        r = lax.broadcasted_iota(jnp.int32, (rows, rows), 0)
        c = lax.broadcasted_iota(jnp.int32, (rows, rows), 1)
        causal = (c <= r) & ((c // SAMPLE_ROWS) == (r // SAMPLE_ROWS))
        row_valid = (rr % SAMPLE_ROWS) < (SAMPLE_ROWS // 2)
        y_diag, xdtd_bf, acum_x, bm, cm = _ssd_tile(xs, bc, dt_ref[...], row_valid, causal, tri_ref[...],
                                                     sel_ref[...], emat_ref[...], dtb_ref[...], alog_ref[...])
        ybase[...] = y_diag
        yoff[...] = jnp.zeros_like(yoff)
        xs_s[...] = xs
        xdtd_s[...] = xdtd_bf
        acumx_s[...] = acum_x
        bm_s[...] = bm
        cm_s[...] = cm

    in_seq_rows = (rr // SAMPLE_ROWS) == s_id
    in_seq_lanes = (lax.broadcasted_iota(jnp.int32, (1, rows), 1) // SAMPLE_ROWS) == s_id
    h0 = h0_ref[0]
    ht = jnp.concatenate([h0[k * LANES:(k + 1) * LANES, :].T for k in range(D_SSD // LANES)], axis=1)
    half = D_SSD // SSD_GROUPS
    y_off, states = [], []
    for g in range(SSD_GROUPS):
        bg_t = jnp.where(in_seq_lanes, bm_s[:, g * SSD_STATE:(g + 1) * SSD_STATE].T, 0.0).astype(BF16)
        cg = cm_s[:, g * SSD_STATE:(g + 1) * SSD_STATE].astype(BF16)
        y_off.append(_dot(cg, ht[:, g * half:(g + 1) * half].astype(BF16)))
        states.append(_dot(bg_t, xdtd_s[:, g * half:(g + 1) * half]))
    yoff[...] = yoff[...] + jnp.where(in_seq_rows, jnp.concatenate(y_off, axis=1), 0.0)
    last = pl.multiple_of(s_id * SAMPLE_ROWS, SAMPLE_ROWS) + (SAMPLE_ROWS - 1)
    ht_new = jnp.exp(acumx_s[pl.ds(last, 1), :]) * ht + jnp.concatenate(states, axis=1)
    for k in range(D_SSD // LANES):
        st_ref[0, k * LANES:(k + 1) * LANES, :] = ht_new[:, k * LANES:(k + 1) * LANES].T

    @pl.when(s_id == nseq - 1)
    def _():
        y = ybase[...] + yoff[...] * jnp.exp(acumx_s[...])
        y_ref[...] = _ssd_finish(y, xs_s[...], z_ref[...], dskip_ref[...], gnorm_ref[...]).astype(BF16)


def _ssd_sample(proj_s, ctx, h0, cw, cb, dtb, alog, dskip, gnorm, tri, sel, emat):
    m = proj_s.shape[0]
    nt = m // SSD_CHUNK
    nseq = SSD_CHUNK // SAMPLE_ROWS
    return pl.pallas_call(
        _ssd_sample_kernel,
        grid=(nt, nseq),
        in_specs=[pl.BlockSpec((SSD_CHUNK, D_SSD), lambda t, s: (t, COL_Z // D_SSD)),
                  pl.BlockSpec((SSD_CHUNK, D_SSD), lambda t, s: (t, COL_XS // D_SSD)),
                  pl.BlockSpec((SSD_CHUNK, BC_DIM), lambda t, s: (t, COL_BC // BC_DIM)),
                  pl.BlockSpec((SSD_CHUNK, DT_PAD), lambda t, s: (t, COL_DT // DT_PAD)),
                  pl.BlockSpec((1, SSD_CHUNK + 8, D_SSD + BC_DIM), lambda t, s: (t, 0, 0)),
                  pl.BlockSpec((1, D_SSD, SSD_STATE), lambda t, s: (t * nseq + s, 0, 0)),
                  _const_spec(cw.shape), _const_spec(cb.shape), _const_spec(dtb.shape), _const_spec(alog.shape),
                  _const_spec(dskip.shape), _const_spec(gnorm.shape), _const_spec(tri.shape),
                  _const_spec(sel.shape), _const_spec(emat.shape)],
        out_specs=[pl.BlockSpec((SSD_CHUNK, D_SSD), lambda t, s: (t, 0)),
                   pl.BlockSpec((1, D_SSD, SSD_STATE), lambda t, s: (t * nseq + s, 0, 0))],
        out_shape=[jax.ShapeDtypeStruct((m, D_SSD), BF16),
                   jax.ShapeDtypeStruct(h0.shape, F32)],
        scratch_shapes=[pltpu.VMEM((SSD_CHUNK + 8, D_SSD), F32),
                        pltpu.VMEM((SSD_CHUNK + 8, BC_DIM), F32),
                        pltpu.VMEM((SSD_CHUNK, D_SSD), F32),
                        pltpu.VMEM((SSD_CHUNK, D_SSD), F32),
                        pltpu.VMEM((SSD_CHUNK, D_SSD), F32),
                        pltpu.VMEM((SSD_CHUNK, D_SSD), BF16),
                        pltpu.VMEM((SSD_CHUNK, D_SSD), F32),
                        pltpu.VMEM((SSD_CHUNK, SSD_GROUPS * SSD_STATE), F32),
                        pltpu.VMEM((SSD_CHUNK, SSD_GROUPS * SSD_STATE), F32)],
        compiler_params=pltpu.CompilerParams(dimension_semantics=("arbitrary", "arbitrary"),
                                             vmem_limit_bytes=VMEM_LIMIT),
        name="ssd_sample",
    )(proj_s, proj_s, proj_s, proj_s, ctx, h0, cw, cb, dtb, alog, dskip, gnorm, tri, sel, emat)


def _sb_block(qb, kb, vb, bias, uo, run, mask):
    z = _dot_nt(qb, kb) * ATT_SCALE + bias
    lg = jnp.log1p(jnp.exp(-jnp.abs(z)))
    ls_pos = jnp.minimum(z, 0.0) - lg
    lm = jnp.minimum(-z, 0.0) - lg
    if mask is not None:
        lm = jnp.where(mask, lm, 0.0)
    hi = lm.astype(BF16)
    lo = (lm - hi.astype(F32)).astype(BF16)
    st = _dot(jnp.concatenate([hi, lo], axis=1), uo)
    w = jnp.exp(ls_pos + run + st[:, :LANES])
    if mask is not None:
        w = jnp.where(mask, w, 0.0)
    return _dot(w.astype(BF16), vb), run + st[:, LANES:]


def _attn_prompt_kernel(bias_ref, q_ref, k_ref, v_ref, uo_ref, o_ref, kbf, vbf):
    h = pl.program_id(1)
    qi = pl.program_id(2)
    tq = q_ref.shape[1]

    @pl.when(qi == 0)
    def _():
        kbf[...] = k_ref[0].astype(BF16)
        vbf[...] = v_ref[0].astype(BF16)

    qb = q_ref[0].astype(BF16)
    bias = bias_ref[h]
    uo = uo_ref[...]
    r = lax.broadcasted_iota(jnp.int32, (tq, tq), 0)
    c = lax.broadcasted_iota(jnp.int32, (tq, tq), 1)
    start = pl.multiple_of(qi * tq, tq)
    acc, run = _sb_block(qb, kbf[pl.ds(start, tq), :], vbf[pl.ds(start, tq), :], bias, uo,
                         jnp.zeros((tq, LANES), F32), c < r)

    def body(i, carry):
        acc, run = carry
        st = pl.multiple_of((qi - 1 - i) * tq, tq)
        pv, run = _sb_block(qb, kbf[pl.ds(st, tq), :], vbf[pl.ds(st, tq), :], bias, uo, run, None)
        return acc + pv, run

    acc, _ = lax.fori_loop(0, qi, body, (acc, run))
    o_ref[0] = acc.astype(BF16)


def _attn_prompt(proj3, sb_bias, uo):
    nb, seq, _ = proj3.shape
    tq = 128
    return pl.pallas_call(
        _attn_prompt_kernel,
        grid=(nb, ATT_HEADS, seq // tq),
        in_specs=[pl.BlockSpec(memory_space=pltpu.SMEM),
                  pl.BlockSpec((1, tq, ATT_HEAD_DIM), lambda b, h, i: (b, i, COL_Q // ATT_HEAD_DIM + h)),
                  pl.BlockSpec((1, seq, ATT_HEAD_DIM), lambda b, h, i: (b, 0, COL_K // ATT_HEAD_DIM + h)),
                  pl.BlockSpec((1, seq, ATT_HEAD_DIM), lambda b, h, i: (b, 0, COL_V // ATT_HEAD_DIM + h)),
                  _const_spec(uo.shape)],
        out_specs=pl.BlockSpec((1, tq, ATT_HEAD_DIM), lambda b, h, i: (b, i, h)),
        out_shape=jax.ShapeDtypeStruct((nb, seq, D_ATT), BF16),
        scratch_shapes=[pltpu.VMEM((seq, ATT_HEAD_DIM), BF16), pltpu.VMEM((seq, ATT_HEAD_DIM), BF16)],
        compiler_params=pltpu.CompilerParams(dimension_semantics=("arbitrary", "arbitrary", "arbitrary"),
                                             vmem_limit_bytes=VMEM_LIMIT),
        name="attn_prompt",
    )(sb_bias, proj3, proj3, proj3, uo)


PAGES_PER_STEP = 4


def _page_by_head(ref):
    heads = [ref[0, pl.ds(h, PAGE_SIZE, stride=ATT_HEADS), :] for h in range(ATT_HEADS)]
    return jnp.concatenate(heads, axis=1).astype(BF16)


def _attn_sample_kernel(pt_ref, q_ref, kn_ref, vn_ref, bias_ref, uo_ref, *rest):
    k_refs = rest[:PAGES_PER_STEP]
    v_refs = rest[PAGES_PER_STEP:2 * PAGES_PER_STEP]
    o_ref, qbd, knew, vnew, acc, run = rest[2 * PAGES_PER_STEP:]
    b = pl.program_id(0)
    j = pl.program_id(1)
    nrow = ATT_HEADS * SAMPLE_ROWS
    bias = bias_ref[...]
    uo = uo_ref[...]

    @pl.when((b == 0) & (j == 0))
    def _():
        knew[...] = jnp.zeros_like(knew)
        vnew[...] = jnp.zeros_like(vnew)

    @pl.when(j == 0)
    def _():
        q8 = q_ref[0]
        rr = lax.broadcasted_iota(jnp.int32, (nrow, D_ATT), 0)
        cc = lax.broadcasted_iota(jnp.int32, (nrow, D_ATT), 1)
        qt = jnp.concatenate([q8] * ATT_HEADS, axis=0)
        qbd[...] = jnp.where((rr // SAMPLE_ROWS) == (cc // ATT_HEAD_DIM), qt, 0.0).astype(BF16)
        knew[0:SAMPLE_ROWS, :] = kn_ref[0]
        vnew[0:SAMPLE_ROWS, :] = vn_ref[0]
        qq = lax.broadcasted_iota(jnp.int32, (nrow, LANES), 0) % SAMPLE_ROWS
        kk = lax.broadcasted_iota(jnp.int32, (nrow, LANES), 1)
        mask = (kk < qq) & (kk < SAMPLE_ROWS // 2)
        pv, r = _sb_block(qbd[...], knew[...].astype(BF16), vnew[...].astype(BF16), bias, uo,
                          jnp.zeros((nrow, LANES), F32), mask)
        acc[...] = pv
        run[...] = r

    q = qbd[...]
    a = acc[...]
    r = run[...]
    for g in range(PAGES_PER_STEP):
        pv, r = _sb_block(q, _page_by_head(k_refs[g]), _page_by_head(v_refs[g]), bias, uo, r, None)
        a = a + pv
    acc[...] = a
    run[...] = r

    @pl.when(j == pl.num_programs(1) - 1)
    def _():
        for h in range(ATT_HEADS):
            o_ref[0, :, h * ATT_HEAD_DIM:(h + 1) * ATT_HEAD_DIM] = (
                a[h * SAMPLE_ROWS:(h + 1) * SAMPLE_ROWS, h * ATT_HEAD_DIM:(h + 1) * ATT_HEAD_DIM])


def _attn_sample(page_table, projs3, cache_k, cache_v, bias_rows, uo):
    nseq, npages = page_table.shape
    steps = npages // PAGES_PER_STEP

    def page_spec(g):
        return pl.BlockSpec((1, PAGE_SIZE * ATT_HEADS, ATT_HEAD_DIM),
                            lambda b, j, pt: (pt[b, npages - 1 - (j * PAGES_PER_STEP + g)], 0, 0))

    grid_spec = pltpu.PrefetchScalarGridSpec(
        num_scalar_prefetch=1,
        grid=(nseq, steps),
        in_specs=[pl.BlockSpec((1, SAMPLE_ROWS, D_ATT), lambda b, j, pt: (b, 0, COL_Q // D_ATT)),
                  pl.BlockSpec((1, SAMPLE_ROWS, D_ATT), lambda b, j, pt: (b, 0, COL_K // D_ATT)),
                  pl.BlockSpec((1, SAMPLE_ROWS, D_ATT), lambda b, j, pt: (b, 0, COL_V // D_ATT)),
                  pl.BlockSpec(bias_rows.shape, lambda b, j, pt: (0, 0)),
                  pl.BlockSpec(uo.shape, lambda b, j, pt: (0, 0))]
                 + [page_spec(g) for g in range(PAGES_PER_STEP)] * 2,
        out_specs=pl.BlockSpec((1, SAMPLE_ROWS, D_ATT), lambda b, j, pt: (b, 0, 0)),
        scratch_shapes=[pltpu.VMEM((ATT_HEADS * SAMPLE_ROWS, D_ATT), BF16),
                        pltpu.VMEM((PAGE_SIZE, D_ATT), F32),
                        pltpu.VMEM((PAGE_SIZE, D_ATT), F32),
                        pltpu.VMEM((ATT_HEADS * SAMPLE_ROWS, D_ATT), F32),
                        pltpu.VMEM((ATT_HEADS * SAMPLE_ROWS, LANES), F32)])
    return pl.pallas_call(
        _attn_sample_kernel,
        grid_spec=grid_spec,
        out_shape=jax.ShapeDtypeStruct((nseq, SAMPLE_ROWS, D_ATT), F32),
        compiler_params=pltpu.CompilerParams(dimension_semantics=("arbitrary", "arbitrary"),
                                             vmem_limit_bytes=VMEM_LIMIT),
        name="attn_sample",
    )(page_table, projs3, projs3, projs3, bias_rows, uo, *([cache_k] * PAGES_PER_STEP), *([cache_v] * PAGES_PER_STEP))


def _outproj_kernel(y_ref, o_ref, x_ref, g1_ref, sh2_ref, sc2_ref, gpost_ref, gpre_ref, wy_ref, wo_ref, x1_ref, h2_ref):
    m = _dot(y_ref[...], wy_ref[...]) + _dot(o_ref[...].astype(BF16), wo_ref[...])
    x1 = x_ref[...] + g1_ref[0, 0] * _rms(m, gpost_ref[...])
    x1_ref[...] = x1
    h2 = _rms(x1, gpre_ref[...]) * (1.0 + sc2_ref[0, 0]) + sh2_ref[0, 0]
    h2_ref[...] = h2.astype(BF16)


def _outproj(y, o, x, mods, gpost, gpre, w_out, tm, tiles_per_group):
    m = x.shape[0]
    rows = mods.shape[2]
    half = w_out.shape[0] // 2
    row = lambda i: (i, 0)
    return pl.pallas_call(
        _outproj_kernel,
        grid=(m // tm,),
        in_specs=[pl.BlockSpec((tm, D_SSD), row), pl.BlockSpec((tm, D_ATT), row), pl.BlockSpec((tm, D_MODEL), row),
                  _mod_spec(2, rows, tiles_per_group), _mod_spec(3, rows, tiles_per_group),
                  _mod_spec(4, rows, tiles_per_group),
                  _const_spec(gpost.shape), _const_spec(gpre.shape),
                  pl.BlockSpec((half, D_MODEL), lambda i: (0, 0)),
                  pl.BlockSpec((half, D_MODEL), lambda i: (1, 0))],
        out_specs=[pl.BlockSpec((tm, D_MODEL), row), pl.BlockSpec((tm, D_MODEL), row)],
        out_shape=[jax.ShapeDtypeStruct((m, D_MODEL), F32), jax.ShapeDtypeStruct((m, D_MODEL), BF16)],
        compiler_params=pltpu.CompilerParams(dimension_semantics=("arbitrary",), vmem_limit_bytes=VMEM_LIMIT),
        name="out_proj",
    )(y, o, x, mods, mods, mods, gpost, gpre, w_out, w_out)


def _ffn_kernel(sample, tiles_per_group, *refs):
    if sample:
        (h_ref, x1_ref, g2_ref, gpost_ref, wg_ref, wv_ref, wo_ref, cwg_ref, cwv_ref, cbg_ref, cbv_ref,
         injg_ref, injv_ref, y_ref, ug_ref, uv_ref, acc, bufg, bufv) = refs
    else:
        (h_ref, x1_ref, g2_ref, gpost_ref, wg_ref, wv_ref, wo_ref, cwg_ref, cwv_ref, cbg_ref, cbv_ref,
         y_ref, ug_ref, uv_ref, acc, bufg, bufv, halog, halov) = refs
    i = pl.program_id(0)
    j = pl.program_id(1)
    tm = h_ref.shape[0]
    hn = h_ref[...]
    ug = _dot(hn, wg_ref[...])
    uv = _dot(hn, wv_ref[...])
    if sample:
        ug_ref[...] = ug
        uv_ref[...] = uv
        is_ctx = (lax.broadcasted_iota(jnp.int32, (tm, 1), 0) % SAMPLE_ROWS) >= SAMPLE_ROWS - (FFN_CONV - 1)
        bufg[0:8, :] = injg_ref[0:8, :]
        bufv[0:8, :] = injv_ref[0:8, :]
        bufg[8:8 + tm, :] = jnp.where(is_ctx, injg_ref[8:, :], ug)
        bufv[8:8 + tm, :] = jnp.where(is_ctx, injv_ref[8:, :], uv)
    else:
        first = lax.rem(i, tiles_per_group) == 0

        @pl.when(first)
        def _():
            bufg[0:8, :] = jnp.zeros((8, bufg.shape[1]), F32)
            bufv[0:8, :] = jnp.zeros((8, bufv.shape[1]), F32)

        @pl.when(jnp.logical_not(first))
        def _():
            bufg[0:8, :] = halog[j]
            bufv[0:8, :] = halov[j]

        bufg[8:8 + tm, :] = ug
        bufv[8:8 + tm, :] = uv
        halog[j] = bufg[tm:tm + 8, :]
        halov[j] = bufv[tm:tm + 8, :]
        ug_ref[0] = bufg[tm + 8 - (FFN_CONV - 1):tm + 8, :]
        uv_ref[0] = bufv[tm + 8 - (FFN_CONV - 1):tm + 8, :]

    def conv(buf, w, b):
        y = buf[6:6 + tm, :] * w[0:1]
        y = y + buf[7:7 + tm, :] * w[1:2]
        y = y + buf[8:8 + tm, :] * w[2:3]
        return y + b

    gate = conv(bufg, cwg_ref[...], cbg_ref[...])
    val = conv(bufv, cwv_ref[...], cbv_ref[...])
    part = _dot((_silu(gate) * val).astype(BF16), wo_ref[...])

    @pl.when(j == 0)
    def _():
        acc[...] = part

    @pl.when(j != 0)
    def _():
        acc[...] = acc[...] + part

    @pl.when(j == pl.num_programs(1) - 1)
    def _():
        y_ref[...] = x1_ref[...] + g2_ref[0, 0] * _rms(acc[...], gpost_ref[...])


def _ffn(h2, x1, mods, gpost, w_in, w_out, cw, cb, tm, tiles_per_group, inj=None):
    m = h2.shape[0]
    rows = mods.shape[2]
    tf = 512
    nf = D_FF // tf
    sample = inj is not None
    row = lambda i, j: (i, 0)
    gate_col = lambda i, j: (0, j)
    val_col = lambda i, j: (0, nf + j)
    in_specs = [pl.BlockSpec((tm, D_MODEL), row), pl.BlockSpec((tm, D_MODEL), row),
                _mod_spec(5, rows, tiles_per_group), _const_spec(gpost.shape),
                pl.BlockSpec((D_MODEL, tf), gate_col), pl.BlockSpec((D_MODEL, tf), val_col),
                pl.BlockSpec((tf, D_MODEL), lambda i, j: (j, 0)),
                pl.BlockSpec((FFN_CONV, tf), gate_col), pl.BlockSpec((FFN_CONV, tf), val_col),
                pl.BlockSpec((1, tf), gate_col), pl.BlockSpec((1, tf), val_col)]
    args = [h2, x1, mods, gpost, w_in, w_in, w_out, cw, cw, cb, cb]
    scratch = [pltpu.VMEM((tm, D_MODEL), F32), pltpu.VMEM((tm + 8, tf), F32), pltpu.VMEM((tm + 8, tf), F32)]
    if sample:
        in_specs += [pl.BlockSpec((m + 8, tf), gate_col), pl.BlockSpec((m + 8, tf), val_col)]
        args += [inj, inj]
        u_specs = [pl.BlockSpec((tm, tf), lambda i, j: (i, j))] * 2
        u_shapes = [jax.ShapeDtypeStruct((m, D_FF), F32)] * 2
    else:
        u_specs = [pl.BlockSpec((1, FFN_CONV - 1, tf), lambda i, j: (i, 0, j))] * 2
        u_shapes = [jax.ShapeDtypeStruct((m // tm, FFN_CONV - 1, D_FF), F32)] * 2
        scratch += [pltpu.VMEM((nf, 8, tf), F32), pltpu.VMEM((nf, 8, tf), F32)]
    return pl.pallas_call(
        functools.partial(_ffn_kernel, sample, tiles_per_group),
        grid=(m // tm, nf),
        in_specs=in_specs,
        out_specs=[pl.BlockSpec((tm, D_MODEL), row)] + u_specs,
        out_shape=[jax.ShapeDtypeStruct((m, D_MODEL), F32)] + u_shapes,
        scratch_shapes=scratch,
        compiler_params=pltpu.CompilerParams(dimension_semantics=("arbitrary", "arbitrary"),
                                             vmem_limit_bytes=VMEM_LIMIT),
        name="ffn_sample" if sample else "ffn_prompt",
    )(*args)


def _ssd_constants(block_rows):
    r = jnp.arange(SSD_CHUNK)[:, None]
    c = jnp.arange(SSD_CHUNK)[None, :]
    same = (r // block_rows) == (c // block_rows)
    tri = ((c <= r) & same).astype(BF16)
    sel = (c == (r // block_rows) * block_rows + block_rows - 1).astype(BF16)
    hrow = jnp.arange(LANES)[:, None]
    e1 = (hrow == jnp.arange(D_SSD)[None, :] // SSD_HEAD_DIM).astype(BF16)
    e2 = (hrow == jnp.arange(SSD_HEADS * LANES)[None, :] // LANES).astype(BF16)
    return tri, sel, jnp.concatenate([e1, e2], axis=1)


def _suffix_matrix():
    j = jnp.arange(LANES)[:, None]
    s = jnp.arange(LANES)[None, :]
    u = jnp.concatenate([(j > s).astype(BF16), jnp.ones((LANES, LANES), BF16)], axis=1)
    return jnp.concatenate([u, u], axis=0)


def _inject_rows(ctx, nctx):
    nseq, _, c = ctx.shape
    body = jnp.concatenate([ctx[1:], jnp.zeros((1, nctx, c), F32)], axis=0)
    body = jnp.pad(body, ((0, 0), (SAMPLE_ROWS - nctx, 0), (0, 0))).reshape(nseq * SAMPLE_ROWS, c)
    head = jnp.pad(ctx[0], ((8 - nctx, 0), (0, 0)))
    return jnp.concatenate([head, body], axis=0)


def kernel(x_prompt, x_sample, c_prompt, c_sample, cache_k, cache_v, page_table, state_ssm, state_conv, state_ffn_conv, w_ada, b_ada, g_pre_mix, g_post_mix, g_pre_ffn, g_post_ffn, w_in, conv_w, conv_b, dt_bias, a_log, d_skip, g_ssd_norm, sb_bias, w_out, w_ffn_in, ffn_conv_w, ffn_conv_b, w_ffn_out):
    assert w_ada.shape[0] == 1, "one layer"
    nb, seq, d = x_prompt.shape
    ns, ls, _ = x_sample.shape
    n_pool = cache_k.shape[1]
    mp = nb * seq
    ms = ns * SAMPLE_ROWS

    w = w_in[0]
    o1, o2, o3, o4, o5 = 1024, 2560, 2576, 3600, 4624
    w_in_r = jnp.concatenate([w[:, :o1], w[:, o1:o1 + D_SSD], w[:, o3:o4], w[:, o4:o5], w[:, o5:],
                              w[:, o1 + D_SSD:o2], w[:, o2:o3], jnp.zeros((d, DT_PAD - SSD_HEADS), F32)],
                             axis=1).astype(BF16)
    w_out_b = w_out[0].astype(BF16)
    w_ffn_in_b = w_ffn_in[0].astype(BF16)
    w_ffn_out_b = w_ffn_out[0].astype(BF16)
    row = lambda v: v.reshape(1, -1)
    dtb = jnp.pad(dt_bias[0], (0, DT_PAD - SSD_HEADS)).reshape(1, DT_PAD)
    alog = jnp.pad(a_log[0], (0, DT_PAD - SSD_HEADS)).reshape(1, DT_PAD)
    dskip = jnp.repeat(d_skip[0], SSD_HEAD_DIM).reshape(1, D_SSD)
    gnorm = row(g_ssd_norm[0])
    cw, cb = conv_w[0], row(conv_b[0])
    fcw, fcb = ffn_conv_w[0], row(ffn_conv_b[0])
    uo = _suffix_matrix()

    n_c = nb + ns
    c_all = jnp.concatenate([c_prompt, c_sample, jnp.zeros((-n_c % 16, d), F32)], axis=0)
    mod = _ada(c_all, w_ada[0], row(b_ada[0]))
    mods_p = mod[:nb].reshape(nb, 6, 1, d)
    mods_s = jnp.repeat(mod[nb:n_c], SAMPLE_ROWS, axis=0).reshape(ms, 6, d).transpose(1, 0, 2)[None]

    tm_p = 1024
    tpg_p = seq // tm_p
    xp = x_prompt.reshape(mp, d)
    proj_p = _inproj(xp, mods_p, row(g_pre_mix[0]), w_in_r, tm_p, tpg_p)
    proj3 = proj_p.reshape(nb, seq, PROJ_W)
    tri, sel, emat = _ssd_constants(SSD_CHUNK)
    y_ssd, ssm_p = _ssd_prompt(proj3, cw, cb, dtb, alog, dskip, gnorm, tri, sel, emat)
    o_att = _attn_prompt(proj3, sb_bias[0], uo)
    tm_e = 512
    x1_p, h2_p = _outproj(y_ssd.reshape(mp, D_SSD), o_att.reshape(mp, D_ATT), xp, mods_p, row(g_post_mix[0]),
                          row(g_pre_ffn[0]), w_out_b, tm_e, seq // tm_e)
    yp, ctxg_p, ctxv_p = _ffn(h2_p, x1_p, mods_p, row(g_post_ffn[0]), w_ffn_in_b, w_ffn_out_b, fcw, fcb,
                              tm_e, seq // tm_e)

    xs_pad = jnp.pad(x_sample, ((0, 0), (0, SAMPLE_ROWS - ls), (0, 0))).reshape(ms, d)
    proj_s = _inproj(xs_pad, mods_s, row(g_pre_mix[0]), w_in_r, ms, 1)
    projs3 = proj_s.reshape(ns, SAMPLE_ROWS, PROJ_W)
    seq_per_tile = SSD_CHUNK // SAMPLE_ROWS
    conv_ctx = state_conv[0].reshape(ns // seq_per_tile, seq_per_tile, SSD_CONV - 1, -1)
    ctx_inj = jax.vmap(lambda c: _inject_rows(c, SSD_CONV - 1))(conv_ctx)
    tri_s, sel_s, _ = _ssd_constants(SAMPLE_ROWS)
    y_ssd_s, ssm_s = _ssd_sample(proj_s, ctx_inj, state_ssm[0].reshape(ns, D_SSD, SSD_STATE), cw, cb, dtb, alog,
                                 dskip, gnorm, tri_s, sel_s, emat)
    bias_rows = jnp.broadcast_to(jnp.repeat(sb_bias[0], SAMPLE_ROWS)[:, None], (ATT_HEADS * SAMPLE_ROWS, LANES))
    o_att_s = _attn_sample(page_table, projs3, cache_k[0].reshape(n_pool, PAGE_SIZE * ATT_HEADS, ATT_HEAD_DIM),
                           cache_v[0].reshape(n_pool, PAGE_SIZE * ATT_HEADS, ATT_HEAD_DIM), bias_rows, uo)
    x1_s, h2_s = _outproj(y_ssd_s, o_att_s.reshape(ms, D_ATT), xs_pad, mods_s, row(g_post_mix[0]),
                          row(g_pre_ffn[0]), w_out_b, ms, 1)
    inj_f = _inject_rows(state_ffn_conv[0], FFN_CONV - 1)
    ys_pad, ug_s, uv_s = _ffn(h2_s, x1_s, mods_s, row(g_post_ffn[0]), w_ffn_in_b, w_ffn_out_b, fcw, fcb,
                              ms, 1, inj=inj_f)

    heads = (ATT_HEADS, ATT_HEAD_DIM)
    k_p = proj3[:, :, COL_K:COL_K + D_ATT].reshape(1, nb, seq, *heads)
    v_p = proj3[:, :, COL_V:COL_V + D_ATT].reshape(1, nb, seq, *heads)
    tail = slice(seq - (SSD_CONV - 1), seq)
    conv_p = jnp.concatenate([proj3[:, tail, COL_XS:COL_XS + D_SSD], proj3[:, tail, COL_BC:COL_BC + BC_DIM]], axis=-1)[None]
    ffn_p = jnp.concatenate([ctxg_p, ctxv_p], axis=-1)[seq // tm_e - 1::seq // tm_e][None]
    k_s = projs3[:, :ls, COL_K:COL_K + D_ATT].reshape(1, ns, ls, *heads)
    v_s = projs3[:, :ls, COL_V:COL_V + D_ATT].reshape(1, ns, ls, *heads)
    raw_s = jnp.concatenate([projs3[:, :ls, COL_XS:COL_XS + D_SSD], projs3[:, :ls, COL_BC:COL_BC + BC_DIM]], axis=-1)
    conv_s = jnp.concatenate([state_conv[0], raw_s], axis=1)[:, -(SSD_CONV - 1):][None]
    u_s = jnp.concatenate([ug_s, uv_s], axis=-1).reshape(ns, SAMPLE_ROWS, 2 * D_FF)[:, :ls]
    ffn_s = jnp.concatenate([state_ffn_conv[0], u_s], axis=1)[:, -(FFN_CONV - 1):][None]
    return (yp.reshape(nb, seq, d), ys_pad.reshape(ns, SAMPLE_ROWS, d)[:, :ls],
            k_p, v_p, ssm_p.reshape(1, nb, SSD_HEADS, SSD_HEAD_DIM, SSD_STATE), conv_p, ffn_p,
            k_s, v_s, ssm_s.reshape(1, ns, SSD_HEADS, SSD_HEAD_DIM, SSD_STATE), conv_s, ffn_s)
```

```python
import functools

import jax
import jax.numpy as jnp
from jax import lax
from jax.experimental import pallas as pl
from jax.experimental.pallas import tpu as pltpu

F32 = jnp.float32
BF16 = jnp.bfloat16

D_MODEL = 2048
D_SSD = 1024
SSD_HEAD_DIM = 64
SSD_HEADS = 16
SSD_GROUPS = 2
SSD_STATE = 128
SSD_CONV = 4
SSD_CHUNK = 128
BC_DIM = 2 * SSD_GROUPS * SSD_STATE
D_ATT = 1024
ATT_HEAD_DIM = 128
ATT_HEADS = 8
D_FF = 5632
FFN_CONV = 3
PAGE_SIZE = 128
EPS = 1e-6
ATT_SCALE = ATT_HEAD_DIM ** -0.5

COL_Z, COL_XS, COL_Q, COL_K, COL_V, COL_BC, COL_DT = 0, 1024, 2048, 3072, 4096, 5120, 5632
PROJ_TN = 1024
PROJ_W = 6 * PROJ_TN
DT_PAD = 128

SAMPLE_ROWS = 8
LANES = 128
VMEM_LIMIT = 56 * 1024 * 1024


def _dot(a, b):
    return jnp.dot(a, b, preferred_element_type=F32)


def _dot_nt(a, b):
    return lax.dot_general(a, b, (((1,), (1,)), ((), ())), preferred_element_type=F32)


def _sigmoid(x):
    return 1.0 / (1.0 + jnp.exp(-x))


def _silu(x):
    return x * _sigmoid(x)


def _softplus(x):
    return jnp.maximum(x, 0.0) + jnp.log1p(jnp.exp(-jnp.abs(x)))


def _rms(x, g):
    ms = jnp.mean(x * x, axis=-1, keepdims=True)
    return x * lax.rsqrt(ms + EPS) * g


def _split3(x):
    h = x.astype(BF16)
    r = x - h.astype(F32)
    m = r.astype(BF16)
    l = (r - m.astype(F32)).astype(BF16)
    return h, m, l


def _dot3_right(x, mat):
    h, m, l = _split3(x)
    return _dot(h, mat) + _dot(m, mat) + _dot(l, mat)


def _dot3_left(mat, x):
    h, m, l = _split3(x)
    return _dot(mat, h) + _dot(mat, m) + _dot(mat, l)


def _ada_kernel(c_ref, w_ref, b_ref, o_ref):
    s = _silu(c_ref[...]).astype(BF16)
    o_ref[...] = _dot(s, w_ref[...].astype(BF16)) + b_ref[...]


def _ada(c_all, w_ada, b_ada):
    m, d = c_all.shape
    n = w_ada.shape[1]
    tn = 1024
    return pl.pallas_call(
        _ada_kernel,
        grid=(n // tn,),
        in_specs=[pl.BlockSpec((m, d), lambda j: (0, 0)),
                  pl.BlockSpec((d, tn), lambda j: (0, j)),
                  pl.BlockSpec((1, tn), lambda j: (0, j))],
        out_specs=pl.BlockSpec((m, tn), lambda j: (0, j)),
        out_shape=jax.ShapeDtypeStruct((m, n), F32),
        compiler_params=pltpu.CompilerParams(dimension_semantics=("arbitrary",), vmem_limit_bytes=VMEM_LIMIT),
        name="ada_mod",
    )(c_all, w_ada, b_ada)


def _mod_spec(idx, rows, tiles_per_group):
    return pl.BlockSpec((1, rows, D_MODEL), lambda i, *_: (i // tiles_per_group, 0, idx))


def _inproj_kernel(x_ref, sh_ref, sc_ref, g_ref, w_ref, o_ref, k_ref, v_ref, hn_ref):
    j = pl.program_id(1)
    tm = x_ref.shape[0]

    @pl.when(j == 0)
    def _():
        h = _rms(x_ref[...], g_ref[...]) * (1.0 + sc_ref[0]) + sh_ref[0]
        hn_ref[...] = h.astype(BF16)

    res = _dot(hn_ref[...], w_ref[...])
    o_ref[...] = res

    def by_token(dst):
        for h in range(ATT_HEADS):
            dst[pl.ds(h, tm, stride=ATT_HEADS), :] = res[:, h * ATT_HEAD_DIM:(h + 1) * ATT_HEAD_DIM]

    @pl.when(j == COL_K // PROJ_TN)
    def _():
        by_token(k_ref)

    @pl.when(j == COL_V // PROJ_TN)
    def _():
        by_token(v_ref)


def _inproj(x, mods, g, w, tm, tiles_per_group):
    m = x.shape[0]
    rows = mods.shape[1]
    kv_spec = pl.BlockSpec((tm * ATT_HEADS, ATT_HEAD_DIM), lambda i, j: (i, 0))
    kv_shape = jax.ShapeDtypeStruct((m * ATT_HEADS, ATT_HEAD_DIM), F32)
    return pl.pallas_call(
        _inproj_kernel,
        grid=(m // tm, PROJ_W // PROJ_TN),
        in_specs=[pl.BlockSpec((tm, D_MODEL), lambda i, j: (i, 0)),
                  _mod_spec(0, rows, tiles_per_group),
                  _mod_spec(1, rows, tiles_per_group),
                  pl.BlockSpec((1, D_MODEL), lambda i, j: (0, 0)),
                  pl.BlockSpec((D_MODEL, PROJ_TN), lambda i, j: (0, j))],
        out_specs=[pl.BlockSpec((tm, PROJ_TN), lambda i, j: (i, j)), kv_spec, kv_spec],
        out_shape=[jax.ShapeDtypeStruct((m, PROJ_W), F32), kv_shape, kv_shape],
        scratch_shapes=[pltpu.VMEM((tm, D_MODEL), BF16)],
        compiler_params=pltpu.CompilerParams(dimension_semantics=("arbitrary", "arbitrary"),
                                             vmem_limit_bytes=VMEM_LIMIT),
        name="in_proj",
    )(x, mods, mods, g, w)


def _conv4(buf_ref, w, b, rows):
    y = buf_ref[5:5 + rows, :] * w[0:1]
    y = y + buf_ref[6:6 + rows, :] * w[1:2]
    y = y + buf_ref[7:7 + rows, :] * w[2:3]
    y = y + buf_ref[8:8 + rows, :] * w[3:4]
    return y + b


def _ssd_tile(xs, bc, dtr, row_valid, causal, tri, sel, emat, dtb, alog):
    dt = _softplus(dtr + dtb)
    if row_valid is not None:
        dt = jnp.where(row_valid, dt, 0.0)
    a = -jnp.exp(alog)
    acum = _dot3_left(tri, dt * a)
    alast = _dot3_left(sel, acum)
    acx = _dot3_right(acum, emat)
    acum_x = acx[:, :D_SSD]
    acum_x2 = acx[:, D_SSD:]
    e1 = emat[:, :D_SSD]
    dt_x = _dot3_right(dt, e1)
    alast_x = _dot3_right(alast, e1)
    acum_t = acum.T

    xdt = xs * dt_x
    xdt_bf = xdt.astype(BF16)
    xdtd_bf = (xdt * jnp.exp(alast_x - acum_x)).astype(BF16)
    bm = bc[:, :SSD_GROUPS * SSD_STATE]
    cm = bc[:, SSD_GROUPS * SSD_STATE:]
    lane = lax.broadcasted_iota(jnp.int32, (SSD_CHUNK, LANES), 1)
    first_head = lane < SSD_HEAD_DIM

    pieces = []
    heads_per_group = SSD_HEADS // SSD_GROUPS
    for g in range(SSD_GROUPS):
        bg = bm[:, g * SSD_STATE:(g + 1) * SSD_STATE].astype(BF16)
        cg = cm[:, g * SSD_STATE:(g + 1) * SSD_STATE].astype(BF16)
        cb = _dot_nt(cg, bg)
        for pair in range(heads_per_group // 2):
            h0 = g * heads_per_group + 2 * pair
            ys = []
            for hh in (h0, h0 + 1):
                seg = acum_x2[:, hh * LANES:(hh + 1) * LANES] - acum_t[hh:hh + 1, :]
                decay = jnp.where(causal, jnp.exp(jnp.where(causal, seg, 0.0)), 0.0)
                sc = (cb * decay).astype(BF16)
                ys.append(_dot(sc, xdt_bf[:, (h0 // 2) * LANES:(h0 // 2 + 1) * LANES]))
            pieces.append(jnp.where(first_head, ys[0], ys[1]))
    y_diag = jnp.concatenate(pieces, axis=1)
    return y_diag, xdtd_bf, acum_x, bm, cm


def _ssd_finish(y, xs, z, dskip, gnorm):
    y = y + dskip * xs
    y = y * _silu(z)
    half = D_SSD // SSD_GROUPS
    outs = []
    for g in range(SSD_GROUPS):
        outs.append(_rms(y[:, g * half:(g + 1) * half], gnorm[:, g * half:(g + 1) * half]))
    return jnp.concatenate(outs, axis=1)


def _ssd_prompt_kernel(z_ref, xs_ref, bc_ref, dt_ref, cw_ref, cb_ref, dtb_ref, alog_ref, dskip_ref, gnorm_ref,
                       tri_ref, sel_ref, emat_ref, y_ref, st_ref, ht_ref, xsbuf, bcbuf):
    c = pl.program_id(1)
    rows = SSD_CHUNK

    @pl.when(c == 0)
    def _():
        ht_ref[...] = jnp.zeros_like(ht_ref)
        xsbuf[0:8, :] = jnp.zeros((8, D_SSD), F32)
        bcbuf[0:8, :] = jnp.zeros((8, BC_DIM), F32)

    xsbuf[8:8 + rows, :] = xs_ref[0]
    bcbuf[8:8 + rows, :] = bc_ref[0]
    cw = cw_ref[...]
    cb = cb_ref[...]
    xs = _silu(_conv4(xsbuf, cw[:, :D_SSD], cb[:, :D_SSD], rows))
    bc = _silu(_conv4(bcbuf, cw[:, D_SSD:], cb[:, D_SSD:], rows))
    xsbuf[0:8, :] = xsbuf[rows:rows + 8, :]
    bcbuf[0:8, :] = bcbuf[rows:rows + 8, :]

    r = lax.broadcasted_iota(jnp.int32, (rows, rows), 0)
    s = lax.broadcasted_iota(jnp.int32, (rows, rows), 1)
    causal = s <= r
    y_diag, xdtd_bf, acum_x, bm, cm = _ssd_tile(xs, bc, dt_ref[0], None, causal, tri_ref[...], sel_ref[...],
                                                 emat_ref[...], dtb_ref[...], alog_ref[...])
    ht = ht_ref[...]
    half = D_SSD // SSD_GROUPS
    y_off, states = [], []
    for g in range(SSD_GROUPS):
        bg = bm[:, g * SSD_STATE:(g + 1) * SSD_STATE]
        cg = cm[:, g * SSD_STATE:(g + 1) * SSD_STATE].astype(BF16)
        y_off.append(_dot(cg, ht[:, g * half:(g + 1) * half].astype(BF16)))
        states.append(_dot(bg.T.astype(BF16), xdtd_bf[:, g * half:(g + 1) * half]))
    y = y_diag + jnp.concatenate(y_off, axis=1) * jnp.exp(acum_x)
    ht_new = jnp.exp(acum_x[rows - 1:rows, :]) * ht + jnp.concatenate(states, axis=1)
    ht_ref[...] = ht_new
    y_ref[0] = _ssd_finish(y, xs, z_ref[0], dskip_ref[...], gnorm_ref[...]).astype(BF16)

    @pl.when(c == pl.num_programs(1) - 1)
    def _():
        for k in range(D_SSD // LANES):
            st_ref[0, k * LANES:(k + 1) * LANES, :] = ht_new[:, k * LANES:(k + 1) * LANES].T


def _const_spec(shape):
    nd = len(shape)
    return pl.BlockSpec(shape, lambda *_: (0,) * nd)


def _ssd_prompt(proj3, cw, cb, dtb, alog, dskip, gnorm, tri, sel, emat):
    nb, seq, _ = proj3.shape
    nc = seq // SSD_CHUNK
    return pl.pallas_call(
        _ssd_prompt_kernel,
        grid=(nb, nc),
        in_specs=[pl.BlockSpec((1, SSD_CHUNK, D_SSD), lambda b, c: (b, c, COL_Z // D_SSD)),
                  pl.BlockSpec((1, SSD_CHUNK, D_SSD), lambda b, c: (b, c, COL_XS // D_SSD)),
                  pl.BlockSpec((1, SSD_CHUNK, BC_DIM), lambda b, c: (b, c, COL_BC // BC_DIM)),
                  pl.BlockSpec((1, SSD_CHUNK, DT_PAD), lambda b, c: (b, c, COL_DT // DT_PAD)),
                  _const_spec(cw.shape), _const_spec(cb.shape), _const_spec(dtb.shape), _const_spec(alog.shape),
                  _const_spec(dskip.shape), _const_spec(gnorm.shape), _const_spec(tri.shape),
                  _const_spec(sel.shape), _const_spec(emat.shape)],
        out_specs=[pl.BlockSpec((1, SSD_CHUNK, D_SSD), lambda b, c: (b, c, 0)),
                   pl.BlockSpec((1, D_SSD, SSD_STATE), lambda b, c: (b, 0, 0))],
        out_shape=[jax.ShapeDtypeStruct((nb, seq, D_SSD), BF16),
                   jax.ShapeDtypeStruct((nb, D_SSD, SSD_STATE), F32)],
        scratch_shapes=[pltpu.VMEM((SSD_STATE, D_SSD), F32),
                        pltpu.VMEM((SSD_CHUNK + 8, D_SSD), F32),
                        pltpu.VMEM((SSD_CHUNK + 8, BC_DIM), F32)],
        compiler_params=pltpu.CompilerParams(dimension_semantics=("arbitrary", "arbitrary"),
                                             vmem_limit_bytes=VMEM_LIMIT),
        name="ssd_prompt",
    )(proj3, proj3, proj3, proj3, cw, cb, dtb, alog, dskip, gnorm, tri, sel, emat)


def _ssd_sample_kernel(z_ref, xs_ref, bc_ref, dt_ref, ctx_ref, h0_ref, cw_ref, cb_ref, dtb_ref, alog_ref, dskip_ref,
                       gnorm_ref, tri_ref, sel_ref, emat_ref, y_ref, st_ref,
                       xsbuf, bcbuf, ybase, yoff, xs_s, xdtd_s, acumx_s, bm_s, cm_s):
    s_id = pl.program_id(1)
    rows = SSD_CHUNK
    nseq = rows // SAMPLE_ROWS
    rr = lax.broadcasted_iota(jnp.int32, (rows, 1), 0)

    @pl.when(s_id == 0)
    def _():
        ctx = ctx_ref[0]
        is_ctx = (rr % SAMPLE_ROWS) >= SAMPLE_ROWS - (SSD_CONV - 1)
        xsbuf[0:8, :] = ctx[0:8, :D_SSD]
        bcbuf[0:8, :] = ctx[0:8, D_SSD:]
        xsbuf[8:8 + rows, :] = jnp.where(is_ctx, ctx[8:, :D_SSD], xs_ref[...])
        bcbuf[8:8 + rows, :] = jnp.where(is_ctx, ctx[8:, D_SSD:], bc_ref[...])
        cw = cw_ref[...]
        cb = cb_ref[...]
        xs = _silu(_conv4(xsbuf, cw[:, :D_SSD], cb[:, :D_SSD], rows))
        bc = _silu(_conv4(bcbuf, cw[:, D_SSD:], cb[:, D_SSD:], rows))
        r = lax.broadcasted_iota(jnp.int32, (rows, rows), 0)
        c = lax.broadcasted_iota(jnp.int32, (rows, rows), 1)
        causal = (c <= r) & ((c // SAMPLE_ROWS) == (r // SAMPLE_ROWS))
        row_valid = (rr % SAMPLE_ROWS) < (SAMPLE_ROWS // 2)
        y_diag, xdtd_bf, acum_x, bm, cm = _ssd_tile(xs, bc, dt_ref[...], row_valid, causal, tri_ref[...],
                                                     sel_ref[...], emat_ref[...], dtb_ref[...], alog_ref[...])
        ybase[...] = y_diag
        yoff[...] = jnp.zeros_like(yoff)
        xs_s[...] = xs
        xdtd_s[...] = xdtd_bf
        acumx_s[...] = acum_x
        bm_s[...] = bm
        cm_s[...] = cm

    in_seq_rows = (rr // SAMPLE_ROWS) == s_id
    in_seq_lanes = (lax.broadcasted_iota(jnp.int32, (1, rows), 1) // SAMPLE_ROWS) == s_id
    h0 = h0_ref[0]
    ht = jnp.concatenate([h0[k * LANES:(k + 1) * LANES, :].T for k in range(D_SSD // LANES)], axis=1)
    half = D_SSD // SSD_GROUPS
    y_off, states = [], []
    for g in range(SSD_GROUPS):
        bg_t = jnp.where(in_seq_lanes, bm_s[:, g * SSD_STATE:(g + 1) * SSD_STATE].T, 0.0).astype(BF16)
        cg = cm_s[:, g * SSD_STATE:(g + 1) * SSD_STATE].astype(BF16)
        y_off.append(_dot(cg, ht[:, g * half:(g + 1) * half].astype(BF16)))
        states.append(_dot(bg_t, xdtd_s[:, g * half:(g + 1) * half]))
    yoff[...] = yoff[...] + jnp.where(in_seq_rows, jnp.concatenate(y_off, axis=1), 0.0)
    last = pl.multiple_of(s_id * SAMPLE_ROWS, SAMPLE_ROWS) + (SAMPLE_ROWS - 1)
    ht_new = jnp.exp(acumx_s[pl.ds(last, 1), :]) * ht + jnp.concatenate(states, axis=1)
    for k in range(D_SSD // LANES):
        st_ref[0, k * LANES:(k + 1) * LANES, :] = ht_new[:, k * LANES:(k + 1) * LANES].T

    @pl.when(s_id == nseq - 1)
    def _():
        y = ybase[...] + yoff[...] * jnp.exp(acumx_s[...])
        y_ref[...] = _ssd_finish(y, xs_s[...], z_ref[...], dskip_ref[...], gnorm_ref[...]).astype(BF16)


def _ssd_sample(proj_s, ctx, h0, cw, cb, dtb, alog, dskip, gnorm, tri, sel, emat):
    m = proj_s.shape[0]
    nt = m // SSD_CHUNK
    nseq = SSD_CHUNK // SAMPLE_ROWS
    return pl.pallas_call(
        _ssd_sample_kernel,
        grid=(nt, nseq),
        in_specs=[pl.BlockSpec((SSD_CHUNK, D_SSD), lambda t, s: (t, COL_Z // D_SSD)),
                  pl.BlockSpec((SSD_CHUNK, D_SSD), lambda t, s: (t, COL_XS // D_SSD)),
                  pl.BlockSpec((SSD_CHUNK, BC_DIM), lambda t, s: (t, COL_BC // BC_DIM)),
                  pl.BlockSpec((SSD_CHUNK, DT_PAD), lambda t, s: (t, COL_DT // DT_PAD)),
                  pl.BlockSpec((1, SSD_CHUNK + 8, D_SSD + BC_DIM), lambda t, s: (t, 0, 0)),
                  pl.BlockSpec((1, D_SSD, SSD_STATE), lambda t, s: (t * nseq + s, 0, 0)),
                  _const_spec(cw.shape), _const_spec(cb.shape), _const_spec(dtb.shape), _const_spec(alog.shape),
                  _const_spec(dskip.shape), _const_spec(gnorm.shape), _const_spec(tri.shape),
                  _const_spec(sel.shape), _const_spec(emat.shape)],
        out_specs=[pl.BlockSpec((SSD_CHUNK, D_SSD), lambda t, s: (t, 0)),
                   pl.BlockSpec((1, D_SSD, SSD_STATE), lambda t, s: (t * nseq + s, 0, 0))],
        out_shape=[jax.ShapeDtypeStruct((m, D_SSD), BF16),
                   jax.ShapeDtypeStruct(h0.shape, F32)],
        scratch_shapes=[pltpu.VMEM((SSD_CHUNK + 8, D_SSD), F32),
                        pltpu.VMEM((SSD_CHUNK + 8, BC_DIM), F32),
                        pltpu.VMEM((SSD_CHUNK, D_SSD), F32),
                        pltpu.VMEM((SSD_CHUNK, D_SSD), F32),
                        pltpu.VMEM((SSD_CHUNK, D_SSD), F32),
                        pltpu.VMEM((SSD_CHUNK, D_SSD), BF16),
                        pltpu.VMEM((SSD_CHUNK, D_SSD), F32),
                        pltpu.VMEM((SSD_CHUNK, SSD_GROUPS * SSD_STATE), F32),
                        pltpu.VMEM((SSD_CHUNK, SSD_GROUPS * SSD_STATE), F32)],
        compiler_params=pltpu.CompilerParams(dimension_semantics=("arbitrary", "arbitrary"),
                                             vmem_limit_bytes=VMEM_LIMIT),
        name="ssd_sample",
    )(proj_s, proj_s, proj_s, proj_s, ctx, h0, cw, cb, dtb, alog, dskip, gnorm, tri, sel, emat)


def _sb_block(qb, kb, vb, bias, uo, run, mask, nearest_first):
    z = _dot_nt(qb, kb) * ATT_SCALE + bias
    lg = jnp.log(1.0 + jnp.exp(-jnp.abs(z)))
    ls_pos = jnp.minimum(z, 0.0) - lg
    lm = jnp.minimum(-z, 0.0) - lg
    if mask is not None:
        lm = jnp.where(mask, lm, 0.0)
    hi = lm.astype(BF16)
    lo = (lm - hi.astype(F32)).astype(BF16)
    nsub = z.shape[1] // LANES
    tails = [None] * nsub
    for c in (range(nsub) if nearest_first else reversed(range(nsub))):
        cols = slice(c * LANES, (c + 1) * LANES)
        st = _dot(jnp.concatenate([hi[:, cols], lo[:, cols]], axis=1), uo)
        tails[c] = run + st[:, :LANES]
        run = run + st[:, LANES:]
    w = jnp.exp(ls_pos + jnp.concatenate(tails, axis=1))
    if mask is not None:
        w = jnp.where(mask, w, 0.0)
    return _dot(w.astype(BF16), vb), run


ATT_TQ = 256
ATT_TK = 512


def _attn_prompt_kernel(bias_ref, q_ref, k_ref, v_ref, uo_ref, o_ref, kbf, vbf):
    h = pl.program_id(1)
    qi = pl.program_id(2)

    @pl.when(qi == 0)
    def _():
        kbf[...] = k_ref[0].astype(BF16)
        vbf[...] = v_ref[0].astype(BF16)

    qb = q_ref[0].astype(BF16)
    bias = bias_ref[h]
    uo = uo_ref[...]
    q0 = qi * ATT_TQ
    qpos = q0 + lax.broadcasted_iota(jnp.int32, (ATT_TQ, ATT_TK), 0)
    col = lax.broadcasted_iota(jnp.int32, (ATT_TQ, ATT_TK), 1)
    last = q0 // ATT_TK

    def block(kb, run, masked):
        start = pl.multiple_of(kb * ATT_TK, ATT_TK)
        mask = (start + col < qpos) if masked else None
        return _sb_block(qb, kbf[pl.ds(start, ATT_TK), :], vbf[pl.ds(start, ATT_TK), :], bias, uo, run, mask, False)

    acc, run = block(last, jnp.zeros((ATT_TQ, LANES), F32), True)

    def body(i, carry):
        acc, run = carry
        pv, run = block(last - 1 - i, run, False)
        return acc + pv, run

    acc, _ = lax.fori_loop(0, last, body, (acc, run))
    o_ref[0] = acc.astype(BF16)


def _attn_prompt(proj3, sb_bias, uo):
    nb, seq, _ = proj3.shape
    tq = ATT_TQ
    return pl.pallas_call(
        _attn_prompt_kernel,
        grid=(nb, ATT_HEADS, seq // tq),
        in_specs=[pl.BlockSpec(memory_space=pltpu.SMEM),
                  pl.BlockSpec((1, tq, ATT_HEAD_DIM), lambda b, h, i: (b, i, COL_Q // ATT_HEAD_DIM + h)),
                  pl.BlockSpec((1, seq, ATT_HEAD_DIM), lambda b, h, i: (b, 0, COL_K // ATT_HEAD_DIM + h)),
                  pl.BlockSpec((1, seq, ATT_HEAD_DIM), lambda b, h, i: (b, 0, COL_V // ATT_HEAD_DIM + h)),
                  _const_spec(uo.shape)],
        out_specs=pl.BlockSpec((1, tq, ATT_HEAD_DIM), lambda b, h, i: (b, i, h)),
        out_shape=jax.ShapeDtypeStruct((nb, seq, D_ATT), BF16),
        scratch_shapes=[pltpu.VMEM((seq, ATT_HEAD_DIM), BF16), pltpu.VMEM((seq, ATT_HEAD_DIM), BF16)],
        compiler_params=pltpu.CompilerParams(dimension_semantics=("arbitrary", "arbitrary", "arbitrary"),
                                             vmem_limit_bytes=VMEM_LIMIT),
        name="attn_prompt",
    )(sb_bias, proj3, proj3, proj3, uo)


PAGES_PER_STEP = 8


def _page_by_head(ref):
    heads = [ref[0, pl.ds(h, PAGE_SIZE, stride=ATT_HEADS), :] for h in range(ATT_HEADS)]
    return jnp.concatenate(heads, axis=1).astype(BF16)


def _attn_sample_kernel(pt_ref, q_ref, kn_ref, vn_ref, bias_ref, uo_ref, *rest):
    k_refs = rest[:PAGES_PER_STEP]
    v_refs = rest[PAGES_PER_STEP:2 * PAGES_PER_STEP]
    o_ref, qbd, knew, vnew, acc, run = rest[2 * PAGES_PER_STEP:]
    b = pl.program_id(0)
    j = pl.program_id(1)
    nrow = ATT_HEADS * SAMPLE_ROWS
    bias = bias_ref[...]
    uo = uo_ref[...]

    @pl.when((b == 0) & (j == 0))
    def _():
        knew[...] = jnp.zeros_like(knew)
        vnew[...] = jnp.zeros_like(vnew)

    @pl.when(j == 0)
    def _():
        q8 = q_ref[0]
        rr = lax.broadcasted_iota(jnp.int32, (nrow, D_ATT), 0)
        cc = lax.broadcasted_iota(jnp.int32, (nrow, D_ATT), 1)
        qt = jnp.concatenate([q8] * ATT_HEADS, axis=0)
        qbd[...] = jnp.where((rr // SAMPLE_ROWS) == (cc // ATT_HEAD_DIM), qt, 0.0).astype(BF16)
        knew[0:SAMPLE_ROWS, :] = kn_ref[0]
        vnew[0:SAMPLE_ROWS, :] = vn_ref[0]
        qq = lax.broadcasted_iota(jnp.int32, (nrow, LANES), 0) % SAMPLE_ROWS
        kk = lax.broadcasted_iota(jnp.int32, (nrow, LANES), 1)
        mask = (kk < qq) & (kk < SAMPLE_ROWS // 2)
        pv, r = _sb_block(qbd[...], knew[...].astype(BF16), vnew[...].astype(BF16), bias, uo,
                          jnp.zeros((nrow, LANES), F32), mask, True)
        acc[...] = pv
        run[...] = r

    kcat = jnp.concatenate([_page_by_head(k_refs[g]) for g in range(PAGES_PER_STEP)], axis=0)
    vcat = jnp.concatenate([_page_by_head(v_refs[g]) for g in range(PAGES_PER_STEP)], axis=0)
    pv, r = _sb_block(qbd[...], kcat, vcat, jnp.concatenate([bias] * PAGES_PER_STEP, axis=1), uo, run[...], None, True)
    a = acc[...] + pv
    acc[...] = a
    run[...] = r

    @pl.when(j == pl.num_programs(1) - 1)
    def _():
        for h in range(ATT_HEADS):
            o_ref[0, :, h * ATT_HEAD_DIM:(h + 1) * ATT_HEAD_DIM] = (
                a[h * SAMPLE_ROWS:(h + 1) * SAMPLE_ROWS, h * ATT_HEAD_DIM:(h + 1) * ATT_HEAD_DIM])


def _attn_sample(page_table, projs3, cache_k, cache_v, bias_rows, uo):
    nseq, npages = page_table.shape
    steps = npages // PAGES_PER_STEP

    def page_spec(g):
        return pl.BlockSpec((1, PAGE_SIZE * ATT_HEADS, ATT_HEAD_DIM),
                            lambda b, j, pt: (pt[b, npages - 1 - (j * PAGES_PER_STEP + g)], 0, 0))

    grid_spec = pltpu.PrefetchScalarGridSpec(
        num_scalar_prefetch=1,
        grid=(nseq, steps),
        in_specs=[pl.BlockSpec((1, SAMPLE_ROWS, D_ATT), lambda b, j, pt: (b, 0, COL_Q // D_ATT)),
                  pl.BlockSpec((1, SAMPLE_ROWS, D_ATT), lambda b, j, pt: (b, 0, COL_K // D_ATT)),
                  pl.BlockSpec((1, SAMPLE_ROWS, D_ATT), lambda b, j, pt: (b, 0, COL_V // D_ATT)),
                  pl.BlockSpec(bias_rows.shape, lambda b, j, pt: (0, 0)),
                  pl.BlockSpec(uo.shape, lambda b, j, pt: (0, 0))]
                 + [page_spec(g) for g in range(PAGES_PER_STEP)] * 2,
        out_specs=pl.BlockSpec((1, SAMPLE_ROWS, D_ATT), lambda b, j, pt: (b, 0, 0)),
        scratch_shapes=[pltpu.VMEM((ATT_HEADS * SAMPLE_ROWS, D_ATT), BF16),
                        pltpu.VMEM((PAGE_SIZE, D_ATT), F32),
                        pltpu.VMEM((PAGE_SIZE, D_ATT), F32),
                        pltpu.VMEM((ATT_HEADS * SAMPLE_ROWS, D_ATT), F32),
                        pltpu.VMEM((ATT_HEADS * SAMPLE_ROWS, LANES), F32)])
    return pl.pallas_call(
        _attn_sample_kernel,
        grid_spec=grid_spec,
        out_shape=jax.ShapeDtypeStruct((nseq, SAMPLE_ROWS, D_ATT), F32),
        compiler_params=pltpu.CompilerParams(dimension_semantics=("arbitrary", "arbitrary"),
                                             vmem_limit_bytes=VMEM_LIMIT),
        name="attn_sample",
    )(page_table, projs3, projs3, projs3, bias_rows, uo, *([cache_k] * PAGES_PER_STEP), *([cache_v] * PAGES_PER_STEP))


def _outproj_kernel(y_ref, o_ref, x_ref, g1_ref, sh2_ref, sc2_ref, gpost_ref, gpre_ref, wy_ref, wo_ref, x1_ref, h2_ref):
    m = _dot(y_ref[...], wy_ref[...]) + _dot(o_ref[...].astype(BF16), wo_ref[...])
    x1 = x_ref[...] + g1_ref[0] * _rms(m, gpost_ref[...])
    x1_ref[...] = x1
    h2 = _rms(x1, gpre_ref[...]) * (1.0 + sc2_ref[0]) + sh2_ref[0]
    h2_ref[...] = h2.astype(BF16)


def _outproj(y, o, x, mods, gpost, gpre, w_out, tm, tiles_per_group):
    m = x.shape[0]
    rows = mods.shape[1]
    half = w_out.shape[0] // 2
    row = lambda i: (i, 0)
    return pl.pallas_call(
        _outproj_kernel,
        grid=(m // tm,),
        in_specs=[pl.BlockSpec((tm, D_SSD), row), pl.BlockSpec((tm, D_ATT), row), pl.BlockSpec((tm, D_MODEL), row),
                  _mod_spec(2, rows, tiles_per_group), _mod_spec(3, rows, tiles_per_group),
                  _mod_spec(4, rows, tiles_per_group),
                  _const_spec(gpost.shape), _const_spec(gpre.shape),
                  pl.BlockSpec((half, D_MODEL), lambda i: (0, 0)),
                  pl.BlockSpec((half, D_MODEL), lambda i: (1, 0))],
        out_specs=[pl.BlockSpec((tm, D_MODEL), row), pl.BlockSpec((tm, D_MODEL), row)],
        out_shape=[jax.ShapeDtypeStruct((m, D_MODEL), F32), jax.ShapeDtypeStruct((m, D_MODEL), BF16)],
        compiler_params=pltpu.CompilerParams(dimension_semantics=("arbitrary",), vmem_limit_bytes=VMEM_LIMIT),
        name="out_proj",
    )(y, o, x, mods, mods, mods, gpost, gpre, w_out, w_out)


FFN_CHUNK = 256


def _shift_rows(u, prev8, s):
    r = pltpu.roll(u, s, 0)
    head = jnp.where(lax.broadcasted_iota(jnp.int32, (8, 1), 0) < s, pltpu.roll(prev8, s, 0), r[:8])
    return jnp.concatenate([head, r[8:]], axis=0)


def _conv_rows(u, prev8, w, b):
    taps = w.shape[0]
    y = _shift_rows(u, prev8, taps - 1) * w[0:1]
    for t in range(1, taps):
        s = taps - 1 - t
        y = y + (_shift_rows(u, prev8, s) if s else u) * w[t:t + 1]
    return y + b


def _ffn_kernel(sample, tiles_per_group, *refs):
    if sample:
        (h_ref, x1_ref, g2_ref, gpost_ref, wg_ref, wv_ref, wo_ref, cwg_ref, cwv_ref, cbg_ref, cbv_ref,
         injg_ref, injv_ref, y_ref, ug_ref, uv_ref, acc) = refs
    else:
        (h_ref, x1_ref, g2_ref, gpost_ref, wg_ref, wv_ref, wo_ref, cwg_ref, cwv_ref, cbg_ref, cbv_ref,
         y_ref, ug_ref, uv_ref, acc, halog, halov) = refs
    i = pl.program_id(0)
    j = pl.program_id(1)
    tm = h_ref.shape[0]
    tf = wg_ref.shape[1]
    hn = h_ref[...]

    @pl.when(j == 0)
    def _():
        acc[...] = jnp.zeros_like(acc)

    if sample:
        is_ctx = (lax.broadcasted_iota(jnp.int32, (tm, 1), 0) % SAMPLE_ROWS) >= SAMPLE_ROWS - (FFN_CONV - 1)
    else:
        @pl.when((i == 0) & (j == 0))
        def _():
            halog[...] = jnp.zeros_like(halog)
            halov[...] = jnp.zeros_like(halov)

        first = lax.rem(i, tiles_per_group) == 0

    def half(w_ref, cw_ref, cb_ref, u_ref, inj_ref, halo, cols):
        u = _dot(hn, w_ref[:, cols])
        if sample:
            u_ref[:, cols] = u
            prev = inj_ref[0:8, cols]
            u = jnp.where(is_ctx, inj_ref[8:, cols], u)
        else:
            prev = jnp.where(first, 0.0, halo[j, :, cols])
            halo[j, :, cols] = u[tm - 8:, :]
            u_ref[0, :, cols] = u[tm - 8:, :]
        return _conv_rows(u, prev, cw_ref[:, cols], cb_ref[:, cols])

    acts = []
    for c in range(tf // FFN_CHUNK):
        cols = slice(c * FFN_CHUNK, (c + 1) * FFN_CHUNK)
        gate = half(wg_ref, cwg_ref, cbg_ref, ug_ref, None if not sample else injg_ref, None if sample else halog, cols)
        val = half(wv_ref, cwv_ref, cbv_ref, uv_ref, None if not sample else injv_ref, None if sample else halov, cols)
        acts.append((_silu(gate) * val).astype(BF16))
    acc[...] += _dot(jnp.concatenate(acts, axis=1), wo_ref[...])

    @pl.when(j == pl.num_programs(1) - 1)
    def _():
        y_ref[...] = x1_ref[...] + g2_ref[0] * _rms(acc[...], gpost_ref[...])


def _ffn(h2, x1, mods, gpost, w_in, w_out, cw, cb, tm, tiles_per_group, inj=None):
    m = h2.shape[0]
    rows = mods.shape[1]
    tf = 512
    nf = D_FF // tf
    sample = inj is not None
    row = lambda i, j: (i, 0)
    gate_col = lambda i, j: (0, j)
    val_col = lambda i, j: (0, nf + j)
    in_specs = [pl.BlockSpec((tm, D_MODEL), row), pl.BlockSpec((tm, D_MODEL), row),
                _mod_spec(5, rows, tiles_per_group), _const_spec(gpost.shape),
                pl.BlockSpec((D_MODEL, tf), gate_col), pl.BlockSpec((D_MODEL, tf), val_col),
                pl.BlockSpec((tf, D_MODEL), lambda i, j: (j, 0)),
                pl.BlockSpec((FFN_CONV, tf), gate_col), pl.BlockSpec((FFN_CONV, tf), val_col),
                pl.BlockSpec((1, tf), gate_col), pl.BlockSpec((1, tf), val_col)]
    args = [h2, x1, mods, gpost, w_in, w_in, w_out, cw, cw, cb, cb]
    scratch = [pltpu.VMEM((tm, D_MODEL), F32)]
    if sample:
        in_specs += [pl.BlockSpec((m + 8, tf), gate_col), pl.BlockSpec((m + 8, tf), val_col)]
        args += [inj, inj]
        u_specs = [pl.BlockSpec((tm, tf), lambda i, j: (i, j))] * 2
        u_shapes = [jax.ShapeDtypeStruct((m, D_FF), F32)] * 2
    else:
        u_specs = [pl.BlockSpec((1, 8, tf), lambda i, j: (i, 0, j))] * 2
        u_shapes = [jax.ShapeDtypeStruct((m // tm, 8, D_FF), F32)] * 2
        scratch += [pltpu.VMEM((nf, 8, tf), F32), pltpu.VMEM((nf, 8, tf), F32)]
    return pl.pallas_call(
        functools.partial(_ffn_kernel, sample, tiles_per_group),
        grid=(m // tm, nf),
        in_specs=in_specs,
        out_specs=[pl.BlockSpec((tm, D_MODEL), row)] + u_specs,
        out_shape=[jax.ShapeDtypeStruct((m, D_MODEL), F32)] + u_shapes,
        scratch_shapes=scratch,
        compiler_params=pltpu.CompilerParams(dimension_semantics=("arbitrary", "arbitrary"),
                                             vmem_limit_bytes=VMEM_LIMIT),
        name="ffn_sample" if sample else "ffn_prompt",
    )(*args)


def _ssd_constants(block_rows):
    r = jnp.arange(SSD_CHUNK)[:, None]
    c = jnp.arange(SSD_CHUNK)[None, :]
    same = (r // block_rows) == (c // block_rows)
    tri = ((c <= r) & same).astype(BF16)
    sel = (c == (r // block_rows) * block_rows + block_rows - 1).astype(BF16)
    hrow = jnp.arange(LANES)[:, None]
    e1 = (hrow == jnp.arange(D_SSD)[None, :] // SSD_HEAD_DIM).astype(BF16)
    e2 = (hrow == jnp.arange(SSD_HEADS * LANES)[None, :] // LANES).astype(BF16)
    return tri, sel, jnp.concatenate([e1, e2], axis=1)


def _suffix_matrix():
    j = jnp.arange(LANES)[:, None]
    s = jnp.arange(LANES)[None, :]
    u = jnp.concatenate([(j > s).astype(BF16), jnp.ones((LANES, LANES), BF16)], axis=1)
    return jnp.concatenate([u, u], axis=0)


def _inject_rows(ctx, nctx):
    nseq, _, c = ctx.shape
    body = jnp.concatenate([ctx[1:], jnp.zeros((1, nctx, c), F32)], axis=0)
    body = jnp.pad(body, ((0, 0), (SAMPLE_ROWS - nctx, 0), (0, 0))).reshape(nseq * SAMPLE_ROWS, c)
    head = jnp.pad(ctx[0], ((8 - nctx, 0), (0, 0)))
    return jnp.concatenate([head, body], axis=0)


def kernel(x_prompt, x_sample, c_prompt, c_sample, cache_k, cache_v, page_table, state_ssm, state_conv, state_ffn_conv, w_ada, b_ada, g_pre_mix, g_post_mix, g_pre_ffn, g_post_ffn, w_in, conv_w, conv_b, dt_bias, a_log, d_skip, g_ssd_norm, sb_bias, w_out, w_ffn_in, ffn_conv_w, ffn_conv_b, w_ffn_out):
    assert w_ada.shape[0] == 1, "one layer"
    nb, seq, d = x_prompt.shape
    ns, ls, _ = x_sample.shape
    n_pool = cache_k.shape[1]
    mp = nb * seq
    ms = ns * SAMPLE_ROWS

    w = w_in[0].astype(BF16)
    o1, o2, o3 = 1024, 2560, 2576
    w_in_r = jnp.concatenate([w[:, :o1 + D_SSD], w[:, o3:], w[:, o1 + D_SSD:o3],
                              jnp.zeros((d, PROJ_W - w.shape[1]), BF16)], axis=1)
    w_out_b = w_out[0].astype(BF16)
    w_ffn_in_b = w_ffn_in[0].astype(BF16)
    w_ffn_out_b = w_ffn_out[0].astype(BF16)
    row = lambda v: v.reshape(1, -1)
    dtb = jnp.pad(dt_bias[0], (0, DT_PAD - SSD_HEADS)).reshape(1, DT_PAD)
    alog = jnp.pad(a_log[0], (0, DT_PAD - SSD_HEADS)).reshape(1, DT_PAD)
    dskip = jnp.repeat(d_skip[0], SSD_HEAD_DIM).reshape(1, D_SSD)
    gnorm = row(g_ssd_norm[0])
    cw, cb = conv_w[0], row(conv_b[0])
    fcw, fcb = ffn_conv_w[0], row(ffn_conv_b[0])
    uo = _suffix_matrix()

    n_c = nb + ns
    c_all = jnp.concatenate([c_prompt, c_sample, jnp.zeros((-n_c % 16, d), F32)], axis=0)
    mod = _ada(c_all, w_ada[0], row(b_ada[0]))
    mods_p = mod[:nb].reshape(nb, 1, 6 * d)
    mods_s = jnp.repeat(mod[nb:n_c], SAMPLE_ROWS, axis=0).reshape(1, ms, 6 * d)

    tm_p = 512
    tpg_p = seq // tm_p
    xp = x_prompt.reshape(mp, d)
    proj_p, k_p, v_p = _inproj(xp, mods_p, row(g_pre_mix[0]), w_in_r, tm_p, tpg_p)
    proj3 = proj_p.reshape(nb, seq, PROJ_W)
    tri, sel, emat = _ssd_constants(SSD_CHUNK)
    y_ssd, ssm_p = _ssd_prompt(proj3, cw, cb, dtb, alog, dskip, gnorm, tri, sel, emat)
    o_att = _attn_prompt(proj3, sb_bias[0], uo)
    tm_e = 512
    x1_p, h2_p = _outproj(y_ssd.reshape(mp, D_SSD), o_att.reshape(mp, D_ATT), xp, mods_p, row(g_post_mix[0]),
                          row(g_pre_ffn[0]), w_out_b, tm_e, seq // tm_e)
    yp, ctxg_p, ctxv_p = _ffn(h2_p, x1_p, mods_p, row(g_post_ffn[0]), w_ffn_in_b, w_ffn_out_b, fcw, fcb,
                              tm_e, seq // tm_e)

    xs_pad = jnp.pad(x_sample, ((0, 0), (0, SAMPLE_ROWS - ls), (0, 0))).reshape(ms, d)
    proj_s, k_s, v_s = _inproj(xs_pad, mods_s, row(g_pre_mix[0]), w_in_r, ms, 1)
    projs3 = proj_s.reshape(ns, SAMPLE_ROWS, PROJ_W)
    seq_per_tile = SSD_CHUNK // SAMPLE_ROWS
    conv_ctx = state_conv[0].reshape(ns // seq_per_tile, seq_per_tile, SSD_CONV - 1, -1)
    ctx_inj = jax.vmap(lambda c: _inject_rows(c, SSD_CONV - 1))(conv_ctx)
    tri_s, sel_s, _ = _ssd_constants(SAMPLE_ROWS)
    y_ssd_s, ssm_s = _ssd_sample(proj_s, ctx_inj, state_ssm[0].reshape(ns, D_SSD, SSD_STATE), cw, cb, dtb, alog,
                                 dskip, gnorm, tri_s, sel_s, emat)
    bias_rows = jnp.broadcast_to(jnp.repeat(sb_bias[0], SAMPLE_ROWS)[:, None], (ATT_HEADS * SAMPLE_ROWS, LANES))
    o_att_s = _attn_sample(page_table, projs3, cache_k[0].reshape(n_pool, PAGE_SIZE * ATT_HEADS, ATT_HEAD_DIM),
                           cache_v[0].reshape(n_pool, PAGE_SIZE * ATT_HEADS, ATT_HEAD_DIM), bias_rows, uo)
    x1_s, h2_s = _outproj(y_ssd_s, o_att_s.reshape(ms, D_ATT), xs_pad, mods_s, row(g_post_mix[0]),
                          row(g_pre_ffn[0]), w_out_b, ms, 1)
    inj_f = _inject_rows(state_ffn_conv[0], FFN_CONV - 1)
    ys_pad, ug_s, uv_s = _ffn(h2_s, x1_s, mods_s, row(g_post_ffn[0]), w_ffn_in_b, w_ffn_out_b, fcw, fcb,
                              ms, 1, inj=inj_f)

    heads = (ATT_HEADS, ATT_HEAD_DIM)
    k_p = k_p.reshape(1, nb, seq, *heads)
    v_p = v_p.reshape(1, nb, seq, *heads)
    tail = slice(seq - (SSD_CONV - 1), seq)
    conv_p = jnp.concatenate([proj3[:, tail, COL_XS:COL_XS + D_SSD], proj3[:, tail, COL_BC:COL_BC + BC_DIM]], axis=-1)[None]
    ffn_p = jnp.concatenate([ctxg_p, ctxv_p], axis=-1)[seq // tm_e - 1::seq // tm_e, 8 - (FFN_CONV - 1):][None]
    k_s = k_s.reshape(ns, SAMPLE_ROWS, *heads)[None, :, :ls]
    v_s = v_s.reshape(ns, SAMPLE_ROWS, *heads)[None, :, :ls]
    raw_s = jnp.concatenate([projs3[:, :ls, COL_XS:COL_XS + D_SSD], projs3[:, :ls, COL_BC:COL_BC + BC_DIM]], axis=-1)
    conv_s = jnp.concatenate([state_conv[0], raw_s], axis=1)[:, -(SSD_CONV - 1):][None]
    u_s = jnp.concatenate([ug_s, uv_s], axis=-1).reshape(ns, SAMPLE_ROWS, 2 * D_FF)[:, :ls]
    ffn_s = jnp.concatenate([state_ffn_conv[0], u_s], axis=1)[:, -(FFN_CONV - 1):][None]
    return (yp.reshape(nb, seq, d), ys_pad.reshape(ns, SAMPLE_ROWS, d)[:, :ls],
            k_p, v_p, ssm_p.reshape(1, nb, SSD_HEADS, SSD_HEAD_DIM, SSD_STATE), conv_p, ffn_p,
            k_s, v_s, ssm_s.reshape(1, ns, SSD_HEADS, SSD_HEAD_DIM, SSD_STATE), conv_s, ffn_s)
```

```python
import functools

import jax
import jax.numpy as jnp
from jax import lax
from jax.experimental import pallas as pl
from jax.experimental.pallas import tpu as pltpu

F32 = jnp.float32
BF16 = jnp.bfloat16

D_MODEL = 2048
D_SSD = 1024
SSD_HEAD_DIM = 64
SSD_HEADS = 16
SSD_GROUPS = 2
SSD_STATE = 128
SSD_CONV = 4
SSD_CHUNK = 128
BC_DIM = 2 * SSD_GROUPS * SSD_STATE
D_ATT = 1024
ATT_HEAD_DIM = 128
ATT_HEADS = 8
D_FF = 5632
FFN_CONV = 3
PAGE_SIZE = 128
EPS = 1e-6
ATT_SCALE = ATT_HEAD_DIM ** -0.5

PROJ_TN = 1024
PROJ_MAIN = 5 * PROJ_TN
DT_PAD = 128
PROJ_TAIL = BC_DIM + DT_PAD

SAMPLE_ROWS = 8
LANES = 128
VMEM_LIMIT = 56 * 1024 * 1024


def _dot(a, b):
    return jnp.dot(a, b, preferred_element_type=F32)


def _dot_nt(a, b):
    return lax.dot_general(a, b, (((1,), (1,)), ((), ())), preferred_element_type=F32)


def _sigmoid(x):
    return 1.0 / (1.0 + jnp.exp(-x))


def _silu(x):
    return x * _sigmoid(x)


def _softplus(x):
    return jnp.maximum(x, 0.0) + jnp.log1p(jnp.exp(-jnp.abs(x)))


def _rms(x, g):
    ms = jnp.mean(x * x, axis=-1, keepdims=True)
    return x * lax.rsqrt(ms + EPS) * g


def _split3(x):
    h = x.astype(BF16)
    r = x - h.astype(F32)
    m = r.astype(BF16)
    l = (r - m.astype(F32)).astype(BF16)
    return h, m, l


def _dot3_right(x, mat):
    h, m, l = _split3(x)
    return _dot(h, mat) + _dot(m, mat) + _dot(l, mat)


def _dot3_left(mat, x):
    h, m, l = _split3(x)
    return _dot(mat, h) + _dot(mat, m) + _dot(mat, l)


def _ada_kernel(c_ref, w_ref, b_ref, o_ref):
    s = _silu(c_ref[...]).astype(BF16)
    o_ref[...] = _dot(s, w_ref[...].astype(BF16)) + b_ref[...]


def _ada(c_all, w_ada, b_ada):
    m, d = c_all.shape
    n = w_ada.shape[1]
    tn = 1024
    return pl.pallas_call(
        _ada_kernel,
        grid=(n // tn,),
        in_specs=[pl.BlockSpec((m, d), lambda j: (0, 0)),
                  pl.BlockSpec((d, tn), lambda j: (0, j)),
                  pl.BlockSpec((1, tn), lambda j: (0, j))],
        out_specs=pl.BlockSpec((m, tn), lambda j: (0, j)),
        out_shape=jax.ShapeDtypeStruct((m, n), F32),
        compiler_params=pltpu.CompilerParams(dimension_semantics=("arbitrary",), vmem_limit_bytes=VMEM_LIMIT),
        name="ada_mod",
    )(c_all, w_ada, b_ada)


def _mod_spec(idx, rows, tiles_per_group):
    return pl.BlockSpec((1, rows, D_MODEL), lambda i, *_: (i // tiles_per_group, 0, idx))


def _inproj_kernel(x_ref, sh_ref, sc_ref, g_ref, w_ref, wt_ref, zx_ref, qkv_ref, tail_ref, k_ref, v_ref, hn_ref):
    j = pl.program_id(1)
    tm = x_ref.shape[0]

    @pl.when(j == 0)
    def _():
        h = _rms(x_ref[...], g_ref[...]) * (1.0 + sc_ref[0]) + sh_ref[0]
        hn_ref[...] = h.astype(BF16)

    @pl.when(j < 5)
    def _():
        res = _dot(hn_ref[...], w_ref[...])

        def by_token(dst):
            for h in range(ATT_HEADS):
                dst[pl.ds(h, tm, stride=ATT_HEADS), :] = res[:, h * ATT_HEAD_DIM:(h + 1) * ATT_HEAD_DIM]

        @pl.when(j < 2)
        def _():
            zx_ref[...] = res

        @pl.when(j >= 2)
        def _():
            qkv_ref[...] = res.astype(qkv_ref.dtype)

        @pl.when(j == 3)
        def _():
            by_token(k_ref)

        @pl.when(j == 4)
        def _():
            by_token(v_ref)

    @pl.when(j == 5)
    def _():
        tail_ref[...] = _dot(hn_ref[...], wt_ref[...])


def _inproj(x, mods, g, w, wt, tm, tiles_per_group, qkv_dtype):
    m = x.shape[0]
    rows = mods.shape[1]
    kv_spec = pl.BlockSpec((tm * ATT_HEADS, ATT_HEAD_DIM), lambda i, j: (i, 0))
    kv_shape = jax.ShapeDtypeStruct((m * ATT_HEADS, ATT_HEAD_DIM), F32)
    return pl.pallas_call(
        _inproj_kernel,
        grid=(m // tm, PROJ_MAIN // PROJ_TN + 1),
        in_specs=[pl.BlockSpec((tm, D_MODEL), lambda i, j: (i, 0)),
                  _mod_spec(0, rows, tiles_per_group),
                  _mod_spec(1, rows, tiles_per_group),
                  pl.BlockSpec((1, D_MODEL), lambda i, j: (0, 0)),
                  pl.BlockSpec((D_MODEL, PROJ_TN), lambda i, j: (0, jnp.minimum(j, 4))),
                  pl.BlockSpec((D_MODEL, PROJ_TAIL), lambda i, j: (0, 0))],
        out_specs=[pl.BlockSpec((tm, PROJ_TN), lambda i, j: (i, jnp.minimum(j, 1))),
                   pl.BlockSpec((tm, PROJ_TN), lambda i, j: (i, jnp.clip(j - 2, 0, 2))),
                   pl.BlockSpec((tm, PROJ_TAIL), lambda i, j: (i, 0)),
                   kv_spec, kv_spec],
        out_shape=[jax.ShapeDtypeStruct((m, 2 * PROJ_TN), F32),
                   jax.ShapeDtypeStruct((m, 3 * PROJ_TN), qkv_dtype),
                   jax.ShapeDtypeStruct((m, PROJ_TAIL), F32), kv_shape, kv_shape],
        scratch_shapes=[pltpu.VMEM((tm, D_MODEL), BF16)],
        compiler_params=pltpu.CompilerParams(dimension_semantics=("arbitrary", "arbitrary"),
                                             vmem_limit_bytes=VMEM_LIMIT),
        name="in_proj",
    )(x, mods, mods, g, w, wt)


def _shift_rows(u, prev8, s):
    r = pltpu.roll(u, s, 0)
    head = jnp.where(lax.broadcasted_iota(jnp.int32, (8, 1), 0) < s, pltpu.roll(prev8, s, 0), r[:8])
    return jnp.concatenate([head, r[8:]], axis=0)


def _conv_rows(u, prev8, w, b):
    taps = w.shape[0]
    y = _shift_rows(u, prev8, taps - 1) * w[0:1]
    for t in range(1, taps):
        s = taps - 1 - t
        y = y + (_shift_rows(u, prev8, s) if s else u) * w[t:t + 1]
    return y + b


def _ssd_tile(xs, bc, dtr, row_valid, causal, tri, sel, emat, dtb, alog):
    dt = _softplus(dtr + dtb)
    if row_valid is not None:
        dt = jnp.where(row_valid, dt, 0.0)
    a = -jnp.exp(alog)
    acum = _dot3_left(tri, dt * a)
    alast = _dot3_left(sel, acum)
    acx = _dot3_right(acum, emat)
    acum_x = acx[:, :D_SSD]
    acum_x2 = acx[:, D_SSD:]
    e1 = emat[:, :D_SSD]
    dt_x = _dot3_right(dt, e1)
    alast_x = _dot3_right(alast, e1)
    acum_t = acum.T

    xdt = xs * dt_x
    xdt_bf = xdt.astype(BF16)
    xdtd_bf = (xdt * jnp.exp(alast_x - acum_x)).astype(BF16)
    bm = bc[:, :SSD_GROUPS * SSD_STATE]
    cm = bc[:, SSD_GROUPS * SSD_STATE:]
    lane = lax.broadcasted_iota(jnp.int32, (SSD_CHUNK, LANES), 1)
    first_head = lane < SSD_HEAD_DIM

    pieces = []
    heads_per_group = SSD_HEADS // SSD_GROUPS
    for g in range(SSD_GROUPS):
        bg = bm[:, g * SSD_STATE:(g + 1) * SSD_STATE].astype(BF16)
        cg = cm[:, g * SSD_STATE:(g + 1) * SSD_STATE].astype(BF16)
        cb = _dot_nt(cg, bg)
        for pair in range(heads_per_group // 2):
            h0 = g * heads_per_group + 2 * pair
            ys = []
            for hh in (h0, h0 + 1):
                seg = acum_x2[:, hh * LANES:(hh + 1) * LANES] - acum_t[hh:hh + 1, :]
                decay = jnp.where(causal, jnp.exp(jnp.where(causal, seg, 0.0)), 0.0)
                sc = (cb * decay).astype(BF16)
                ys.append(_dot(sc, xdt_bf[:, (h0 // 2) * LANES:(h0 // 2 + 1) * LANES]))
            pieces.append(jnp.where(first_head, ys[0], ys[1]))
    y_diag = jnp.concatenate(pieces, axis=1)
    return y_diag, xdtd_bf, acum_x, bm, cm


def _ssd_finish(y, xs, z, dskip, gnorm):
    y = y + dskip * xs
    y = y * _silu(z)
    half = D_SSD // SSD_GROUPS
    outs = []
    for g in range(SSD_GROUPS):
        outs.append(_rms(y[:, g * half:(g + 1) * half], gnorm[:, g * half:(g + 1) * half]))
    return jnp.concatenate(outs, axis=1)


def _ssd_prompt_kernel(z_ref, xs_ref, bc_ref, dt_ref, cw_ref, cb_ref, dtb_ref, alog_ref, dskip_ref, gnorm_ref,
                       tri_ref, sel_ref, emat_ref, y_ref, st_ref, ht_ref, prev_xs, prev_bc):
    c = pl.program_id(1)
    rows = SSD_CHUNK

    @pl.when(c == 0)
    def _():
        ht_ref[...] = jnp.zeros_like(ht_ref)
        prev_xs[...] = jnp.zeros_like(prev_xs)
        prev_bc[...] = jnp.zeros_like(prev_bc)

    xr = xs_ref[0]
    br = bc_ref[0]
    cw = cw_ref[...]
    cb = cb_ref[...]
    xs = _silu(_conv_rows(xr, prev_xs[...], cw[:, :D_SSD], cb[:, :D_SSD]))
    bc = _silu(_conv_rows(br, prev_bc[...], cw[:, D_SSD:], cb[:, D_SSD:]))
    prev_xs[...] = xr[rows - 8:, :]
    prev_bc[...] = br[rows - 8:, :]

    r = lax.broadcasted_iota(jnp.int32, (rows, rows), 0)
    s = lax.broadcasted_iota(jnp.int32, (rows, rows), 1)
    causal = s <= r
    y_diag, xdtd_bf, acum_x, bm, cm = _ssd_tile(xs, bc, dt_ref[0], None, causal, tri_ref[...], sel_ref[...],
                                                 emat_ref[...], dtb_ref[...], alog_ref[...])
    ht = ht_ref[...]
    half = D_SSD // SSD_GROUPS
    y_off, states = [], []
    for g in range(SSD_GROUPS):
        bg = bm[:, g * SSD_STATE:(g + 1) * SSD_STATE]
        cg = cm[:, g * SSD_STATE:(g + 1) * SSD_STATE].astype(BF16)
        y_off.append(_dot(cg, ht[:, g * half:(g + 1) * half].astype(BF16)))
        states.append(_dot(bg.T.astype(BF16), xdtd_bf[:, g * half:(g + 1) * half]))
    y = y_diag + jnp.concatenate(y_off, axis=1) * jnp.exp(acum_x)
    ht_new = jnp.exp(acum_x[rows - 1:rows, :]) * ht + jnp.concatenate(states, axis=1)
    ht_ref[...] = ht_new
    y_ref[0] = _ssd_finish(y, xs, z_ref[0], dskip_ref[...], gnorm_ref[...]).astype(BF16)

    @pl.when(c == pl.num_programs(1) - 1)
    def _():
        for k in range(D_SSD // LANES):
            st_ref[0, k * LANES:(k + 1) * LANES, :] = ht_new[:, k * LANES:(k + 1) * LANES].T


def _const_spec(shape):
    nd = len(shape)
    return pl.BlockSpec(shape, lambda *_: (0,) * nd)


def _ssd_prompt(zx3, tail3, cw, cb, dtb, alog, dskip, gnorm, tri, sel, emat):
    nb, seq, _ = zx3.shape
    nc = seq // SSD_CHUNK
    return pl.pallas_call(
        _ssd_prompt_kernel,
        grid=(nb, nc),
        in_specs=[pl.BlockSpec((1, SSD_CHUNK, D_SSD), lambda b, c: (b, c, 0)),
                  pl.BlockSpec((1, SSD_CHUNK, D_SSD), lambda b, c: (b, c, 1)),
                  pl.BlockSpec((1, SSD_CHUNK, BC_DIM), lambda b, c: (b, c, 0)),
                  pl.BlockSpec((1, SSD_CHUNK, DT_PAD), lambda b, c: (b, c, BC_DIM // DT_PAD)),
                  _const_spec(cw.shape), _const_spec(cb.shape), _const_spec(dtb.shape), _const_spec(alog.shape),
                  _const_spec(dskip.shape), _const_spec(gnorm.shape), _const_spec(tri.shape),
                  _const_spec(sel.shape), _const_spec(emat.shape)],
        out_specs=[pl.BlockSpec((1, SSD_CHUNK, D_SSD), lambda b, c: (b, c, 0)),
                   pl.BlockSpec((1, D_SSD, SSD_STATE), lambda b, c: (b, 0, 0))],
        out_shape=[jax.ShapeDtypeStruct((nb, seq, D_SSD), BF16),
                   jax.ShapeDtypeStruct((nb, D_SSD, SSD_STATE), F32)],
        scratch_shapes=[pltpu.VMEM((SSD_STATE, D_SSD), F32),
                        pltpu.VMEM((8, D_SSD), F32),
                        pltpu.VMEM((8, BC_DIM), F32)],
        compiler_params=pltpu.CompilerParams(dimension_semantics=("arbitrary", "arbitrary"),
                                             vmem_limit_bytes=VMEM_LIMIT),
        name="ssd_prompt",
    )(zx3, zx3, tail3, tail3, cw, cb, dtb, alog, dskip, gnorm, tri, sel, emat)


def _ssd_sample_kernel(z_ref, xs_ref, bc_ref, dt_ref, ctx_ref, h0_ref, cw_ref, cb_ref, dtb_ref, alog_ref, dskip_ref,
                       gnorm_ref, tri_ref, sel_ref, emat_ref, y_ref, st_ref,
                       ybase, yoff, xs_s, xdtd_s, acumx_s, bm_s, cm_s):
    s_id = pl.program_id(1)
    rows = SSD_CHUNK
    nseq = rows // SAMPLE_ROWS
    rr = lax.broadcasted_iota(jnp.int32, (rows, 1), 0)

    @pl.when(s_id == 0)
    def _():
        ctx = ctx_ref[0]
        is_ctx = (rr % SAMPLE_ROWS) >= SAMPLE_ROWS - (SSD_CONV - 1)
        xr = jnp.where(is_ctx, ctx[8:, :D_SSD], xs_ref[...])
        br = jnp.where(is_ctx, ctx[8:, D_SSD:], bc_ref[...])
        cw = cw_ref[...]
        cb = cb_ref[...]
        xs = _silu(_conv_rows(xr, ctx[0:8, :D_SSD], cw[:, :D_SSD], cb[:, :D_SSD]))
        bc = _silu(_conv_rows(br, ctx[0:8, D_SSD:], cw[:, D_SSD:], cb[:, D_SSD:]))
        r = lax.broadcasted_iota(jnp.int32, (rows, rows), 0)
        c = lax.broadcasted_iota(jnp.int32, (rows, rows), 1)
        causal = (c <= r) & ((c // SAMPLE_ROWS) == (r // SAMPLE_ROWS))
        row_valid = (rr % SAMPLE_ROWS) < (SAMPLE_ROWS // 2)
        y_diag, xdtd_bf, acum_x, bm, cm = _ssd_tile(xs, bc, dt_ref[...], row_valid, causal, tri_ref[...],
                                                     sel_ref[...], emat_ref[...], dtb_ref[...], alog_ref[...])
        ybase[...] = y_diag
        yoff[...] = jnp.zeros_like(yoff)
        xs_s[...] = xs
        xdtd_s[...] = xdtd_bf
        acumx_s[...] = acum_x
        bm_s[...] = bm
        cm_s[...] = cm

    in_seq_rows = (rr // SAMPLE_ROWS) == s_id
    in_seq_lanes = (lax.broadcasted_iota(jnp.int32, (1, rows), 1) // SAMPLE_ROWS) == s_id
    h0 = h0_ref[0]
    ht = jnp.concatenate([h0[k * LANES:(k + 1) * LANES, :].T for k in range(D_SSD // LANES)], axis=1)
    half = D_SSD // SSD_GROUPS
    y_off, states = [], []
    for g in range(SSD_GROUPS):
        bg_t = jnp.where(in_seq_lanes, bm_s[:, g * SSD_STATE:(g + 1) * SSD_STATE].T, 0.0).astype(BF16)
        cg = cm_s[:, g * SSD_STATE:(g + 1) * SSD_STATE].astype(BF16)
        y_off.append(_dot(cg, ht[:, g * half:(g + 1) * half].astype(BF16)))
        states.append(_dot(bg_t, xdtd_s[:, g * half:(g + 1) * half]))
    yoff[...] = yoff[...] + jnp.where(in_seq_rows, jnp.concatenate(y_off, axis=1), 0.0)
    last = pl.multiple_of(s_id * SAMPLE_ROWS, SAMPLE_ROWS) + (SAMPLE_ROWS - 1)
    ht_new = jnp.exp(acumx_s[pl.ds(last, 1), :]) * ht + jnp.concatenate(states, axis=1)
    for k in range(D_SSD // LANES):
        st_ref[0, k * LANES:(k + 1) * LANES, :] = ht_new[:, k * LANES:(k + 1) * LANES].T

    @pl.when(s_id == nseq - 1)
    def _():
        y = ybase[...] + yoff[...] * jnp.exp(acumx_s[...])
        y_ref[...] = _ssd_finish(y, xs_s[...], z_ref[...], dskip_ref[...], gnorm_ref[...]).astype(BF16)


def _ssd_sample(zx, tail, ctx, h0, cw, cb, dtb, alog, dskip, gnorm, tri, sel, emat):
    m = zx.shape[0]
    nt = m // SSD_CHUNK
    nseq = SSD_CHUNK // SAMPLE_ROWS
    return pl.pallas_call(
        _ssd_sample_kernel,
        grid=(nt, nseq),
        in_specs=[pl.BlockSpec((SSD_CHUNK, D_SSD), lambda t, s: (t, 0)),
                  pl.BlockSpec((SSD_CHUNK, D_SSD), lambda t, s: (t, 1)),
                  pl.BlockSpec((SSD_CHUNK, BC_DIM), lambda t, s: (t, 0)),
                  pl.BlockSpec((SSD_CHUNK, DT_PAD), lambda t, s: (t, BC_DIM // DT_PAD)),
                  pl.BlockSpec((1, SSD_CHUNK + 8, D_SSD + BC_DIM), lambda t, s: (t, 0, 0)),
                  pl.BlockSpec((1, D_SSD, SSD_STATE), lambda t, s: (t * nseq + s, 0, 0)),
                  _const_spec(cw.shape), _const_spec(cb.shape), _const_spec(dtb.shape), _const_spec(alog.shape),
                  _const_spec(dskip.shape), _const_spec(gnorm.shape), _const_spec(tri.shape),
                  _const_spec(sel.shape), _const_spec(emat.shape)],
        out_specs=[pl.BlockSpec((SSD_CHUNK, D_SSD), lambda t, s: (t, 0)),
                   pl.BlockSpec((1, D_SSD, SSD_STATE), lambda t, s: (t * nseq + s, 0, 0))],
        out_shape=[jax.ShapeDtypeStruct((m, D_SSD), BF16),
                   jax.ShapeDtypeStruct(h0.shape, F32)],
        scratch_shapes=[pltpu.VMEM((SSD_CHUNK, D_SSD), F32),
                        pltpu.VMEM((SSD_CHUNK, D_SSD), F32),
                        pltpu.VMEM((SSD_CHUNK, D_SSD), F32),
                        pltpu.VMEM((SSD_CHUNK, D_SSD), BF16),
                        pltpu.VMEM((SSD_CHUNK, D_SSD), F32),
                        pltpu.VMEM((SSD_CHUNK, SSD_GROUPS * SSD_STATE), F32),
                        pltpu.VMEM((SSD_CHUNK, SSD_GROUPS * SSD_STATE), F32)],
        compiler_params=pltpu.CompilerParams(dimension_semantics=("arbitrary", "arbitrary"),
                                             vmem_limit_bytes=VMEM_LIMIT),
        name="ssd_sample",
    )(zx, zx, tail, tail, ctx, h0, cw, cb, dtb, alog, dskip, gnorm, tri, sel, emat)


def _sb_block(qb, kb, vb, bias, uo, run, mask, nearest_first):
    z = _dot_nt(qb, kb) * ATT_SCALE + bias
    lg = jnp.log(1.0 + jnp.exp(-jnp.abs(z)))
    ls_pos = jnp.minimum(z, 0.0) - lg
    lm = jnp.minimum(-z, 0.0) - lg
    if mask is not None:
        lm = jnp.where(mask, lm, 0.0)
    hi = lm.astype(BF16)
    lo = (lm - hi.astype(F32)).astype(BF16)
    nsub = z.shape[1] // LANES
    tails = [None] * nsub
    for c in (range(nsub) if nearest_first else reversed(range(nsub))):
        cols = slice(c * LANES, (c + 1) * LANES)
        st = _dot(jnp.concatenate([hi[:, cols], lo[:, cols]], axis=1), uo)
        tails[c] = run + st[:, :LANES]
        run = run + st[:, LANES:]
    w = jnp.exp(ls_pos + jnp.concatenate(tails, axis=1))
    if mask is not None:
        w = jnp.where(mask, w, 0.0)
    return _dot(w.astype(BF16), vb), run


ATT_TQ = 512
ATT_TK = ATT_TQ


def _attn_prompt_kernel(bias_ref, q_ref, k_ref, v_ref, uo_ref, o_ref):
    h = pl.program_id(1)
    qi = pl.program_id(2)
    qb = q_ref[0]
    kbf = k_ref.at[0]
    vbf = v_ref.at[0]
    bias = bias_ref[h]
    uo = uo_ref[...]
    q0 = pl.multiple_of(qi * ATT_TQ, ATT_TQ)
    accs, runs = [], []
    for r in range(ATT_TQ // LANES):
        nk = (r + 1) * LANES
        row = lax.broadcasted_iota(jnp.int32, (LANES, nk), 0) + r * LANES
        col = lax.broadcasted_iota(jnp.int32, (LANES, nk), 1)
        a, rn = _sb_block(qb[r * LANES:(r + 1) * LANES], kbf[pl.ds(q0, nk), :], vbf[pl.ds(q0, nk), :], bias, uo,
                          jnp.zeros((LANES, LANES), F32), col < row, False)
        accs.append(a)
        runs.append(rn)
    acc = jnp.concatenate(accs, axis=0)
    run = jnp.concatenate(runs, axis=0)

    def body(i, carry):
        acc, run = carry
        start = pl.multiple_of((qi - 1 - i) * ATT_TK, ATT_TK)
        pv, run = _sb_block(qb, kbf[pl.ds(start, ATT_TK), :], vbf[pl.ds(start, ATT_TK), :], bias, uo, run, None, False)
        return acc + pv, run

    acc, _ = lax.fori_loop(0, qi, body, (acc, run))
    o_ref[0] = acc.astype(BF16)


def _attn_prompt(qkv3, sb_bias, uo):
    nb, seq, _ = qkv3.shape
    tq = ATT_TQ
    return pl.pallas_call(
        _attn_prompt_kernel,
        grid=(nb, ATT_HEADS, seq // tq),
        in_specs=[pl.BlockSpec(memory_space=pltpu.SMEM),
                  pl.BlockSpec((1, tq, ATT_HEAD_DIM), lambda b, h, i: (b, i, h)),
                  pl.BlockSpec((1, seq, ATT_HEAD_DIM), lambda b, h, i: (b, 0, ATT_HEADS + h)),
                  pl.BlockSpec((1, seq, ATT_HEAD_DIM), lambda b, h, i: (b, 0, 2 * ATT_HEADS + h)),
                  _const_spec(uo.shape)],
        out_specs=pl.BlockSpec((1, tq, ATT_HEAD_DIM), lambda b, h, i: (b, i, h)),
        out_shape=jax.ShapeDtypeStruct((nb, seq, D_ATT), BF16),
        compiler_params=pltpu.CompilerParams(dimension_semantics=("arbitrary", "arbitrary", "arbitrary"),
                                             vmem_limit_bytes=VMEM_LIMIT),
        name="attn_prompt",
    )(sb_bias, qkv3, qkv3, qkv3, uo)


PAGES_PER_STEP = 8


def _page_by_head(ref):
    heads = [ref[0, pl.ds(h, PAGE_SIZE, stride=ATT_HEADS), :] for h in range(ATT_HEADS)]
    return jnp.concatenate(heads, axis=1).astype(BF16)


def _attn_sample_kernel(pt_ref, q_ref, kn_ref, vn_ref, bias_ref, uo_ref, *rest):
    k_refs = rest[:PAGES_PER_STEP]
    v_refs = rest[PAGES_PER_STEP:2 * PAGES_PER_STEP]
    o_ref, qbd, knew, vnew, acc, run = rest[2 * PAGES_PER_STEP:]
    b = pl.program_id(0)
    j = pl.program_id(1)
    nrow = ATT_HEADS * SAMPLE_ROWS
    bias = bias_ref[...]
    uo = uo_ref[...]

    @pl.when((b == 0) & (j == 0))
    def _():
        knew[...] = jnp.zeros_like(knew)
        vnew[...] = jnp.zeros_like(vnew)

    @pl.when(j == 0)
    def _():
        q8 = q_ref[0]
        rr = lax.broadcasted_iota(jnp.int32, (nrow, D_ATT), 0)
        cc = lax.broadcasted_iota(jnp.int32, (nrow, D_ATT), 1)
        qt = jnp.concatenate([q8] * ATT_HEADS, axis=0)
        qbd[...] = jnp.where((rr // SAMPLE_ROWS) == (cc // ATT_HEAD_DIM), qt, 0.0).astype(BF16)
        knew[0:SAMPLE_ROWS, :] = kn_ref[0]
        vnew[0:SAMPLE_ROWS, :] = vn_ref[0]
        qq = lax.broadcasted_iota(jnp.int32, (nrow, LANES), 0) % SAMPLE_ROWS
        kk = lax.broadcasted_iota(jnp.int32, (nrow, LANES), 1)
        mask = (kk < qq) & (kk < SAMPLE_ROWS // 2)
        pv, r = _sb_block(qbd[...], knew[...].astype(BF16), vnew[...].astype(BF16), bias, uo,
                          jnp.zeros((nrow, LANES), F32), mask, True)
        acc[...] = pv
        run[...] = r

    kcat = jnp.concatenate([_page_by_head(k_refs[g]) for g in range(PAGES_PER_STEP)], axis=0)
    vcat = jnp.concatenate([_page_by_head(v_refs[g]) for g in range(PAGES_PER_STEP)], axis=0)
    pv, r = _sb_block(qbd[...], kcat, vcat, jnp.concatenate([bias] * PAGES_PER_STEP, axis=1), uo, run[...], None, True)
    a = acc[...] + pv
    acc[...] = a
    run[...] = r

    @pl.when(j == pl.num_programs(1) - 1)
    def _():
        for h in range(ATT_HEADS):
            o_ref[0, :, h * ATT_HEAD_DIM:(h + 1) * ATT_HEAD_DIM] = (
                a[h * SAMPLE_ROWS:(h + 1) * SAMPLE_ROWS, h * ATT_HEAD_DIM:(h + 1) * ATT_HEAD_DIM])


def _attn_sample(page_table, projs3, cache_k, cache_v, bias_rows, uo):
    nseq, npages = page_table.shape
    steps = npages // PAGES_PER_STEP

    def page_spec(g):
        return pl.BlockSpec((1, PAGE_SIZE * ATT_HEADS, ATT_HEAD_DIM),
                            lambda b, j, pt: (pt[b, npages - 1 - (j * PAGES_PER_STEP + g)], 0, 0))

    grid_spec = pltpu.PrefetchScalarGridSpec(
        num_scalar_prefetch=1,
        grid=(nseq, steps),
        in_specs=[pl.BlockSpec((1, SAMPLE_ROWS, D_ATT), lambda b, j, pt: (b, 0, 0)),
                  pl.BlockSpec((1, SAMPLE_ROWS, D_ATT), lambda b, j, pt: (b, 0, 1)),
                  pl.BlockSpec((1, SAMPLE_ROWS, D_ATT), lambda b, j, pt: (b, 0, 2)),
                  pl.BlockSpec(bias_rows.shape, lambda b, j, pt: (0, 0)),
                  pl.BlockSpec(uo.shape, lambda b, j, pt: (0, 0))]
                 + [page_spec(g) for g in range(PAGES_PER_STEP)] * 2,
        out_specs=pl.BlockSpec((1, SAMPLE_ROWS, D_ATT), lambda b, j, pt: (b, 0, 0)),
        scratch_shapes=[pltpu.VMEM((ATT_HEADS * SAMPLE_ROWS, D_ATT), BF16),
                        pltpu.VMEM((PAGE_SIZE, D_ATT), F32),
                        pltpu.VMEM((PAGE_SIZE, D_ATT), F32),
                        pltpu.VMEM((ATT_HEADS * SAMPLE_ROWS, D_ATT), F32),
                        pltpu.VMEM((ATT_HEADS * SAMPLE_ROWS, LANES), F32)])
    return pl.pallas_call(
        _attn_sample_kernel,
        grid_spec=grid_spec,
        out_shape=jax.ShapeDtypeStruct((nseq, SAMPLE_ROWS, D_ATT), F32),
        compiler_params=pltpu.CompilerParams(dimension_semantics=("arbitrary", "arbitrary"),
                                             vmem_limit_bytes=VMEM_LIMIT),
        name="attn_sample",
    )(page_table, projs3, projs3, projs3, bias_rows, uo, *([cache_k] * PAGES_PER_STEP), *([cache_v] * PAGES_PER_STEP))


def _outproj_kernel(y_ref, o_ref, x_ref, g1_ref, sh2_ref, sc2_ref, gpost_ref, gpre_ref, wy_ref, wo_ref, x1_ref, h2_ref):
    m = _dot(y_ref[...], wy_ref[...]) + _dot(o_ref[...].astype(BF16), wo_ref[...])
    x1 = x_ref[...] + g1_ref[0] * _rms(m, gpost_ref[...])
    x1_ref[...] = x1
    h2 = _rms(x1, gpre_ref[...]) * (1.0 + sc2_ref[0]) + sh2_ref[0]
    h2_ref[...] = h2.astype(BF16)


def _outproj(y, o, x, mods, gpost, gpre, w_out, tm, tiles_per_group):
    m = x.shape[0]
    rows = mods.shape[1]
    half = w_out.shape[0] // 2
    row = lambda i: (i, 0)
    return pl.pallas_call(
        _outproj_kernel,
        grid=(m // tm,),
        in_specs=[pl.BlockSpec((tm, D_SSD), row), pl.BlockSpec((tm, D_ATT), row), pl.BlockSpec((tm, D_MODEL), row),
                  _mod_spec(2, rows, tiles_per_group), _mod_spec(3, rows, tiles_per_group),
                  _mod_spec(4, rows, tiles_per_group),
                  _const_spec(gpost.shape), _const_spec(gpre.shape),
                  pl.BlockSpec((half, D_MODEL), lambda i: (0, 0)),
                  pl.BlockSpec((half, D_MODEL), lambda i: (1, 0))],
        out_specs=[pl.BlockSpec((tm, D_MODEL), row), pl.BlockSpec((tm, D_MODEL), row)],
        out_shape=[jax.ShapeDtypeStruct((m, D_MODEL), F32), jax.ShapeDtypeStruct((m, D_MODEL), BF16)],
        compiler_params=pltpu.CompilerParams(dimension_semantics=("arbitrary",), vmem_limit_bytes=VMEM_LIMIT),
        name="out_proj",
    )(y, o, x, mods, mods, mods, gpost, gpre, w_out, w_out)


FFN_CHUNK = 256


def _ffn_kernel(sample, tiles_per_group, *refs):
    if sample:
        (h_ref, x1_ref, g2_ref, gpost_ref, wg_ref, wv_ref, wo_ref, cwg_ref, cwv_ref, cbg_ref, cbv_ref,
         injg_ref, injv_ref, y_ref, ug_ref, uv_ref, acc) = refs
    else:
        (h_ref, x1_ref, g2_ref, gpost_ref, wg_ref, wv_ref, wo_ref, cwg_ref, cwv_ref, cbg_ref, cbv_ref,
         y_ref, ug_ref, uv_ref, acc, halog, halov) = refs
    i = pl.program_id(0)
    j = pl.program_id(1)
    tm = h_ref.shape[0]
    tf = wg_ref.shape[1]
    hn = h_ref[...]

    @pl.when(j == 0)
    def _():
        acc[...] = jnp.zeros_like(acc)

    if sample:
        is_ctx = (lax.broadcasted_iota(jnp.int32, (tm, 1), 0) % SAMPLE_ROWS) >= SAMPLE_ROWS - (FFN_CONV - 1)
    else:
        @pl.when((i == 0) & (j == 0))
        def _():
            halog[...] = jnp.zeros_like(halog)
            halov[...] = jnp.zeros_like(halov)

        first = lax.rem(i, tiles_per_group) == 0

    def half(w_ref, cw_ref, cb_ref, u_ref, inj_ref, halo, cols):
        u = jnp.concatenate([_dot(hn[r:r + 128], w_ref[:, cols]) for r in range(0, tm, 128)], axis=0)
        if sample:
            u_ref[:, cols] = u
            prev = inj_ref[0:8, cols]
            u = jnp.where(is_ctx, inj_ref[8:, cols], u)
        else:
            prev = jnp.where(first, 0.0, halo[j, :, cols])
            halo[j, :, cols] = u[tm - 8:, :]
            u_ref[0, :, cols] = u[tm - 8:, :]
        return _conv_rows(u, prev, cw_ref[:, cols], cb_ref[:, cols])

    acts = []
    for c in range(tf // FFN_CHUNK):
        cols = slice(c * FFN_CHUNK, (c + 1) * FFN_CHUNK)
        gate = half(wg_ref, cwg_ref, cbg_ref, ug_ref, None if not sample else injg_ref, None if sample else halog, cols)
        val = half(wv_ref, cwv_ref, cbv_ref, uv_ref, None if not sample else injv_ref, None if sample else halov, cols)
        acts.append((_silu(gate) * val).astype(BF16))
    acc[...] += _dot(jnp.concatenate(acts, axis=1), wo_ref[...])

    @pl.when(j == pl.num_programs(1) - 1)
    def _():
        y_ref[...] = x1_ref[...] + g2_ref[0] * _rms(acc[...], gpost_ref[...])


def _ffn(h2, x1, mods, gpost, w_in, w_out, cw, cb, tm, tiles_per_group, inj=None):
    m = h2.shape[0]
    rows = mods.shape[1]
    tf = 512
    nf = D_FF // tf
    sample = inj is not None
    row = lambda i, j: (i, 0)
    gate_col = lambda i, j: (0, j)
    val_col = lambda i, j: (0, nf + j)
    in_specs = [pl.BlockSpec((tm, D_MODEL), row), pl.BlockSpec((tm, D_MODEL), row),
                _mod_spec(5, rows, tiles_per_group), _const_spec(gpost.shape),
                pl.BlockSpec((D_MODEL, tf), gate_col), pl.BlockSpec((D_MODEL, tf), val_col),
                pl.BlockSpec((tf, D_MODEL), lambda i, j: (j, 0)),
                pl.BlockSpec((FFN_CONV, tf), gate_col), pl.BlockSpec((FFN_CONV, tf), val_col),
                pl.BlockSpec((1, tf), gate_col), pl.BlockSpec((1, tf), val_col)]
    args = [h2, x1, mods, gpost, w_in, w_in, w_out, cw, cw, cb, cb]
    scratch = [pltpu.VMEM((tm, D_MODEL), F32)]
    if sample:
        in_specs += [pl.BlockSpec((m + 8, tf), gate_col), pl.BlockSpec((m + 8, tf), val_col)]
        args += [inj, inj]
        u_specs = [pl.BlockSpec((tm, tf), lambda i, j: (i, j))] * 2
        u_shapes = [jax.ShapeDtypeStruct((m, D_FF), F32)] * 2
    else:
        u_specs = [pl.BlockSpec((1, 8, tf), lambda i, j: (i, 0, j))] * 2
        u_shapes = [jax.ShapeDtypeStruct((m // tm, 8, D_FF), F32)] * 2
        scratch += [pltpu.VMEM((nf, 8, tf), F32), pltpu.VMEM((nf, 8, tf), F32)]
    return pl.pallas_call(
        functools.partial(_ffn_kernel, sample, tiles_per_group),
        grid=(m // tm, nf),
        in_specs=in_specs,
        out_specs=[pl.BlockSpec((tm, D_MODEL), row)] + u_specs,
        out_shape=[jax.ShapeDtypeStruct((m, D_MODEL), F32)] + u_shapes,
        scratch_shapes=scratch,
        compiler_params=pltpu.CompilerParams(dimension_semantics=("arbitrary", "arbitrary"),
                                             vmem_limit_bytes=VMEM_LIMIT),
        name="ffn_sample" if sample else "ffn_prompt",
    )(*args)


def _ssd_constants(block_rows):
    r = jnp.arange(SSD_CHUNK)[:, None]
    c = jnp.arange(SSD_CHUNK)[None, :]
    same = (r // block_rows) == (c // block_rows)
    tri = ((c <= r) & same).astype(BF16)
    sel = (c == (r // block_rows) * block_rows + block_rows - 1).astype(BF16)
    hrow = jnp.arange(LANES)[:, None]
    e1 = (hrow == jnp.arange(D_SSD)[None, :] // SSD_HEAD_DIM).astype(BF16)
    e2 = (hrow == jnp.arange(SSD_HEADS * LANES)[None, :] // LANES).astype(BF16)
    return tri, sel, jnp.concatenate([e1, e2], axis=1)


def _suffix_matrix():
    j = jnp.arange(LANES)[:, None]
    s = jnp.arange(LANES)[None, :]
    u = jnp.concatenate([(j > s).astype(BF16), jnp.ones((LANES, LANES), BF16)], axis=1)
    return jnp.concatenate([u, u], axis=0)


def _inject_rows(ctx, nctx):
    nseq, _, c = ctx.shape
    body = jnp.concatenate([ctx[1:], jnp.zeros((1, nctx, c), F32)], axis=0)
    body = jnp.pad(body, ((0, 0), (SAMPLE_ROWS - nctx, 0), (0, 0))).reshape(nseq * SAMPLE_ROWS, c)
    head = jnp.pad(ctx[0], ((8 - nctx, 0), (0, 0)))
    return jnp.concatenate([head, body], axis=0)


def kernel(x_prompt, x_sample, c_prompt, c_sample, cache_k, cache_v, page_table, state_ssm, state_conv, state_ffn_conv, w_ada, b_ada, g_pre_mix, g_post_mix, g_pre_ffn, g_post_ffn, w_in, conv_w, conv_b, dt_bias, a_log, d_skip, g_ssd_norm, sb_bias, w_out, w_ffn_in, ffn_conv_w, ffn_conv_b, w_ffn_out):
    assert w_ada.shape[0] == 1, "one layer"
    nb, seq, d = x_prompt.shape
    ns, ls, _ = x_sample.shape
    n_pool = cache_k.shape[1]
    mp = nb * seq
    ms = ns * SAMPLE_ROWS

    w = w_in[0].astype(BF16)
    o_bc, o_dt, o_q = 2 * D_SSD, 2 * D_SSD + BC_DIM, 2 * D_SSD + BC_DIM + SSD_HEADS
    w_main = jnp.concatenate([w[:, :o_bc], w[:, o_q:]], axis=1)
    w_tail = jnp.concatenate([w[:, o_bc:o_q], jnp.zeros((d, DT_PAD - SSD_HEADS), BF16)], axis=1)
    w_out_b = w_out[0].astype(BF16)
    w_ffn_in_b = w_ffn_in[0].astype(BF16)
    w_ffn_out_b = w_ffn_out[0].astype(BF16)
    row = lambda v: v.reshape(1, -1)
    dtb = jnp.pad(dt_bias[0], (0, DT_PAD - SSD_HEADS)).reshape(1, DT_PAD)
    alog = jnp.pad(a_log[0], (0, DT_PAD - SSD_HEADS)).reshape(1, DT_PAD)
    dskip = jnp.repeat(d_skip[0], SSD_HEAD_DIM).reshape(1, D_SSD)
    gnorm = row(g_ssd_norm[0])
    cw, cb = conv_w[0], row(conv_b[0])
    fcw, fcb = ffn_conv_w[0], row(ffn_conv_b[0])
    uo = _suffix_matrix()

    n_c = nb + ns
    c_all = jnp.concatenate([c_prompt, c_sample, jnp.zeros((-n_c % 16, d), F32)], axis=0)
    mod = _ada(c_all, w_ada[0], row(b_ada[0]))
    mods_p = mod[:nb].reshape(nb, 1, 6 * d)
    mods_s = jnp.repeat(mod[nb:n_c], SAMPLE_ROWS, axis=0).reshape(1, ms, 6 * d)

    tm_p = 512
    tpg_p = seq // tm_p
    xp = x_prompt.reshape(mp, d)
    zx_p, qkv_p, tail_p, k_p, v_p = _inproj(xp, mods_p, row(g_pre_mix[0]), w_main, w_tail, tm_p, tpg_p, BF16)
    zx3 = zx_p.reshape(nb, seq, 2 * D_SSD)
    tail3 = tail_p.reshape(nb, seq, PROJ_TAIL)
    tri, sel, emat = _ssd_constants(SSD_CHUNK)
    y_ssd, ssm_p = _ssd_prompt(zx3, tail3, cw, cb, dtb, alog, dskip, gnorm, tri, sel, emat)
    o_att = _attn_prompt(qkv_p.reshape(nb, seq, 3 * D_ATT), sb_bias[0], uo)
    tm_e = 512
    x1_p, h2_p = _outproj(y_ssd.reshape(mp, D_SSD), o_att.reshape(mp, D_ATT), xp, mods_p, row(g_post_mix[0]),
                          row(g_pre_ffn[0]), w_out_b, tm_e, seq // tm_e)
    yp, ctxg_p, ctxv_p = _ffn(h2_p, x1_p, mods_p, row(g_post_ffn[0]), w_ffn_in_b, w_ffn_out_b, fcw, fcb,
                              tm_e, seq // tm_e)

    xs_pad = jnp.pad(x_sample, ((0, 0), (0, SAMPLE_ROWS - ls), (0, 0))).reshape(ms, d)
    zx_s, qkv_s, tail_s, k_s, v_s = _inproj(xs_pad, mods_s, row(g_pre_mix[0]), w_main, w_tail, ms, 1, F32)
    seq_per_tile = SSD_CHUNK // SAMPLE_ROWS
    conv_ctx = state_conv[0].reshape(ns // seq_per_tile, seq_per_tile, SSD_CONV - 1, -1)
    ctx_inj = jax.vmap(lambda c: _inject_rows(c, SSD_CONV - 1))(conv_ctx)
    tri_s, sel_s, _ = _ssd_constants(SAMPLE_ROWS)
    y_ssd_s, ssm_s = _ssd_sample(zx_s, tail_s, ctx_inj, state_ssm[0].reshape(ns, D_SSD, SSD_STATE), cw, cb, dtb, alog,
                                 dskip, gnorm, tri_s, sel_s, emat)
    bias_rows = jnp.broadcast_to(jnp.repeat(sb_bias[0], SAMPLE_ROWS)[:, None], (ATT_HEADS * SAMPLE_ROWS, LANES))
    o_att_s = _attn_sample(page_table, qkv_s.reshape(ns, SAMPLE_ROWS, 3 * D_ATT),
                           cache_k[0].reshape(n_pool, PAGE_SIZE * ATT_HEADS, ATT_HEAD_DIM),
                           cache_v[0].reshape(n_pool, PAGE_SIZE * ATT_HEADS, ATT_HEAD_DIM), bias_rows, uo)
    x1_s, h2_s = _outproj(y_ssd_s, o_att_s.reshape(ms, D_ATT), xs_pad, mods_s, row(g_post_mix[0]),
                          row(g_pre_ffn[0]), w_out_b, ms, 1)
    inj_f = _inject_rows(state_ffn_conv[0], FFN_CONV - 1)
    ys_pad, ug_s, uv_s = _ffn(h2_s, x1_s, mods_s, row(g_post_ffn[0]), w_ffn_in_b, w_ffn_out_b, fcw, fcb,
                              ms, 1, inj=inj_f)

    heads = (ATT_HEADS, ATT_HEAD_DIM)
    k_p = k_p.reshape(1, nb, seq, *heads)
    v_p = v_p.reshape(1, nb, seq, *heads)
    tail = slice(seq - (SSD_CONV - 1), seq)
    conv_p = jnp.concatenate([zx3[:, tail, D_SSD:], tail3[:, tail, :BC_DIM]], axis=-1)[None]
    ffn_p = jnp.concatenate([ctxg_p, ctxv_p], axis=-1)[seq // tm_e - 1::seq // tm_e, 8 - (FFN_CONV - 1):][None]
    k_s = k_s.reshape(ns, SAMPLE_ROWS, *heads)[None, :, :ls]
    v_s = v_s.reshape(ns, SAMPLE_ROWS, *heads)[None, :, :ls]
    raw_s = jnp.concatenate([zx_s.reshape(ns, SAMPLE_ROWS, -1)[:, :ls, D_SSD:],
                             tail_s.reshape(ns, SAMPLE_ROWS, -1)[:, :ls, :BC_DIM]], axis=-1)
    conv_s = jnp.concatenate([state_conv[0], raw_s], axis=1)[:, -(SSD_CONV - 1):][None]
    u_s = jnp.concatenate([ug_s, uv_s], axis=-1).reshape(ns, SAMPLE_ROWS, 2 * D_FF)[:, :ls]
    ffn_s = jnp.concatenate([state_ffn_conv[0], u_s], axis=1)[:, -(FFN_CONV - 1):][None]
    return (yp.reshape(nb, seq, d), ys_pad.reshape(ns, SAMPLE_ROWS, d)[:, :ls],
            k_p, v_p, ssm_p.reshape(1, nb, SSD_HEADS, SSD_HEAD_DIM, SSD_STATE), conv_p, ffn_p,
            k_s, v_s, ssm_s.reshape(1, ns, SSD_HEADS, SSD_HEAD_DIM, SSD_STATE), conv_s, ffn_s)
```

```python
import functools

import jax
import jax.numpy as jnp
from jax import lax
from jax.experimental import pallas as pl
from jax.experimental.pallas import tpu as pltpu

F32 = jnp.float32
BF16 = jnp.bfloat16

D_MODEL = 2048
D_SSD = 1024
SSD_HEAD_DIM = 64
SSD_HEADS = 16
SSD_GROUPS = 2
SSD_STATE = 128
SSD_CONV = 4
SSD_CHUNK = 128
BC_DIM = 2 * SSD_GROUPS * SSD_STATE
D_ATT = 1024
ATT_HEAD_DIM = 128
ATT_HEADS = 8
D_FF = 5632
FFN_CONV = 3
PAGE_SIZE = 128
EPS = 1e-6
ATT_SCALE = ATT_HEAD_DIM ** -0.5

PROJ_TN = 1024
PROJ_MAIN = 5 * PROJ_TN
DT_PAD = 128
PROJ_TAIL = BC_DIM + DT_PAD

SAMPLE_ROWS = 8
LANES = 128
VMEM_LIMIT = 56 * 1024 * 1024


def _dot(a, b):
    return jnp.dot(a, b, preferred_element_type=F32)


def _dot_nt(a, b):
    return lax.dot_general(a, b, (((1,), (1,)), ((), ())), preferred_element_type=F32)


def _sigmoid(x):
    return 1.0 / (1.0 + jnp.exp(-x))


def _silu(x):
    return x * _sigmoid(x)


def _softplus(x):
    return jnp.maximum(x, 0.0) + jnp.log1p(jnp.exp(-jnp.abs(x)))


def _rms(x, g):
    ms = jnp.mean(x * x, axis=-1, keepdims=True)
    return x * lax.rsqrt(ms + EPS) * g


def _split3(x):
    h = x.astype(BF16)
    r = x - h.astype(F32)
    m = r.astype(BF16)
    l = (r - m.astype(F32)).astype(BF16)
    return h, m, l


def _dot3_right(x, mat):
    h, m, l = _split3(x)
    return _dot(h, mat) + _dot(m, mat) + _dot(l, mat)


def _dot3_left(mat, x):
    h, m, l = _split3(x)
    return _dot(mat, h) + _dot(mat, m) + _dot(mat, l)


def _ada_kernel(c_ref, w_ref, b_ref, o_ref):
    s = _silu(c_ref[...]).astype(BF16)
    o_ref[...] = _dot(s, w_ref[...].astype(BF16)) + b_ref[...]


def _ada(c_all, w_ada, b_ada):
    m, d = c_all.shape
    n = w_ada.shape[1]
    tn = 1024
    return pl.pallas_call(
        _ada_kernel,
        grid=(n // tn,),
        in_specs=[pl.BlockSpec((m, d), lambda j: (0, 0)),
                  pl.BlockSpec((d, tn), lambda j: (0, j)),
                  pl.BlockSpec((1, tn), lambda j: (0, j))],
        out_specs=pl.BlockSpec((m, tn), lambda j: (0, j)),
        out_shape=jax.ShapeDtypeStruct((m, n), F32),
        compiler_params=pltpu.CompilerParams(dimension_semantics=("arbitrary",), vmem_limit_bytes=VMEM_LIMIT),
        name="ada_mod",
    )(c_all, w_ada, b_ada)


def _mod_spec(idx, rows, tiles_per_group):
    return pl.BlockSpec((1, rows, D_MODEL), lambda i, *_: (i // tiles_per_group, 0, idx))


def _inproj_kernel(x_ref, sh_ref, sc_ref, g_ref, w_ref, wt_ref, zx_ref, qkv_ref, tail_ref, k_ref, v_ref, hn_ref):
    j = pl.program_id(1)
    tm = x_ref.shape[0]
    chunk = min(tm, 128)

    @pl.when(j == 0)
    def _():
        g = g_ref[...]
        for r in range(0, tm, chunk):
            sc, sh = sc_ref[0], sh_ref[0]
            if sc.shape[0] != 1:
                sc, sh = sc[r:r + chunk], sh[r:r + chunk]
            h = (_rms(x_ref[r:r + chunk, :], g) * (1.0 + sc) + sh).astype(BF16)
            hn_ref[r:r + chunk, :] = h
            tail_ref[r:r + chunk, :] = _dot(h, wt_ref[...])

    @pl.when(j > 0)
    def _():
        res = _dot(hn_ref[...], w_ref[...])

        def by_token(dst):
            for h in range(ATT_HEADS):
                dst[pl.ds(h, tm, stride=ATT_HEADS), :] = res[:, h * ATT_HEAD_DIM:(h + 1) * ATT_HEAD_DIM]

        @pl.when(j < 3)
        def _():
            zx_ref[...] = res

        @pl.when(j >= 3)
        def _():
            qkv_ref[...] = res.astype(qkv_ref.dtype)

        @pl.when(j == 4)
        def _():
            by_token(k_ref)

        @pl.when(j == 5)
        def _():
            by_token(v_ref)


def _inproj(x, mods, g, w, wt, tm, tiles_per_group, qkv_dtype):
    m = x.shape[0]
    rows = mods.shape[1]
    kv_spec = pl.BlockSpec((tm * ATT_HEADS, ATT_HEAD_DIM), lambda i, j: (i, 0))
    kv_shape = jax.ShapeDtypeStruct((m * ATT_HEADS, ATT_HEAD_DIM), F32)
    return pl.pallas_call(
        _inproj_kernel,
        grid=(m // tm, PROJ_MAIN // PROJ_TN + 1),
        in_specs=[pl.BlockSpec((tm, D_MODEL), lambda i, j: (i, 0)),
                  _mod_spec(0, rows, tiles_per_group),
                  _mod_spec(1, rows, tiles_per_group),
                  pl.BlockSpec((1, D_MODEL), lambda i, j: (0, 0)),
                  pl.BlockSpec((D_MODEL, PROJ_TN), lambda i, j: (0, jnp.maximum(j - 1, 0))),
                  pl.BlockSpec((D_MODEL, PROJ_TAIL), lambda i, j: (0, 0))],
        out_specs=[pl.BlockSpec((tm, PROJ_TN), lambda i, j: (i, jnp.clip(j - 1, 0, 1))),
                   pl.BlockSpec((tm, PROJ_TN), lambda i, j: (i, jnp.clip(j - 3, 0, 2))),
                   pl.BlockSpec((tm, PROJ_TAIL), lambda i, j: (i, 0)),
                   kv_spec, kv_spec],
        out_shape=[jax.ShapeDtypeStruct((m, 2 * PROJ_TN), F32),
                   jax.ShapeDtypeStruct((m, 3 * PROJ_TN), qkv_dtype),
                   jax.ShapeDtypeStruct((m, PROJ_TAIL), F32), kv_shape, kv_shape],
        scratch_shapes=[pltpu.VMEM((tm, D_MODEL), BF16)],
        compiler_params=pltpu.CompilerParams(dimension_semantics=("arbitrary", "arbitrary"),
                                             vmem_limit_bytes=VMEM_LIMIT),
        name="in_proj",
    )(x, mods, mods, g, w, wt)


def _shift_rows(u, prev8, s):
    r = pltpu.roll(u, s, 0)
    head = jnp.where(lax.broadcasted_iota(jnp.int32, (8, 1), 0) < s, pltpu.roll(prev8, s, 0), r[:8])
    return jnp.concatenate([head, r[8:]], axis=0)


def _conv_rows(u, prev8, w, b):
    taps = w.shape[0]
    y = _shift_rows(u, prev8, taps - 1) * w[0:1]
    for t in range(1, taps):
        s = taps - 1 - t
        y = y + (_shift_rows(u, prev8, s) if s else u) * w[t:t + 1]
    return y + b


def _ssd_tile(xs, bc, dtr, row_valid, causal, tri, sel, emat, dtb, alog):
    dt = _softplus(dtr + dtb)
    if row_valid is not None:
        dt = jnp.where(row_valid, dt, 0.0)
    a = -jnp.exp(alog)
    acum = _dot3_left(tri, dt * a)
    alast = _dot3_left(sel, acum)
    acx = _dot3_right(acum, emat)
    acum_x = acx[:, :D_SSD]
    acum_x2 = acx[:, D_SSD:]
    e1 = emat[:, :D_SSD]
    dt_x = _dot3_right(dt, e1)
    alast_x = _dot3_right(alast, e1)
    acum_t = acum.T

    xdt = xs * dt_x
    xdt_bf = xdt.astype(BF16)
    xdtd_bf = (xdt * jnp.exp(alast_x - acum_x)).astype(BF16)
    bm = bc[:, :SSD_GROUPS * SSD_STATE]
    cm = bc[:, SSD_GROUPS * SSD_STATE:]
    lane = lax.broadcasted_iota(jnp.int32, (SSD_CHUNK, LANES), 1)
    first_head = lane < SSD_HEAD_DIM

    pieces = []
    heads_per_group = SSD_HEADS // SSD_GROUPS
    for g in range(SSD_GROUPS):
        bg = bm[:, g * SSD_STATE:(g + 1) * SSD_STATE].astype(BF16)
        cg = cm[:, g * SSD_STATE:(g + 1) * SSD_STATE].astype(BF16)
        cb = _dot_nt(cg, bg)
        for pair in range(heads_per_group // 2):
            h0 = g * heads_per_group + 2 * pair
            ys = []
            for hh in (h0, h0 + 1):
                seg = acum_x2[:, hh * LANES:(hh + 1) * LANES] - acum_t[hh:hh + 1, :]
                decay = jnp.where(causal, jnp.exp(jnp.where(causal, seg, 0.0)), 0.0)
                sc = (cb * decay).astype(BF16)
                ys.append(_dot(sc, xdt_bf[:, (h0 // 2) * LANES:(h0 // 2 + 1) * LANES]))
            pieces.append(jnp.where(first_head, ys[0], ys[1]))
    y_diag = jnp.concatenate(pieces, axis=1)
    return y_diag, xdtd_bf, acum_x, bm, cm


def _ssd_finish(y, xs, z, dskip, gnorm):
    y = y + dskip * xs
    y = y * _silu(z)
    half = D_SSD // SSD_GROUPS
    outs = []
    for g in range(SSD_GROUPS):
        outs.append(_rms(y[:, g * half:(g + 1) * half], gnorm[:, g * half:(g + 1) * half]))
    return jnp.concatenate(outs, axis=1)


def _ssd_prompt_kernel(z_ref, xs_ref, bc_ref, dt_ref, cw_ref, cb_ref, dtb_ref, alog_ref, dskip_ref, gnorm_ref,
                       tri_ref, sel_ref, emat_ref, y_ref, st_ref, ht_ref, prev_xs, prev_bc):
    c = pl.program_id(1)
    rows = SSD_CHUNK

    @pl.when(c == 0)
    def _():
        ht_ref[...] = jnp.zeros_like(ht_ref)
        prev_xs[...] = jnp.zeros_like(prev_xs)
        prev_bc[...] = jnp.zeros_like(prev_bc)

    xr = xs_ref[0]
    br = bc_ref[0]
    cw = cw_ref[...]
    cb = cb_ref[...]
    xs = _silu(_conv_rows(xr, prev_xs[...], cw[:, :D_SSD], cb[:, :D_SSD]))
    bc = _silu(_conv_rows(br, prev_bc[...], cw[:, D_SSD:], cb[:, D_SSD:]))
    prev_xs[...] = xr[rows - 8:, :]
    prev_bc[...] = br[rows - 8:, :]

    r = lax.broadcasted_iota(jnp.int32, (rows, rows), 0)
    s = lax.broadcasted_iota(jnp.int32, (rows, rows), 1)
    causal = s <= r
    y_diag, xdtd_bf, acum_x, bm, cm = _ssd_tile(xs, bc, dt_ref[0], None, causal, tri_ref[...], sel_ref[...],
                                                 emat_ref[...], dtb_ref[...], alog_ref[...])
    ht = ht_ref[...]
    half = D_SSD // SSD_GROUPS
    y_off, states = [], []
    for g in range(SSD_GROUPS):
        bg = bm[:, g * SSD_STATE:(g + 1) * SSD_STATE]
        cg = cm[:, g * SSD_STATE:(g + 1) * SSD_STATE].astype(BF16)
        y_off.append(_dot(cg, ht[:, g * half:(g + 1) * half].astype(BF16)))
        states.append(_dot(bg.T.astype(BF16), xdtd_bf[:, g * half:(g + 1) * half]))
    y = y_diag + jnp.concatenate(y_off, axis=1) * jnp.exp(acum_x)
    ht_new = jnp.exp(acum_x[rows - 1:rows, :]) * ht + jnp.concatenate(states, axis=1)
    ht_ref[...] = ht_new
    y_ref[0] = _ssd_finish(y, xs, z_ref[0], dskip_ref[...], gnorm_ref[...]).astype(BF16)

    @pl.when(c == pl.num_programs(1) - 1)
    def _():
        for k in range(D_SSD // LANES):
            st_ref[0, k * LANES:(k + 1) * LANES, :] = ht_new[:, k * LANES:(k + 1) * LANES].T


def _const_spec(shape):
    nd = len(shape)
    return pl.BlockSpec(shape, lambda *_: (0,) * nd)


def _ssd_prompt(zx3, tail3, cw, cb, dtb, alog, dskip, gnorm, tri, sel, emat):
    nb, seq, _ = zx3.shape
    nc = seq // SSD_CHUNK
    return pl.pallas_call(
        _ssd_prompt_kernel,
        grid=(nb, nc),
        in_specs=[pl.BlockSpec((1, SSD_CHUNK, D_SSD), lambda b, c: (b, c, 0)),
                  pl.BlockSpec((1, SSD_CHUNK, D_SSD), lambda b, c: (b, c, 1)),
                  pl.BlockSpec((1, SSD_CHUNK, BC_DIM), lambda b, c: (b, c, 0)),
                  pl.BlockSpec((1, SSD_CHUNK, DT_PAD), lambda b, c: (b, c, BC_DIM // DT_PAD)),
                  _const_spec(cw.shape), _const_spec(cb.shape), _const_spec(dtb.shape), _const_spec(alog.shape),
                  _const_spec(dskip.shape), _const_spec(gnorm.shape), _const_spec(tri.shape),
                  _const_spec(sel.shape), _const_spec(emat.shape)],
        out_specs=[pl.BlockSpec((1, SSD_CHUNK, D_SSD), lambda b, c: (b, c, 0)),
                   pl.BlockSpec((1, D_SSD, SSD_STATE), lambda b, c: (b, 0, 0))],
        out_shape=[jax.ShapeDtypeStruct((nb, seq, D_SSD), BF16),
                   jax.ShapeDtypeStruct((nb, D_SSD, SSD_STATE), F32)],
        scratch_shapes=[pltpu.VMEM((SSD_STATE, D_SSD), F32),
                        pltpu.VMEM((8, D_SSD), F32),
                        pltpu.VMEM((8, BC_DIM), F32)],
        compiler_params=pltpu.CompilerParams(dimension_semantics=("arbitrary", "arbitrary"),
                                             vmem_limit_bytes=VMEM_LIMIT),
        name="ssd_prompt",
    )(zx3, zx3, tail3, tail3, cw, cb, dtb, alog, dskip, gnorm, tri, sel, emat)


def _ssd_sample_kernel(z_ref, xs_ref, bc_ref, dt_ref, ctx_ref, h0_ref, cw_ref, cb_ref, dtb_ref, alog_ref, dskip_ref,
                       gnorm_ref, tri_ref, sel_ref, emat_ref, y_ref, st_ref,
                       ybase, yoff, xs_s, xdtd_s, acumx_s, bm_s, cm_s):
    s_id = pl.program_id(1)
    rows = SSD_CHUNK
    nseq = rows // SAMPLE_ROWS
    rr = lax.broadcasted_iota(jnp.int32, (rows, 1), 0)

    @pl.when(s_id == 0)
    def _():
        ctx = ctx_ref[0]
        is_ctx = (rr % SAMPLE_ROWS) >= SAMPLE_ROWS - (SSD_CONV - 1)
        xr = jnp.where(is_ctx, ctx[8:, :D_SSD], xs_ref[...])
        br = jnp.where(is_ctx, ctx[8:, D_SSD:], bc_ref[...])
        cw = cw_ref[...]
        cb = cb_ref[...]
        xs = _silu(_conv_rows(xr, ctx[0:8, :D_SSD], cw[:, :D_SSD], cb[:, :D_SSD]))
        bc = _silu(_conv_rows(br, ctx[0:8, D_SSD:], cw[:, D_SSD:], cb[:, D_SSD:]))
        r = lax.broadcasted_iota(jnp.int32, (rows, rows), 0)
        c = lax.broadcasted_iota(jnp.int32, (rows, rows), 1)
        causal = (c <= r) & ((c // SAMPLE_ROWS) == (r // SAMPLE_ROWS))
        row_valid = (rr % SAMPLE_ROWS) < (SAMPLE_ROWS // 2)
        y_diag, xdtd_bf, acum_x, bm, cm = _ssd_tile(xs, bc, dt_ref[...], row_valid, causal, tri_ref[...],
                                                     sel_ref[...], emat_ref[...], dtb_ref[...], alog_ref[...])
        ybase[...] = y_diag
        yoff[...] = jnp.zeros_like(yoff)
        xs_s[...] = xs
        xdtd_s[...] = xdtd_bf
        acumx_s[...] = acum_x
        bm_s[...] = bm
        cm_s[...] = cm

    in_seq_rows = (rr // SAMPLE_ROWS) == s_id
    in_seq_lanes = (lax.broadcasted_iota(jnp.int32, (1, rows), 1) // SAMPLE_ROWS) == s_id
    h0 = h0_ref[0]
    ht = jnp.concatenate([h0[k * LANES:(k + 1) * LANES, :].T for k in range(D_SSD // LANES)], axis=1)
    half = D_SSD // SSD_GROUPS
    y_off, states = [], []
    for g in range(SSD_GROUPS):
        bg_t = jnp.where(in_seq_lanes, bm_s[:, g * SSD_STATE:(g + 1) * SSD_STATE].T, 0.0).astype(BF16)
        cg = cm_s[:, g * SSD_STATE:(g + 1) * SSD_STATE].astype(BF16)
        y_off.append(_dot(cg, ht[:, g * half:(g + 1) * half].astype(BF16)))
        states.append(_dot(bg_t, xdtd_s[:, g * half:(g + 1) * half]))
    yoff[...] = yoff[...] + jnp.where(in_seq_rows, jnp.concatenate(y_off, axis=1), 0.0)
    last = pl.multiple_of(s_id * SAMPLE_ROWS, SAMPLE_ROWS) + (SAMPLE_ROWS - 1)
    ht_new = jnp.exp(acumx_s[pl.ds(last, 1), :]) * ht + jnp.concatenate(states, axis=1)
    for k in range(D_SSD // LANES):
        st_ref[0, k * LANES:(k + 1) * LANES, :] = ht_new[:, k * LANES:(k + 1) * LANES].T

    @pl.when(s_id == nseq - 1)
    def _():
        y = ybase[...] + yoff[...] * jnp.exp(acumx_s[...])
        y_ref[...] = _ssd_finish(y, xs_s[...], z_ref[...], dskip_ref[...], gnorm_ref[...]).astype(BF16)


def _ssd_sample(zx, tail, ctx, h0, cw, cb, dtb, alog, dskip, gnorm, tri, sel, emat):
    m = zx.shape[0]
    nt = m // SSD_CHUNK
    nseq = SSD_CHUNK // SAMPLE_ROWS
    return pl.pallas_call(
        _ssd_sample_kernel,
        grid=(nt, nseq),
        in_specs=[pl.BlockSpec((SSD_CHUNK, D_SSD), lambda t, s: (t, 0)),
                  pl.BlockSpec((SSD_CHUNK, D_SSD), lambda t, s: (t, 1)),
                  pl.BlockSpec((SSD_CHUNK, BC_DIM), lambda t, s: (t, 0)),
                  pl.BlockSpec((SSD_CHUNK, DT_PAD), lambda t, s: (t, BC_DIM // DT_PAD)),
                  pl.BlockSpec((1, SSD_CHUNK + 8, D_SSD + BC_DIM), lambda t, s: (t, 0, 0)),
                  pl.BlockSpec((1, D_SSD, SSD_STATE), lambda t, s: (t * nseq + s, 0, 0)),
                  _const_spec(cw.shape), _const_spec(cb.shape), _const_spec(dtb.shape), _const_spec(alog.shape),
                  _const_spec(dskip.shape), _const_spec(gnorm.shape), _const_spec(tri.shape),
                  _const_spec(sel.shape), _const_spec(emat.shape)],
        out_specs=[pl.BlockSpec((SSD_CHUNK, D_SSD), lambda t, s: (t, 0)),
                   pl.BlockSpec((1, D_SSD, SSD_STATE), lambda t, s: (t * nseq + s, 0, 0))],
        out_shape=[jax.ShapeDtypeStruct((m, D_SSD), BF16),
                   jax.ShapeDtypeStruct(h0.shape, F32)],
        scratch_shapes=[pltpu.VMEM((SSD_CHUNK, D_SSD), F32),
                        pltpu.VMEM((SSD_CHUNK, D_SSD), F32),
                        pltpu.VMEM((SSD_CHUNK, D_SSD), F32),
                        pltpu.VMEM((SSD_CHUNK, D_SSD), BF16),
                        pltpu.VMEM((SSD_CHUNK, D_SSD), F32),
                        pltpu.VMEM((SSD_CHUNK, SSD_GROUPS * SSD_STATE), F32),
                        pltpu.VMEM((SSD_CHUNK, SSD_GROUPS * SSD_STATE), F32)],
        compiler_params=pltpu.CompilerParams(dimension_semantics=("arbitrary", "arbitrary"),
                                             vmem_limit_bytes=VMEM_LIMIT),
        name="ssd_sample",
    )(zx, zx, tail, tail, ctx, h0, cw, cb, dtb, alog, dskip, gnorm, tri, sel, emat)


def _sb_block(qb, kb, vb, bias, uo, run, mask, nearest_first):
    z = _dot_nt(qb, kb) * ATT_SCALE + bias
    nz = -z
    lg = jnp.log(1.0 + jnp.exp(jnp.minimum(z, nz)))
    ls_pos = jnp.minimum(z, 0.0) - lg
    lm = jnp.minimum(nz, 0.0) - lg
    if mask is not None:
        lm = jnp.where(mask, lm, 0.0)
    hi = lm.astype(BF16)
    lo = (lm - hi.astype(F32)).astype(BF16)
    nsub = z.shape[1] // LANES
    tails = [None] * nsub
    for c in (range(nsub) if nearest_first else reversed(range(nsub))):
        cols = slice(c * LANES, (c + 1) * LANES)
        st = _dot(jnp.concatenate([hi[:, cols], lo[:, cols]], axis=1), uo)
        tails[c] = run + st[:, :LANES]
        run = run + st[:, LANES:]
    w = jnp.exp(ls_pos + jnp.concatenate(tails, axis=1))
    if mask is not None:
        w = jnp.where(mask, w, 0.0)
    return _dot(w.astype(BF16), vb), run


ATT_TQ = 512
ATT_TK = ATT_TQ


def _attn_prompt_kernel(bias_ref, q_ref, k_ref, v_ref, uo_ref, o_ref):
    h = pl.program_id(1)
    qi = pl.program_id(2)
    qb = q_ref[0]
    kbf = k_ref.at[0]
    vbf = v_ref.at[0]
    bias = bias_ref[h]
    uo = uo_ref[...]
    q0 = pl.multiple_of(qi * ATT_TQ, ATT_TQ)
    accs, runs = [], []
    for r in range(ATT_TQ // LANES):
        nk = (r + 1) * LANES
        row = lax.broadcasted_iota(jnp.int32, (LANES, nk), 0) + r * LANES
        col = lax.broadcasted_iota(jnp.int32, (LANES, nk), 1)
        a, rn = _sb_block(qb[r * LANES:(r + 1) * LANES], kbf[pl.ds(q0, nk), :], vbf[pl.ds(q0, nk), :], bias, uo,
                          jnp.zeros((LANES, LANES), F32), col < row, False)
        accs.append(a)
        runs.append(rn)
    acc = jnp.concatenate(accs, axis=0)
    run = jnp.concatenate(runs, axis=0)

    def body(i, carry):
        acc, run = carry
        start = pl.multiple_of((qi - 1 - i) * ATT_TK, ATT_TK)
        pv, run = _sb_block(qb, kbf[pl.ds(start, ATT_TK), :], vbf[pl.ds(start, ATT_TK), :], bias, uo, run, None, False)
        return acc + pv, run

    acc, _ = lax.fori_loop(0, qi, body, (acc, run))
    o_ref[0] = acc.astype(BF16)


def _attn_prompt(qkv3, sb_bias, uo):
    nb, seq, _ = qkv3.shape
    tq = ATT_TQ
    return pl.pallas_call(
        _attn_prompt_kernel,
        grid=(nb, ATT_HEADS, seq // tq),
        in_specs=[pl.BlockSpec(memory_space=pltpu.SMEM),
                  pl.BlockSpec((1, tq, ATT_HEAD_DIM), lambda b, h, i: (b, i, h)),
                  pl.BlockSpec((1, seq, ATT_HEAD_DIM), lambda b, h, i: (b, 0, ATT_HEADS + h)),
                  pl.BlockSpec((1, seq, ATT_HEAD_DIM), lambda b, h, i: (b, 0, 2 * ATT_HEADS + h)),
                  _const_spec(uo.shape)],
        out_specs=pl.BlockSpec((1, tq, ATT_HEAD_DIM), lambda b, h, i: (b, i, h)),
        out_shape=jax.ShapeDtypeStruct((nb, seq, D_ATT), BF16),
        compiler_params=pltpu.CompilerParams(dimension_semantics=("arbitrary", "arbitrary", "arbitrary"),
                                             vmem_limit_bytes=VMEM_LIMIT),
        name="attn_prompt",
    )(sb_bias, qkv3, qkv3, qkv3, uo)


PAGES_PER_STEP = 8
PAGE_SLOTS = 3


def _page_by_head(ref):
    heads = [ref[pl.ds(h, PAGE_SIZE, stride=ATT_HEADS), :] for h in range(ATT_HEADS)]
    return jnp.concatenate(heads, axis=1).astype(BF16)


def _attn_sample_kernel(nseq, steps, pt_ref, q_ref, kn_ref, vn_ref, bias_ref, uo_ref, ck_hbm, cv_hbm, o_ref,
                        qbd, knew, vnew, acc, run, kbuf, vbuf, sems):
    b = pl.program_id(0)
    j = pl.program_id(1)
    total = nseq * steps
    t = b * steps + j
    nrow = ATT_HEADS * SAMPLE_ROWS
    bias = bias_ref[...]
    uo = uo_ref[...]

    def page_copies(step):
        sb = step // steps
        sj = step - sb * steps
        slot = step % PAGE_SLOTS
        copies = []
        for g in range(PAGES_PER_STEP):
            page = pt_ref[sb, steps * PAGES_PER_STEP - 1 - (sj * PAGES_PER_STEP + g)]
            copies.append(pltpu.make_async_copy(ck_hbm.at[page], kbuf.at[slot, g], sems.at[0, slot]))
            copies.append(pltpu.make_async_copy(cv_hbm.at[page], vbuf.at[slot, g], sems.at[1, slot]))
        return copies

    @pl.when(t == 0)
    def _():
        for step in range(PAGE_SLOTS - 1):
            for c in page_copies(step):
                c.start()

    @pl.when(t + (PAGE_SLOTS - 1) < total)
    def _():
        for c in page_copies(t + (PAGE_SLOTS - 1)):
            c.start()

    @pl.when((b == 0) & (j == 0))
    def _():
        knew[...] = jnp.zeros_like(knew)
        vnew[...] = jnp.zeros_like(vnew)

    @pl.when(j == 0)
    def _():
        q8 = q_ref[0]
        rr = lax.broadcasted_iota(jnp.int32, (nrow, D_ATT), 0)
        cc = lax.broadcasted_iota(jnp.int32, (nrow, D_ATT), 1)
        qt = jnp.concatenate([q8] * ATT_HEADS, axis=0)
        qbd[...] = jnp.where((rr // SAMPLE_ROWS) == (cc // ATT_HEAD_DIM), qt, 0.0).astype(BF16)
        knew[0:SAMPLE_ROWS, :] = kn_ref[0]
        vnew[0:SAMPLE_ROWS, :] = vn_ref[0]
        qq = lax.broadcasted_iota(jnp.int32, (nrow, LANES), 0) % SAMPLE_ROWS
        kk = lax.broadcasted_iota(jnp.int32, (nrow, LANES), 1)
        mask = (kk < qq) & (kk < SAMPLE_ROWS // 2)
        pv, r = _sb_block(qbd[...], knew[...].astype(BF16), vnew[...].astype(BF16), bias, uo,
                          jnp.zeros((nrow, LANES), F32), mask, True)
        acc[...] = pv
        run[...] = r

    for c in page_copies(t):
        c.wait()
    slot = t % PAGE_SLOTS
    kcat = jnp.concatenate([_page_by_head(kbuf.at[slot, g]) for g in range(PAGES_PER_STEP)], axis=0)
    vcat = jnp.concatenate([_page_by_head(vbuf.at[slot, g]) for g in range(PAGES_PER_STEP)], axis=0)
    pv, r = _sb_block(qbd[...], kcat, vcat, jnp.concatenate([bias] * PAGES_PER_STEP, axis=1), uo, run[...], None, True)
    a = acc[...] + pv
    acc[...] = a
    run[...] = r

    @pl.when(j == steps - 1)
    def _():
        for h in range(ATT_HEADS):
            o_ref[0, :, h * ATT_HEAD_DIM:(h + 1) * ATT_HEAD_DIM] = (
                a[h * SAMPLE_ROWS:(h + 1) * SAMPLE_ROWS, h * ATT_HEAD_DIM:(h + 1) * ATT_HEAD_DIM])


def _attn_sample(page_table, projs3, cache_k, cache_v, bias_rows, uo):
    nseq, npages = page_table.shape
    steps = npages // PAGES_PER_STEP
    assert steps * PAGES_PER_STEP == npages and nseq * steps >= PAGE_SLOTS - 1
    page_buf = pltpu.VMEM((PAGE_SLOTS, PAGES_PER_STEP, PAGE_SIZE * ATT_HEADS, ATT_HEAD_DIM), F32)

    grid_spec = pltpu.PrefetchScalarGridSpec(
        num_scalar_prefetch=1,
        grid=(nseq, steps),
        in_specs=[pl.BlockSpec((1, SAMPLE_ROWS, D_ATT), lambda b, j, pt: (b, 0, 0)),
                  pl.BlockSpec((1, SAMPLE_ROWS, D_ATT), lambda b, j, pt: (b, 0, 1)),
                  pl.BlockSpec((1, SAMPLE_ROWS, D_ATT), lambda b, j, pt: (b, 0, 2)),
                  pl.BlockSpec(bias_rows.shape, lambda b, j, pt: (0, 0)),
                  pl.BlockSpec(uo.shape, lambda b, j, pt: (0, 0)),
                  pl.BlockSpec(memory_space=pl.ANY), pl.BlockSpec(memory_space=pl.ANY)],
        out_specs=pl.BlockSpec((1, SAMPLE_ROWS, D_ATT), lambda b, j, pt: (b, 0, 0)),
        scratch_shapes=[pltpu.VMEM((ATT_HEADS * SAMPLE_ROWS, D_ATT), BF16),
                        pltpu.VMEM((PAGE_SIZE, D_ATT), F32),
                        pltpu.VMEM((PAGE_SIZE, D_ATT), F32),
                        pltpu.VMEM((ATT_HEADS * SAMPLE_ROWS, D_ATT), F32),
                        pltpu.VMEM((ATT_HEADS * SAMPLE_ROWS, LANES), F32),
                        page_buf, page_buf, pltpu.SemaphoreType.DMA((2, PAGE_SLOTS))])
    return pl.pallas_call(
        functools.partial(_attn_sample_kernel, nseq, steps),
        grid_spec=grid_spec,
        out_shape=jax.ShapeDtypeStruct((nseq, SAMPLE_ROWS, D_ATT), F32),
        compiler_params=pltpu.CompilerParams(dimension_semantics=("arbitrary", "arbitrary"),
                                             vmem_limit_bytes=VMEM_LIMIT),
        name="attn_sample",
    )(page_table, projs3, projs3, projs3, bias_rows, uo, cache_k, cache_v)


def _outproj_kernel(y_ref, o_ref, x_ref, g1_ref, sh2_ref, sc2_ref, gpost_ref, gpre_ref, wy_ref, wo_ref, x1_ref, h2_ref):
    tm = x_ref.shape[0]
    chunk = min(tm, 256)
    for r in range(0, tm, chunk):
        rows = slice(r, r + chunk)
        g1, sh2, sc2 = (v if v.shape[0] == 1 else v[rows] for v in (g1_ref[0], sh2_ref[0], sc2_ref[0]))
        m = _dot(y_ref[rows, :], wy_ref[...]) + _dot(o_ref[rows, :].astype(BF16), wo_ref[...])
        x1 = x_ref[rows, :] + g1 * _rms(m, gpost_ref[...])
        x1_ref[rows, :] = x1
        h2 = _rms(x1, gpre_ref[...]) * (1.0 + sc2) + sh2
        h2_ref[rows, :] = h2.astype(BF16)


def _outproj(y, o, x, mods, gpost, gpre, w_out, tm, tiles_per_group):
    m = x.shape[0]
    rows = mods.shape[1]
    half = w_out.shape[0] // 2
    row = lambda i: (i, 0)
    return pl.pallas_call(
        _outproj_kernel,
        grid=(m // tm,),
        in_specs=[pl.BlockSpec((tm, D_SSD), row), pl.BlockSpec((tm, D_ATT), row), pl.BlockSpec((tm, D_MODEL), row),
                  _mod_spec(2, rows, tiles_per_group), _mod_spec(3, rows, tiles_per_group),
                  _mod_spec(4, rows, tiles_per_group),
                  _const_spec(gpost.shape), _const_spec(gpre.shape),
                  pl.BlockSpec((half, D_MODEL), lambda i: (0, 0)),
                  pl.BlockSpec((half, D_MODEL), lambda i: (1, 0))],
        out_specs=[pl.BlockSpec((tm, D_MODEL), row), pl.BlockSpec((tm, D_MODEL), row)],
        out_shape=[jax.ShapeDtypeStruct((m, D_MODEL), F32), jax.ShapeDtypeStruct((m, D_MODEL), BF16)],
        compiler_params=pltpu.CompilerParams(dimension_semantics=("arbitrary",), vmem_limit_bytes=VMEM_LIMIT),
        name="out_proj",
    )(y, o, x, mods, mods, mods, gpost, gpre, w_out, w_out)


FFN_CHUNK = 256


def _ffn_kernel(sample, tiles_per_group, *refs):
    if sample:
        (h_ref, x1_ref, g2_ref, gpost_ref, wg_ref, wv_ref, wo_ref, cwg_ref, cwv_ref, cbg_ref, cbv_ref,
         injg_ref, injv_ref, y_ref, ug_ref, uv_ref, acc) = refs
    else:
        (h_ref, x1_ref, g2_ref, gpost_ref, wg_ref, wv_ref, wo_ref, cwg_ref, cwv_ref, cbg_ref, cbv_ref,
         y_ref, ug_ref, uv_ref, acc, halog, halov) = refs
    i = pl.program_id(0)
    j = pl.program_id(1)
    tm = h_ref.shape[0]
    tf = wg_ref.shape[1]
    hn = h_ref[...]

    @pl.when(j == 0)
    def _():
        acc[...] = jnp.zeros_like(acc)

    if sample:
        is_ctx = (lax.broadcasted_iota(jnp.int32, (tm, 1), 0) % SAMPLE_ROWS) >= SAMPLE_ROWS - (FFN_CONV - 1)
    else:
        @pl.when((i == 0) & (j == 0))
        def _():
            halog[...] = jnp.zeros_like(halog)
            halov[...] = jnp.zeros_like(halov)

        first = lax.rem(i, tiles_per_group) == 0

    def half(w_ref, cw_ref, cb_ref, u_ref, inj_ref, halo, cols):
        u = jnp.concatenate([_dot(hn[r:r + 128], w_ref[:, cols]) for r in range(0, tm, 128)], axis=0)
        if sample:
            u_ref[:, cols] = u
            prev = inj_ref[0:8, cols]
            u = jnp.where(is_ctx, inj_ref[8:, cols], u)
        else:
            prev = jnp.where(first, 0.0, halo[j, :, cols])
            halo[j, :, cols] = u[tm - 8:, :]
            u_ref[0, :, cols] = u[tm - 8:, :]
        return _conv_rows(u, prev, cw_ref[:, cols], cb_ref[:, cols])

    acts = []
    for c in range(tf // FFN_CHUNK):
        cols = slice(c * FFN_CHUNK, (c + 1) * FFN_CHUNK)
        gate = half(wg_ref, cwg_ref, cbg_ref, ug_ref, None if not sample else injg_ref, None if sample else halog, cols)
        val = half(wv_ref, cwv_ref, cbv_ref, uv_ref, None if not sample else injv_ref, None if sample else halov, cols)
        acts.append((_silu(gate) * val).astype(BF16))
    acc[...] += _dot(jnp.concatenate(acts, axis=1), wo_ref[...])

    @pl.when(j == pl.num_programs(1) - 1)
    def _():
        y_ref[...] = x1_ref[...] + g2_ref[0] * _rms(acc[...], gpost_ref[...])


def _ffn(h2, x1, mods, gpost, w_in, w_out, cw, cb, tm, tiles_per_group, inj=None):
    m = h2.shape[0]
    rows = mods.shape[1]
    tf = 512
    nf = D_FF // tf
    sample = inj is not None
    row = lambda i, j: (i, 0)
    gate_col = lambda i, j: (0, j)
    val_col = lambda i, j: (0, nf + j)
    in_specs = [pl.BlockSpec((tm, D_MODEL), row), pl.BlockSpec((tm, D_MODEL), row),
                _mod_spec(5, rows, tiles_per_group), _const_spec(gpost.shape),
                pl.BlockSpec((D_MODEL, tf), gate_col), pl.BlockSpec((D_MODEL, tf), val_col),
                pl.BlockSpec((tf, D_MODEL), lambda i, j: (j, 0)),
                pl.BlockSpec((FFN_CONV, tf), gate_col), pl.BlockSpec((FFN_CONV, tf), val_col),
                pl.BlockSpec((1, tf), gate_col), pl.BlockSpec((1, tf), val_col)]
    args = [h2, x1, mods, gpost, w_in, w_in, w_out, cw, cw, cb, cb]
    scratch = [pltpu.VMEM((tm, D_MODEL), F32)]
    if sample:
        in_specs += [pl.BlockSpec((m + 8, tf), gate_col), pl.BlockSpec((m + 8, tf), val_col)]
        args += [inj, inj]
        u_specs = [pl.BlockSpec((tm, tf), lambda i, j: (i, j))] * 2
        u_shapes = [jax.ShapeDtypeStruct((m, D_FF), F32)] * 2
    else:
        u_specs = [pl.BlockSpec((1, 8, tf), lambda i, j: (i, 0, j))] * 2
        u_shapes = [jax.ShapeDtypeStruct((m // tm, 8, D_FF), F32)] * 2
        scratch += [pltpu.VMEM((nf, 8, tf), F32), pltpu.VMEM((nf, 8, tf), F32)]
    return pl.pallas_call(
        functools.partial(_ffn_kernel, sample, tiles_per_group),
        grid=(m // tm, nf),
        in_specs=in_specs,
        out_specs=[pl.BlockSpec((tm, D_MODEL), row)] + u_specs,
        out_shape=[jax.ShapeDtypeStruct((m, D_MODEL), F32)] + u_shapes,
        scratch_shapes=scratch,
        compiler_params=pltpu.CompilerParams(dimension_semantics=("arbitrary", "arbitrary"),
                                             vmem_limit_bytes=VMEM_LIMIT),
        name="ffn_sample" if sample else "ffn_prompt",
    )(*args)


def _ssd_constants(block_rows):
    r = jnp.arange(SSD_CHUNK)[:, None]
    c = jnp.arange(SSD_CHUNK)[None, :]
    same = (r // block_rows) == (c // block_rows)
    tri = ((c <= r) & same).astype(BF16)
    sel = (c == (r // block_rows) * block_rows + block_rows - 1).astype(BF16)
    hrow = jnp.arange(LANES)[:, None]
    e1 = (hrow == jnp.arange(D_SSD)[None, :] // SSD_HEAD_DIM).astype(BF16)
    e2 = (hrow == jnp.arange(SSD_HEADS * LANES)[None, :] // LANES).astype(BF16)
    return tri, sel, jnp.concatenate([e1, e2], axis=1)


def _suffix_matrix():
    j = jnp.arange(LANES)[:, None]
    s = jnp.arange(LANES)[None, :]
    u = jnp.concatenate([(j > s).astype(BF16), jnp.ones((LANES, LANES), BF16)], axis=1)
    return jnp.concatenate([u, u], axis=0)


def _inject_rows(ctx, nctx):
    nseq, _, c = ctx.shape
    body = jnp.concatenate([ctx[1:], jnp.zeros((1, nctx, c), F32)], axis=0)
    body = jnp.pad(body, ((0, 0), (SAMPLE_ROWS - nctx, 0), (0, 0))).reshape(nseq * SAMPLE_ROWS, c)
    head = jnp.pad(ctx[0], ((8 - nctx, 0), (0, 0)))
    return jnp.concatenate([head, body], axis=0)


def kernel(x_prompt, x_sample, c_prompt, c_sample, cache_k, cache_v, page_table, state_ssm, state_conv, state_ffn_conv, w_ada, b_ada, g_pre_mix, g_post_mix, g_pre_ffn, g_post_ffn, w_in, conv_w, conv_b, dt_bias, a_log, d_skip, g_ssd_norm, sb_bias, w_out, w_ffn_in, ffn_conv_w, ffn_conv_b, w_ffn_out):
    assert w_ada.shape[0] == 1, "one layer"
    nb, seq, d = x_prompt.shape
    ns, ls, _ = x_sample.shape
    n_pool = cache_k.shape[1]
    mp = nb * seq
    ms = ns * SAMPLE_ROWS

    w = w_in[0].astype(BF16)
    o_bc, o_dt, o_q = 2 * D_SSD, 2 * D_SSD + BC_DIM, 2 * D_SSD + BC_DIM + SSD_HEADS
    w_main = jnp.concatenate([w[:, :o_bc], w[:, o_q:]], axis=1)
    w_tail = jnp.concatenate([w[:, o_bc:o_q], jnp.zeros((d, DT_PAD - SSD_HEADS), BF16)], axis=1)
    w_out_b = w_out[0].astype(BF16)
    w_ffn_in_b = w_ffn_in[0].astype(BF16)
    w_ffn_out_b = w_ffn_out[0].astype(BF16)
    row = lambda v: v.reshape(1, -1)
    dtb = jnp.pad(dt_bias[0], (0, DT_PAD - SSD_HEADS)).reshape(1, DT_PAD)
    alog = jnp.pad(a_log[0], (0, DT_PAD - SSD_HEADS)).reshape(1, DT_PAD)
    dskip = jnp.repeat(d_skip[0], SSD_HEAD_DIM).reshape(1, D_SSD)
    gnorm = row(g_ssd_norm[0])
    cw, cb = conv_w[0], row(conv_b[0])
    fcw, fcb = ffn_conv_w[0], row(ffn_conv_b[0])
    uo = _suffix_matrix()

    n_c = nb + ns
    c_all = jnp.concatenate([c_prompt, c_sample, jnp.zeros((-n_c % 16, d), F32)], axis=0)
    mod = _ada(c_all, w_ada[0], row(b_ada[0]))
    mods_p = mod[:nb].reshape(nb, 1, 6 * d)
    mods_s = jnp.repeat(mod[nb:n_c], SAMPLE_ROWS, axis=0).reshape(1, ms, 6 * d)

    tm_p = 512
    tpg_p = seq // tm_p
    xp = x_prompt.reshape(mp, d)
    zx_p, qkv_p, tail_p, k_p, v_p = _inproj(xp, mods_p, row(g_pre_mix[0]), w_main, w_tail, tm_p, tpg_p, BF16)
    zx3 = zx_p.reshape(nb, seq, 2 * D_SSD)
    tail3 = tail_p.reshape(nb, seq, PROJ_TAIL)
    tri, sel, emat = _ssd_constants(SSD_CHUNK)
    y_ssd, ssm_p = _ssd_prompt(zx3, tail3, cw, cb, dtb, alog, dskip, gnorm, tri, sel, emat)
    o_att = _attn_prompt(qkv_p.reshape(nb, seq, 3 * D_ATT), sb_bias[0], uo)
    tm_e = 512
    x1_p, h2_p = _outproj(y_ssd.reshape(mp, D_SSD), o_att.reshape(mp, D_ATT), xp, mods_p, row(g_post_mix[0]),
                          row(g_pre_ffn[0]), w_out_b, tm_e, seq // tm_e)
    yp, ctxg_p, ctxv_p = _ffn(h2_p, x1_p, mods_p, row(g_post_ffn[0]), w_ffn_in_b, w_ffn_out_b, fcw, fcb,
                              tm_e, seq // tm_e)

    xs_pad = jnp.pad(x_sample, ((0, 0), (0, SAMPLE_ROWS - ls), (0, 0))).reshape(ms, d)
    zx_s, qkv_s, tail_s, k_s, v_s = _inproj(xs_pad, mods_s, row(g_pre_mix[0]), w_main, w_tail, ms, 1, F32)
    seq_per_tile = SSD_CHUNK // SAMPLE_ROWS
    conv_ctx = state_conv[0].reshape(ns // seq_per_tile, seq_per_tile, SSD_CONV - 1, -1)
    ctx_inj = jax.vmap(lambda c: _inject_rows(c, SSD_CONV - 1))(conv_ctx)
    tri_s, sel_s, _ = _ssd_constants(SAMPLE_ROWS)
    y_ssd_s, ssm_s = _ssd_sample(zx_s, tail_s, ctx_inj, state_ssm[0].reshape(ns, D_SSD, SSD_STATE), cw, cb, dtb, alog,
                                 dskip, gnorm, tri_s, sel_s, emat)
    bias_rows = jnp.broadcast_to(jnp.repeat(sb_bias[0], SAMPLE_ROWS)[:, None], (ATT_HEADS * SAMPLE_ROWS, LANES))
    o_att_s = _attn_sample(page_table, qkv_s.reshape(ns, SAMPLE_ROWS, 3 * D_ATT),
                           cache_k[0].reshape(n_pool, PAGE_SIZE * ATT_HEADS, ATT_HEAD_DIM),
                           cache_v[0].reshape(n_pool, PAGE_SIZE * ATT_HEADS, ATT_HEAD_DIM), bias_rows, uo)
    x1_s, h2_s = _outproj(y_ssd_s, o_att_s.reshape(ms, D_ATT), xs_pad, mods_s, row(g_post_mix[0]),
                          row(g_pre_ffn[0]), w_out_b, ms, 1)
    inj_f = _inject_rows(state_ffn_conv[0], FFN_CONV - 1)
    ys_pad, ug_s, uv_s = _ffn(h2_s, x1_s, mods_s, row(g_post_ffn[0]), w_ffn_in_b, w_ffn_out_b, fcw, fcb,
                              ms, 1, inj=inj_f)

    heads = (ATT_HEADS, ATT_HEAD_DIM)
    k_p = k_p.reshape(1, nb, seq, *heads)
    v_p = v_p.reshape(1, nb, seq, *heads)
    tail = slice(seq - (SSD_CONV - 1), seq)
    conv_p = jnp.concatenate([zx3[:, tail, D_SSD:], tail3[:, tail, :BC_DIM]], axis=-1)[None]
    ffn_p = jnp.concatenate([ctxg_p, ctxv_p], axis=-1)[seq // tm_e - 1::seq // tm_e, 8 - (FFN_CONV - 1):][None]
    k_s = k_s.reshape(ns, SAMPLE_ROWS, *heads)[None, :, :ls]
    v_s = v_s.reshape(ns, SAMPLE_ROWS, *heads)[None, :, :ls]
    raw_s = jnp.concatenate([zx_s.reshape(ns, SAMPLE_ROWS, -1)[:, :ls, D_SSD:],
                             tail_s.reshape(ns, SAMPLE_ROWS, -1)[:, :ls, :BC_DIM]], axis=-1)
    conv_s = jnp.concatenate([state_conv[0], raw_s], axis=1)[:, -(SSD_CONV - 1):][None]
    u_s = jnp.concatenate([ug_s, uv_s], axis=-1).reshape(ns, SAMPLE_ROWS, 2 * D_FF)[:, :ls]
    ffn_s = jnp.concatenate([state_ffn_conv[0], u_s], axis=1)[:, -(FFN_CONV - 1):][None]
    return (yp.reshape(nb, seq, d), ys_pad.reshape(ns, SAMPLE_ROWS, d)[:, :ls],
            k_p, v_p, ssm_p.reshape(1, nb, SSD_HEADS, SSD_HEAD_DIM, SSD_STATE), conv_p, ffn_p,
            k_s, v_s, ssm_s.reshape(1, ns, SSD_HEADS, SSD_HEAD_DIM, SSD_STATE), conv_s, ffn_s)
```

```python
import functools

import jax
import jax.numpy as jnp
from jax import lax
from jax.experimental import pallas as pl
from jax.experimental.pallas import tpu as pltpu

F32 = jnp.float32
BF16 = jnp.bfloat16

D_MODEL = 2048
D_SSD = 1024
SSD_HEAD_DIM = 64
SSD_HEADS = 16
SSD_GROUPS = 2
SSD_STATE = 128
SSD_CONV = 4
SSD_CHUNK = 128
BC_DIM = 2 * SSD_GROUPS * SSD_STATE
D_ATT = 1024
ATT_HEAD_DIM = 128
ATT_HEADS = 8
D_FF = 5632
FFN_CONV = 3
PAGE_SIZE = 128
EPS = 1e-6
ATT_SCALE = ATT_HEAD_DIM ** -0.5

PROJ_TN = 1024
PROJ_MAIN = 5 * PROJ_TN
DT_PAD = 128
PROJ_TAIL = BC_DIM + DT_PAD

SAMPLE_ROWS = 8
LANES = 128
VMEM_LIMIT = 56 * 1024 * 1024


def _dot(a, b):
    return jnp.dot(a, b, preferred_element_type=F32)


def _dot_nt(a, b):
    return lax.dot_general(a, b, (((1,), (1,)), ((), ())), preferred_element_type=F32)


def _sigmoid(x):
    return 1.0 / (1.0 + jnp.exp(-x))


def _silu(x):
    return x * _sigmoid(x)


def _softplus(x):
    return jnp.maximum(x, 0.0) + jnp.log1p(jnp.exp(-jnp.abs(x)))


def _rms(x, g):
    ms = jnp.mean(x * x, axis=-1, keepdims=True)
    return x * lax.rsqrt(ms + EPS) * g


def _split3(x):
    h = x.astype(BF16)
    r = x - h.astype(F32)
    m = r.astype(BF16)
    l = (r - m.astype(F32)).astype(BF16)
    return h, m, l


def _dot3_right(x, mat):
    h, m, l = _split3(x)
    return _dot(h, mat) + _dot(m, mat) + _dot(l, mat)


def _dot3_left(mat, x):
    h, m, l = _split3(x)
    return _dot(mat, h) + _dot(mat, m) + _dot(mat, l)


def _ada_kernel(c_ref, w_ref, b_ref, o_ref):
    s = _silu(c_ref[...]).astype(BF16)
    o_ref[...] = _dot(s, w_ref[...].astype(BF16)) + b_ref[...]


def _ada(c_all, w_ada, b_ada):
    m, d = c_all.shape
    n = w_ada.shape[1]
    tn = 1024
    return pl.pallas_call(
        _ada_kernel,
        grid=(n // tn,),
        in_specs=[pl.BlockSpec((m, d), lambda j: (0, 0)),
                  pl.BlockSpec((d, tn), lambda j: (0, j)),
                  pl.BlockSpec((1, tn), lambda j: (0, j))],
        out_specs=pl.BlockSpec((m, tn), lambda j: (0, j)),
        out_shape=jax.ShapeDtypeStruct((m, n), F32),
        compiler_params=pltpu.CompilerParams(dimension_semantics=("arbitrary",), vmem_limit_bytes=VMEM_LIMIT),
        name="ada_mod",
    )(c_all, w_ada, b_ada)


def _mod_spec(idx, rows, tiles_per_group):
    return pl.BlockSpec((1, rows, D_MODEL), lambda i, *_: (i // tiles_per_group, 0, idx))


def _inproj_kernel(x_ref, sh_ref, sc_ref, g_ref, w_ref, wt_ref, zx_ref, qkv_ref, tail_ref, k_ref, v_ref, hn_ref):
    j = pl.program_id(1)
    tm = x_ref.shape[0]
    chunk = min(tm, 128)

    @pl.when(j == 0)
    def _():
        g = g_ref[...]
        for r in range(0, tm, chunk):
            sc, sh = sc_ref[0], sh_ref[0]
            if sc.shape[0] != 1:
                sc, sh = sc[r:r + chunk], sh[r:r + chunk]
            h = (_rms(x_ref[r:r + chunk, :], g) * (1.0 + sc) + sh).astype(BF16)
            hn_ref[r:r + chunk, :] = h
            tail_ref[r:r + chunk, :] = _dot(h, wt_ref[...])

    @pl.when(j > 0)
    def _():
        res = _dot(hn_ref[...], w_ref[...])

        def by_token(dst):
            for h in range(ATT_HEADS):
                dst[pl.ds(h, tm, stride=ATT_HEADS), :] = res[:, h * ATT_HEAD_DIM:(h + 1) * ATT_HEAD_DIM]

        @pl.when(j < 3)
        def _():
            zx_ref[...] = res

        @pl.when(j >= 3)
        def _():
            qkv_ref[...] = res.astype(qkv_ref.dtype)

        @pl.when(j == 4)
        def _():
            by_token(k_ref)

        @pl.when(j == 5)
        def _():
            by_token(v_ref)


def _inproj(x, mods, g, w, wt, tm, tiles_per_group, qkv_dtype):
    m = x.shape[0]
    rows = mods.shape[1]
    kv_spec = pl.BlockSpec((tm * ATT_HEADS, ATT_HEAD_DIM), lambda i, j: (i, 0))
    kv_shape = jax.ShapeDtypeStruct((m * ATT_HEADS, ATT_HEAD_DIM), F32)
    return pl.pallas_call(
        _inproj_kernel,
        grid=(m // tm, PROJ_MAIN // PROJ_TN + 1),
        in_specs=[pl.BlockSpec((tm, D_MODEL), lambda i, j: (i, 0)),
                  _mod_spec(0, rows, tiles_per_group),
                  _mod_spec(1, rows, tiles_per_group),
                  pl.BlockSpec((1, D_MODEL), lambda i, j: (0, 0)),
                  pl.BlockSpec((D_MODEL, PROJ_TN), lambda i, j: (0, jnp.maximum(j - 1, 0))),
                  pl.BlockSpec((D_MODEL, PROJ_TAIL), lambda i, j: (0, 0))],
        out_specs=[pl.BlockSpec((tm, PROJ_TN), lambda i, j: (i, jnp.clip(j - 1, 0, 1))),
                   pl.BlockSpec((tm, PROJ_TN), lambda i, j: (i, jnp.clip(j - 3, 0, 2))),
                   pl.BlockSpec((tm, PROJ_TAIL), lambda i, j: (i, 0)),
                   kv_spec, kv_spec],
        out_shape=[jax.ShapeDtypeStruct((m, 2 * PROJ_TN), F32),
                   jax.ShapeDtypeStruct((m, 3 * PROJ_TN), qkv_dtype),
                   jax.ShapeDtypeStruct((m, PROJ_TAIL), F32), kv_shape, kv_shape],
        scratch_shapes=[pltpu.VMEM((tm, D_MODEL), BF16)],
        compiler_params=pltpu.CompilerParams(dimension_semantics=("arbitrary", "arbitrary"),
                                             vmem_limit_bytes=VMEM_LIMIT),
        name="in_proj",
    )(x, mods, mods, g, w, wt)


def _shift_rows(u, prev8, s):
    r = pltpu.roll(u, s, 0)
    head = jnp.where(lax.broadcasted_iota(jnp.int32, (8, 1), 0) < s, pltpu.roll(prev8, s, 0), r[:8])
    return jnp.concatenate([head, r[8:]], axis=0)


def _conv_rows(u, prev8, w, b):
    taps = w.shape[0]
    y = _shift_rows(u, prev8, taps - 1) * w[0:1]
    for t in range(1, taps):
        s = taps - 1 - t
        y = y + (_shift_rows(u, prev8, s) if s else u) * w[t:t + 1]
    return y + b


def _ssd_tile(xs, bc, dtr, row_valid, causal, tri, sel, emat, dtb, alog):
    dt = _softplus(dtr + dtb)
    if row_valid is not None:
        dt = jnp.where(row_valid, dt, 0.0)
    a = -jnp.exp(alog)
    acum = _dot3_left(tri, dt * a)
    alast = _dot3_left(sel, acum)
    acx = _dot3_right(acum, emat)
    acum_x = acx[:, :D_SSD]
    acum_x2 = acx[:, D_SSD:]
    e1 = emat[:, :D_SSD]
    dt_x = _dot3_right(dt, e1)
    alast_x = _dot3_right(alast, e1)
    acum_t = acum.T

    xdt = xs * dt_x
    xdt_bf = xdt.astype(BF16)
    xdtd_bf = (xdt * jnp.exp(alast_x - acum_x)).astype(BF16)
    bm = bc[:, :SSD_GROUPS * SSD_STATE]
    cm = bc[:, SSD_GROUPS * SSD_STATE:]
    lane = lax.broadcasted_iota(jnp.int32, (SSD_CHUNK, LANES), 1)
    first_head = lane < SSD_HEAD_DIM

    pieces = []
    heads_per_group = SSD_HEADS // SSD_GROUPS
    for g in range(SSD_GROUPS):
        bg = bm[:, g * SSD_STATE:(g + 1) * SSD_STATE].astype(BF16)
        cg = cm[:, g * SSD_STATE:(g + 1) * SSD_STATE].astype(BF16)
        cb = _dot_nt(cg, bg)
        for pair in range(heads_per_group // 2):
            h0 = g * heads_per_group + 2 * pair
            ys = []
            for hh in (h0, h0 + 1):
                seg = acum_x2[:, hh * LANES:(hh + 1) * LANES] - acum_t[hh:hh + 1, :]
                decay = jnp.where(causal, jnp.exp(jnp.where(causal, seg, 0.0)), 0.0)
                sc = (cb * decay).astype(BF16)
                ys.append(_dot(sc, xdt_bf[:, (h0 // 2) * LANES:(h0 // 2 + 1) * LANES]))
            pieces.append(jnp.where(first_head, ys[0], ys[1]))
    y_diag = jnp.concatenate(pieces, axis=1)
    return y_diag, xdtd_bf, acum_x, bm, cm


def _ssd_finish(y, xs, z, dskip, gnorm):
    y = y + dskip * xs
    y = y * _silu(z)
    half = D_SSD // SSD_GROUPS
    outs = []
    for g in range(SSD_GROUPS):
        outs.append(_rms(y[:, g * half:(g + 1) * half], gnorm[:, g * half:(g + 1) * half]))
    return jnp.concatenate(outs, axis=1)


def _ssd_prompt_kernel(z_ref, xs_ref, bc_ref, dt_ref, cw_ref, cb_ref, dtb_ref, alog_ref, dskip_ref, gnorm_ref,
                       tri_ref, sel_ref, emat_ref, y_ref, st_ref, ht_ref, prev_xs, prev_bc):
    c = pl.program_id(1)
    rows = SSD_CHUNK

    @pl.when(c == 0)
    def _():
        ht_ref[...] = jnp.zeros_like(ht_ref)
        prev_xs[...] = jnp.zeros_like(prev_xs)
        prev_bc[...] = jnp.zeros_like(prev_bc)

    xr = xs_ref[0]
    br = bc_ref[0]
    cw = cw_ref[...]
    cb = cb_ref[...]
    xs = _silu(_conv_rows(xr, prev_xs[...], cw[:, :D_SSD], cb[:, :D_SSD]))
    bc = _silu(_conv_rows(br, prev_bc[...], cw[:, D_SSD:], cb[:, D_SSD:]))
    prev_xs[...] = xr[rows - 8:, :]
    prev_bc[...] = br[rows - 8:, :]

    r = lax.broadcasted_iota(jnp.int32, (rows, rows), 0)
    s = lax.broadcasted_iota(jnp.int32, (rows, rows), 1)
    causal = s <= r
    y_diag, xdtd_bf, acum_x, bm, cm = _ssd_tile(xs, bc, dt_ref[0], None, causal, tri_ref[...], sel_ref[...],
                                                 emat_ref[...], dtb_ref[...], alog_ref[...])
    ht = ht_ref[...]
    half = D_SSD // SSD_GROUPS
    y_off, states = [], []
    for g in range(SSD_GROUPS):
        bg = bm[:, g * SSD_STATE:(g + 1) * SSD_STATE]
        cg = cm[:, g * SSD_STATE:(g + 1) * SSD_STATE].astype(BF16)
        y_off.append(_dot(cg, ht[:, g * half:(g + 1) * half].astype(BF16)))
        states.append(_dot(bg.T.astype(BF16), xdtd_bf[:, g * half:(g + 1) * half]))
    y = y_diag + jnp.concatenate(y_off, axis=1) * jnp.exp(acum_x)
    ht_new = jnp.exp(acum_x[rows - 1:rows, :]) * ht + jnp.concatenate(states, axis=1)
    ht_ref[...] = ht_new
    y_ref[0] = _ssd_finish(y, xs, z_ref[0], dskip_ref[...], gnorm_ref[...]).astype(BF16)

    @pl.when(c == pl.num_programs(1) - 1)
    def _():
        for k in range(D_SSD // LANES):
            st_ref[0, k * LANES:(k + 1) * LANES, :] = ht_new[:, k * LANES:(k + 1) * LANES].T


def _const_spec(shape):
    nd = len(shape)
    return pl.BlockSpec(shape, lambda *_: (0,) * nd)


def _ssd_prompt(zx3, tail3, cw, cb, dtb, alog, dskip, gnorm, tri, sel, emat):
    nb, seq, _ = zx3.shape
    nc = seq // SSD_CHUNK
    return pl.pallas_call(
        _ssd_prompt_kernel,
        grid=(nb, nc),
        in_specs=[pl.BlockSpec((1, SSD_CHUNK, D_SSD), lambda b, c: (b, c, 0)),
                  pl.BlockSpec((1, SSD_CHUNK, D_SSD), lambda b, c: (b, c, 1)),
                  pl.BlockSpec((1, SSD_CHUNK, BC_DIM), lambda b, c: (b, c, 0)),
                  pl.BlockSpec((1, SSD_CHUNK, DT_PAD), lambda b, c: (b, c, BC_DIM // DT_PAD)),
                  _const_spec(cw.shape), _const_spec(cb.shape), _const_spec(dtb.shape), _const_spec(alog.shape),
                  _const_spec(dskip.shape), _const_spec(gnorm.shape), _const_spec(tri.shape),
                  _const_spec(sel.shape), _const_spec(emat.shape)],
        out_specs=[pl.BlockSpec((1, SSD_CHUNK, D_SSD), lambda b, c: (b, c, 0)),
                   pl.BlockSpec((1, D_SSD, SSD_STATE), lambda b, c: (b, 0, 0))],
        out_shape=[jax.ShapeDtypeStruct((nb, seq, D_SSD), BF16),
                   jax.ShapeDtypeStruct((nb, D_SSD, SSD_STATE), F32)],
        scratch_shapes=[pltpu.VMEM((SSD_STATE, D_SSD), F32),
                        pltpu.VMEM((8, D_SSD), F32),
                        pltpu.VMEM((8, BC_DIM), F32)],
        compiler_params=pltpu.CompilerParams(dimension_semantics=("arbitrary", "arbitrary"),
                                             vmem_limit_bytes=VMEM_LIMIT),
        name="ssd_prompt",
    )(zx3, zx3, tail3, tail3, cw, cb, dtb, alog, dskip, gnorm, tri, sel, emat)


def _ssd_sample_kernel(z_ref, xs_ref, bc_ref, dt_ref, ctx_ref, h0_ref, cw_ref, cb_ref, dtb_ref, alog_ref, dskip_ref,
                       gnorm_ref, tri_ref, sel_ref, emat_ref, y_ref, st_ref,
                       ybase, yoff, xs_s, xdtd_s, acumx_s, bm_s, cm_s):
    s_id = pl.program_id(1)
    rows = SSD_CHUNK
    nseq = rows // SAMPLE_ROWS
    rr = lax.broadcasted_iota(jnp.int32, (rows, 1), 0)

    @pl.when(s_id == 0)
    def _():
        ctx = ctx_ref[0]
        is_ctx = (rr % SAMPLE_ROWS) >= SAMPLE_ROWS - (SSD_CONV - 1)
        xr = jnp.where(is_ctx, ctx[8:, :D_SSD], xs_ref[...])
        br = jnp.where(is_ctx, ctx[8:, D_SSD:], bc_ref[...])
        cw = cw_ref[...]
        cb = cb_ref[...]
        xs = _silu(_conv_rows(xr, ctx[0:8, :D_SSD], cw[:, :D_SSD], cb[:, :D_SSD]))
        bc = _silu(_conv_rows(br, ctx[0:8, D_SSD:], cw[:, D_SSD:], cb[:, D_SSD:]))
        r = lax.broadcasted_iota(jnp.int32, (rows, rows), 0)
        c = lax.broadcasted_iota(jnp.int32, (rows, rows), 1)
        causal = (c <= r) & ((c // SAMPLE_ROWS) == (r // SAMPLE_ROWS))
        row_valid = (rr % SAMPLE_ROWS) < (SAMPLE_ROWS // 2)
        y_diag, xdtd_bf, acum_x, bm, cm = _ssd_tile(xs, bc, dt_ref[...], row_valid, causal, tri_ref[...],
                                                     sel_ref[...], emat_ref[...], dtb_ref[...], alog_ref[...])
        ybase[...] = y_diag
        yoff[...] = jnp.zeros_like(yoff)
        xs_s[...] = xs
        xdtd_s[...] = xdtd_bf
        acumx_s[...] = acum_x
        bm_s[...] = bm
        cm_s[...] = cm

    in_seq_rows = (rr // SAMPLE_ROWS) == s_id
    in_seq_lanes = (lax.broadcasted_iota(jnp.int32, (1, rows), 1) // SAMPLE_ROWS) == s_id
    h0 = h0_ref[0]
    ht = jnp.concatenate([h0[k * LANES:(k + 1) * LANES, :].T for k in range(D_SSD // LANES)], axis=1)
    half = D_SSD // SSD_GROUPS
    y_off, states = [], []
    for g in range(SSD_GROUPS):
        bg_t = jnp.where(in_seq_lanes, bm_s[:, g * SSD_STATE:(g + 1) * SSD_STATE].T, 0.0).astype(BF16)
        cg = cm_s[:, g * SSD_STATE:(g + 1) * SSD_STATE].astype(BF16)
        y_off.append(_dot(cg, ht[:, g * half:(g + 1) * half].astype(BF16)))
        states.append(_dot(bg_t, xdtd_s[:, g * half:(g + 1) * half]))
    yoff[...] = yoff[...] + jnp.where(in_seq_rows, jnp.concatenate(y_off, axis=1), 0.0)
    last = pl.multiple_of(s_id * SAMPLE_ROWS, SAMPLE_ROWS) + (SAMPLE_ROWS - 1)
    ht_new = jnp.exp(acumx_s[pl.ds(last, 1), :]) * ht + jnp.concatenate(states, axis=1)
    for k in range(D_SSD // LANES):
        st_ref[0, k * LANES:(k + 1) * LANES, :] = ht_new[:, k * LANES:(k + 1) * LANES].T

    @pl.when(s_id == nseq - 1)
    def _():
        y = ybase[...] + yoff[...] * jnp.exp(acumx_s[...])
        y_ref[...] = _ssd_finish(y, xs_s[...], z_ref[...], dskip_ref[...], gnorm_ref[...]).astype(BF16)


def _ssd_sample(zx, tail, ctx, h0, cw, cb, dtb, alog, dskip, gnorm, tri, sel, emat):
    m = zx.shape[0]
    nt = m // SSD_CHUNK
    nseq = SSD_CHUNK // SAMPLE_ROWS
    return pl.pallas_call(
        _ssd_sample_kernel,
        grid=(nt, nseq),
        in_specs=[pl.BlockSpec((SSD_CHUNK, D_SSD), lambda t, s: (t, 0)),
                  pl.BlockSpec((SSD_CHUNK, D_SSD), lambda t, s: (t, 1)),
                  pl.BlockSpec((SSD_CHUNK, BC_DIM), lambda t, s: (t, 0)),
                  pl.BlockSpec((SSD_CHUNK, DT_PAD), lambda t, s: (t, BC_DIM // DT_PAD)),
                  pl.BlockSpec((1, SSD_CHUNK + 8, D_SSD + BC_DIM), lambda t, s: (t, 0, 0)),
                  pl.BlockSpec((1, D_SSD, SSD_STATE), lambda t, s: (t * nseq + s, 0, 0)),
                  _const_spec(cw.shape), _const_spec(cb.shape), _const_spec(dtb.shape), _const_spec(alog.shape),
                  _const_spec(dskip.shape), _const_spec(gnorm.shape), _const_spec(tri.shape),
                  _const_spec(sel.shape), _const_spec(emat.shape)],
        out_specs=[pl.BlockSpec((SSD_CHUNK, D_SSD), lambda t, s: (t, 0)),
                   pl.BlockSpec((1, D_SSD, SSD_STATE), lambda t, s: (t * nseq + s, 0, 0))],
        out_shape=[jax.ShapeDtypeStruct((m, D_SSD), BF16),
                   jax.ShapeDtypeStruct(h0.shape, F32)],
        scratch_shapes=[pltpu.VMEM((SSD_CHUNK, D_SSD), F32),
                        pltpu.VMEM((SSD_CHUNK, D_SSD), F32),
                        pltpu.VMEM((SSD_CHUNK, D_SSD), F32),
                        pltpu.VMEM((SSD_CHUNK, D_SSD), BF16),
                        pltpu.VMEM((SSD_CHUNK, D_SSD), F32),
                        pltpu.VMEM((SSD_CHUNK, SSD_GROUPS * SSD_STATE), F32),
                        pltpu.VMEM((SSD_CHUNK, SSD_GROUPS * SSD_STATE), F32)],
        compiler_params=pltpu.CompilerParams(dimension_semantics=("arbitrary", "arbitrary"),
                                             vmem_limit_bytes=VMEM_LIMIT),
        name="ssd_sample",
    )(zx, zx, tail, tail, ctx, h0, cw, cb, dtb, alog, dskip, gnorm, tri, sel, emat)


def _sb_block(qb, kb, vb, bias, uo, run, mask, nearest_first):
    z = _dot_nt(qb, kb) * ATT_SCALE + bias
    nz = -z
    lg = jnp.log(1.0 + jnp.exp(jnp.minimum(z, nz)))
    ls_pos = jnp.minimum(z, 0.0) - lg
    lm = jnp.minimum(nz, 0.0) - lg
    if mask is not None:
        lm = jnp.where(mask, lm, 0.0)
    hi = lm.astype(BF16)
    lo = (lm - hi.astype(F32)).astype(BF16)
    nsub = z.shape[1] // LANES
    tails = [None] * nsub
    for c in (range(nsub) if nearest_first else reversed(range(nsub))):
        cols = slice(c * LANES, (c + 1) * LANES)
        st = _dot(jnp.concatenate([hi[:, cols], lo[:, cols]], axis=1), uo)
        tails[c] = run + st[:, :LANES]
        run = run + st[:, LANES:]
    w = jnp.exp(ls_pos + jnp.concatenate(tails, axis=1))
    if mask is not None:
        w = jnp.where(mask, w, 0.0)
    return _dot(w.astype(BF16), vb), run


ATT_TQ = 512
ATT_TK = ATT_TQ


def _attn_prompt_kernel(bias_ref, q_ref, k_ref, v_ref, uo_ref, o_ref):
    h = pl.program_id(1)
    qi = pl.program_id(2)
    qb = q_ref[0]
    kbf = k_ref.at[0]
    vbf = v_ref.at[0]
    bias = bias_ref[h]
    uo = uo_ref[...]
    q0 = pl.multiple_of(qi * ATT_TQ, ATT_TQ)

    def diagonal():
        accs, runs = [], []
        for r in range(ATT_TQ // LANES):
            nk = (r + 1) * LANES
            row = lax.broadcasted_iota(jnp.int32, (LANES, nk), 0) + r * LANES
            col = lax.broadcasted_iota(jnp.int32, (LANES, nk), 1)
            a, rn = _sb_block(qb[r * LANES:(r + 1) * LANES], kbf[pl.ds(q0, nk), :], vbf[pl.ds(q0, nk), :], bias, uo,
                              jnp.zeros((LANES, LANES), F32), col < row, False)
            accs.append(a)
            runs.append(rn)
        return jnp.concatenate(accs, axis=0), jnp.concatenate(runs, axis=0)

    def block(kb, acc, run):
        start = pl.multiple_of(kb * ATT_TK, ATT_TK)
        pv, run = _sb_block(qb, kbf[pl.ds(start, ATT_TK), :], vbf[pl.ds(start, ATT_TK), :], bias, uo, run, None, False)
        return acc + pv, run

    @pl.when(qi == 0)
    def _():
        acc, _ = diagonal()
        o_ref[0] = acc.astype(BF16)

    @pl.when(qi > 0)
    def _():
        acc, run = block(qi - 1, *diagonal())
        acc, _ = lax.fori_loop(0, qi - 1, lambda i, carry: block(qi - 2 - i, *carry), (acc, run))
        o_ref[0] = acc.astype(BF16)


def _attn_prompt(qkv3, sb_bias, uo):
    nb, seq, _ = qkv3.shape
    tq = ATT_TQ
    return pl.pallas_call(
        _attn_prompt_kernel,
        grid=(nb, ATT_HEADS, seq // tq),
        in_specs=[pl.BlockSpec(memory_space=pltpu.SMEM),
                  pl.BlockSpec((1, tq, ATT_HEAD_DIM), lambda b, h, i: (b, i, h)),
                  pl.BlockSpec((1, seq, ATT_HEAD_DIM), lambda b, h, i: (b, 0, ATT_HEADS + h)),
                  pl.BlockSpec((1, seq, ATT_HEAD_DIM), lambda b, h, i: (b, 0, 2 * ATT_HEADS + h)),
                  _const_spec(uo.shape)],
        out_specs=pl.BlockSpec((1, tq, ATT_HEAD_DIM), lambda b, h, i: (b, i, h)),
        out_shape=jax.ShapeDtypeStruct((nb, seq, D_ATT), BF16),
        compiler_params=pltpu.CompilerParams(dimension_semantics=("arbitrary", "arbitrary", "arbitrary"),
                                             vmem_limit_bytes=VMEM_LIMIT),
        name="attn_prompt",
    )(sb_bias, qkv3, qkv3, qkv3, uo)


PAGES_PER_STEP = 8
PAGE_SLOTS = 3


def _page_by_head(ref):
    heads = [ref[pl.ds(h, PAGE_SIZE, stride=ATT_HEADS), :] for h in range(ATT_HEADS)]
    return jnp.concatenate(heads, axis=1).astype(BF16)


def _attn_sample_kernel(nseq, steps, pt_ref, q_ref, kn_ref, vn_ref, bias_ref, uo_ref, ck_hbm, cv_hbm, o_ref,
                        qbd, knew, vnew, acc, run, kbuf, vbuf, sems):
    b = pl.program_id(0)
    j = pl.program_id(1)
    total = nseq * steps
    t = b * steps + j
    nrow = ATT_HEADS * SAMPLE_ROWS
    bias = bias_ref[...]
    uo = uo_ref[...]

    def page_copies(step):
        sb = step // steps
        sj = step - sb * steps
        slot = step % PAGE_SLOTS
        copies = []
        for g in range(PAGES_PER_STEP):
            page = pt_ref[sb, steps * PAGES_PER_STEP - 1 - (sj * PAGES_PER_STEP + g)]
            copies.append(pltpu.make_async_copy(ck_hbm.at[page], kbuf.at[slot, g], sems.at[0, slot]))
            copies.append(pltpu.make_async_copy(cv_hbm.at[page], vbuf.at[slot, g], sems.at[1, slot]))
        return copies

    @pl.when(t == 0)
    def _():
        for step in range(PAGE_SLOTS - 1):
            for c in page_copies(step):
                c.start()

    @pl.when(t + (PAGE_SLOTS - 1) < total)
    def _():
        for c in page_copies(t + (PAGE_SLOTS - 1)):
            c.start()

    @pl.when((b == 0) & (j == 0))
    def _():
        knew[...] = jnp.zeros_like(knew)
        vnew[...] = jnp.zeros_like(vnew)

    @pl.when(j == 0)
    def _():
        q8 = q_ref[0]
        rr = lax.broadcasted_iota(jnp.int32, (nrow, D_ATT), 0)
        cc = lax.broadcasted_iota(jnp.int32, (nrow, D_ATT), 1)
        qt = jnp.concatenate([q8] * ATT_HEADS, axis=0)
        qbd[...] = jnp.where((rr // SAMPLE_ROWS) == (cc // ATT_HEAD_DIM), qt, 0.0).astype(BF16)
        knew[0:SAMPLE_ROWS, :] = kn_ref[0]
        vnew[0:SAMPLE_ROWS, :] = vn_ref[0]
        qq = lax.broadcasted_iota(jnp.int32, (nrow, LANES), 0) % SAMPLE_ROWS
        kk = lax.broadcasted_iota(jnp.int32, (nrow, LANES), 1)
        mask = (kk < qq) & (kk < SAMPLE_ROWS // 2)
        pv, r = _sb_block(qbd[...], knew[...].astype(BF16), vnew[...].astype(BF16), bias, uo,
                          jnp.zeros((nrow, LANES), F32), mask, True)
        acc[...] = pv
        run[...] = r

    for c in page_copies(t):
        c.wait()
    slot = t % PAGE_SLOTS
    kcat = jnp.concatenate([_page_by_head(kbuf.at[slot, g]) for g in range(PAGES_PER_STEP)], axis=0)
    vcat = jnp.concatenate([_page_by_head(vbuf.at[slot, g]) for g in range(PAGES_PER_STEP)], axis=0)
    pv, r = _sb_block(qbd[...], kcat, vcat, jnp.concatenate([bias] * PAGES_PER_STEP, axis=1), uo, run[...], None, True)
    a = acc[...] + pv
    acc[...] = a
    run[...] = r

    @pl.when(j == steps - 1)
    def _():
        for h in range(ATT_HEADS):
            o_ref[0, :, h * ATT_HEAD_DIM:(h + 1) * ATT_HEAD_DIM] = (
                a[h * SAMPLE_ROWS:(h + 1) * SAMPLE_ROWS, h * ATT_HEAD_DIM:(h + 1) * ATT_HEAD_DIM])


def _attn_sample(page_table, projs3, cache_k, cache_v, bias_rows, uo):
    nseq, npages = page_table.shape
    steps = npages // PAGES_PER_STEP
    assert steps * PAGES_PER_STEP == npages and nseq * steps >= PAGE_SLOTS - 1
    page_buf = pltpu.VMEM((PAGE_SLOTS, PAGES_PER_STEP, PAGE_SIZE * ATT_HEADS, ATT_HEAD_DIM), F32)

    grid_spec = pltpu.PrefetchScalarGridSpec(
        num_scalar_prefetch=1,
        grid=(nseq, steps),
        in_specs=[pl.BlockSpec((1, SAMPLE_ROWS, D_ATT), lambda b, j, pt: (b, 0, 0)),
                  pl.BlockSpec((1, SAMPLE_ROWS, D_ATT), lambda b, j, pt: (b, 0, 1)),
                  pl.BlockSpec((1, SAMPLE_ROWS, D_ATT), lambda b, j, pt: (b, 0, 2)),
                  pl.BlockSpec(bias_rows.shape, lambda b, j, pt: (0, 0)),
                  pl.BlockSpec(uo.shape, lambda b, j, pt: (0, 0)),
                  pl.BlockSpec(memory_space=pl.ANY), pl.BlockSpec(memory_space=pl.ANY)],
        out_specs=pl.BlockSpec((1, SAMPLE_ROWS, D_ATT), lambda b, j, pt: (b, 0, 0)),
        scratch_shapes=[pltpu.VMEM((ATT_HEADS * SAMPLE_ROWS, D_ATT), BF16),
                        pltpu.VMEM((PAGE_SIZE, D_ATT), F32),
                        pltpu.VMEM((PAGE_SIZE, D_ATT), F32),
                        pltpu.VMEM((ATT_HEADS * SAMPLE_ROWS, D_ATT), F32),
                        pltpu.VMEM((ATT_HEADS * SAMPLE_ROWS, LANES), F32),
                        page_buf, page_buf, pltpu.SemaphoreType.DMA((2, PAGE_SLOTS))])
    return pl.pallas_call(
        functools.partial(_attn_sample_kernel, nseq, steps),
        grid_spec=grid_spec,
        out_shape=jax.ShapeDtypeStruct((nseq, SAMPLE_ROWS, D_ATT), F32),
        compiler_params=pltpu.CompilerParams(dimension_semantics=("arbitrary", "arbitrary"),
                                             vmem_limit_bytes=VMEM_LIMIT),
        name="attn_sample",
    )(page_table, projs3, projs3, projs3, bias_rows, uo, cache_k, cache_v)


def _outproj_kernel(y_ref, o_ref, x_ref, g1_ref, sh2_ref, sc2_ref, gpost_ref, gpre_ref, wy_ref, wo_ref, x1_ref, h2_ref):
    tm = x_ref.shape[0]
    chunk = min(tm, 256)
    for r in range(0, tm, chunk):
        rows = slice(r, r + chunk)
        g1, sh2, sc2 = (v if v.shape[0] == 1 else v[rows] for v in (g1_ref[0], sh2_ref[0], sc2_ref[0]))
        m = _dot(y_ref[rows, :], wy_ref[...]) + _dot(o_ref[rows, :].astype(BF16), wo_ref[...])
        x1 = x_ref[rows, :] + g1 * _rms(m, gpost_ref[...])
        x1_ref[rows, :] = x1
        h2 = _rms(x1, gpre_ref[...]) * (1.0 + sc2) + sh2
        h2_ref[rows, :] = h2.astype(BF16)


def _outproj(y, o, x, mods, gpost, gpre, w_out, tm, tiles_per_group):
    m = x.shape[0]
    rows = mods.shape[1]
    half = w_out.shape[0] // 2
    row = lambda i: (i, 0)
    return pl.pallas_call(
        _outproj_kernel,
        grid=(m // tm,),
        in_specs=[pl.BlockSpec((tm, D_SSD), row), pl.BlockSpec((tm, D_ATT), row), pl.BlockSpec((tm, D_MODEL), row),
                  _mod_spec(2, rows, tiles_per_group), _mod_spec(3, rows, tiles_per_group),
                  _mod_spec(4, rows, tiles_per_group),
                  _const_spec(gpost.shape), _const_spec(gpre.shape),
                  pl.BlockSpec((half, D_MODEL), lambda i: (0, 0)),
                  pl.BlockSpec((half, D_MODEL), lambda i: (1, 0))],
        out_specs=[pl.BlockSpec((tm, D_MODEL), row), pl.BlockSpec((tm, D_MODEL), row)],
        out_shape=[jax.ShapeDtypeStruct((m, D_MODEL), F32), jax.ShapeDtypeStruct((m, D_MODEL), BF16)],
        compiler_params=pltpu.CompilerParams(dimension_semantics=("arbitrary",), vmem_limit_bytes=VMEM_LIMIT),
        name="out_proj",
    )(y, o, x, mods, mods, mods, gpost, gpre, w_out, w_out)


FFN_CHUNK = 256


def _ffn_kernel(sample, tiles_per_group, *refs):
    if sample:
        (h_ref, x1_hbm, g2_ref, gpost_ref, wg_ref, wv_ref, wo_ref, cwg_ref, cwv_ref, cbg_ref, cbv_ref,
         injg_ref, injv_ref, y_ref, ug_ref, uv_ref, x1_buf, x1_sem) = refs
    else:
        (h_ref, x1_hbm, g2_ref, gpost_ref, wg_ref, wv_ref, wo_ref, cwg_ref, cwv_ref, cbg_ref, cbv_ref,
         y_ref, ug_ref, uv_ref, x1_buf, x1_sem, halog, halov) = refs
    i = pl.program_id(0)
    j = pl.program_id(1)
    tm = h_ref.shape[0]
    tf = wg_ref.shape[1]
    hn = h_ref[...]
    x1_copy = pltpu.make_async_copy(x1_hbm.at[pl.ds(pl.multiple_of(i * tm, tm), tm), :], x1_buf, x1_sem)

    @pl.when(j == 0)
    def _():
        x1_copy.start()
        y_ref[...] = jnp.zeros_like(y_ref)

    if sample:
        is_ctx = (lax.broadcasted_iota(jnp.int32, (tm, 1), 0) % SAMPLE_ROWS) >= SAMPLE_ROWS - (FFN_CONV - 1)
    else:
        @pl.when((i == 0) & (j == 0))
        def _():
            halog[...] = jnp.zeros_like(halog)
            halov[...] = jnp.zeros_like(halov)

        first = lax.rem(i, tiles_per_group) == 0

    def half(w_ref, cw_ref, cb_ref, u_ref, inj_ref, halo, cols):
        u = jnp.concatenate([_dot(hn[r:r + 128], w_ref[:, cols]) for r in range(0, tm, 128)], axis=0)
        if sample:
            u_ref[:, cols] = u
            prev = inj_ref[0:8, cols]
            u = jnp.where(is_ctx, inj_ref[8:, cols], u)
        else:
            prev = jnp.where(first, 0.0, halo[j, :, cols])
            halo[j, :, cols] = u[tm - 8:, :]
            u_ref[0, :, cols] = u[tm - 8:, :]
        return _conv_rows(u, prev, cw_ref[:, cols], cb_ref[:, cols])

    acts = []
    for c in range(tf // FFN_CHUNK):
        cols = slice(c * FFN_CHUNK, (c + 1) * FFN_CHUNK)
        gate = half(wg_ref, cwg_ref, cbg_ref, ug_ref, None if not sample else injg_ref, None if sample else halog, cols)
        val = half(wv_ref, cwv_ref, cbv_ref, uv_ref, None if not sample else injv_ref, None if sample else halov, cols)
        acts.append((_silu(gate) * val).astype(BF16))
    y_ref[...] += _dot(jnp.concatenate(acts, axis=1), wo_ref[...])

    @pl.when(j == pl.num_programs(1) - 1)
    def _():
        x1_copy.wait()
        chunk = min(tm, 256)
        for r in range(0, tm, chunk):
            rows = slice(r, r + chunk)
            g2 = g2_ref[0]
            g2 = g2 if g2.shape[0] == 1 else g2[rows]
            y_ref[rows, :] = x1_buf[rows, :] + g2 * _rms(y_ref[rows, :], gpost_ref[...])


def _ffn(h2, x1, mods, gpost, w_in, w_out, cw, cb, tm, tiles_per_group, inj=None):
    m = h2.shape[0]
    rows = mods.shape[1]
    tf = 512
    nf = D_FF // tf
    sample = inj is not None
    row = lambda i, j: (i, 0)
    gate_col = lambda i, j: (0, j)
    val_col = lambda i, j: (0, nf + j)
    in_specs = [pl.BlockSpec((tm, D_MODEL), row), pl.BlockSpec(memory_space=pl.ANY),
                _mod_spec(5, rows, tiles_per_group), _const_spec(gpost.shape),
                pl.BlockSpec((D_MODEL, tf), gate_col), pl.BlockSpec((D_MODEL, tf), val_col),
                pl.BlockSpec((tf, D_MODEL), lambda i, j: (j, 0)),
                pl.BlockSpec((FFN_CONV, tf), gate_col), pl.BlockSpec((FFN_CONV, tf), val_col),
                pl.BlockSpec((1, tf), gate_col), pl.BlockSpec((1, tf), val_col)]
    args = [h2, x1, mods, gpost, w_in, w_in, w_out, cw, cw, cb, cb]
    scratch = [pltpu.VMEM((tm, D_MODEL), F32), pltpu.SemaphoreType.DMA(())]
    if sample:
        in_specs += [pl.BlockSpec((m + 8, tf), gate_col), pl.BlockSpec((m + 8, tf), val_col)]
        args += [inj, inj]
        u_specs = [pl.BlockSpec((tm, tf), lambda i, j: (i, j))] * 2
        u_shapes = [jax.ShapeDtypeStruct((m, D_FF), F32)] * 2
    else:
        u_specs = [pl.BlockSpec((1, 8, tf), lambda i, j: (i, 0, j))] * 2
        u_shapes = [jax.ShapeDtypeStruct((m // tm, 8, D_FF), F32)] * 2
        scratch += [pltpu.VMEM((nf, 8, tf), F32), pltpu.VMEM((nf, 8, tf), F32)]
    return pl.pallas_call(
        functools.partial(_ffn_kernel, sample, tiles_per_group),
        grid=(m // tm, nf),
        in_specs=in_specs,
        out_specs=[pl.BlockSpec((tm, D_MODEL), row)] + u_specs,
        out_shape=[jax.ShapeDtypeStruct((m, D_MODEL), F32)] + u_shapes,
        scratch_shapes=scratch,
        compiler_params=pltpu.CompilerParams(dimension_semantics=("arbitrary", "arbitrary"),
                                             vmem_limit_bytes=VMEM_LIMIT),
        name="ffn_sample" if sample else "ffn_prompt",
    )(*args)


def _ssd_constants(block_rows):
    r = jnp.arange(SSD_CHUNK)[:, None]
    c = jnp.arange(SSD_CHUNK)[None, :]
    same = (r // block_rows) == (c // block_rows)
    tri = ((c <= r) & same).astype(BF16)
    sel = (c == (r // block_rows) * block_rows + block_rows - 1).astype(BF16)
    hrow = jnp.arange(LANES)[:, None]
    e1 = (hrow == jnp.arange(D_SSD)[None, :] // SSD_HEAD_DIM).astype(BF16)
    e2 = (hrow == jnp.arange(SSD_HEADS * LANES)[None, :] // LANES).astype(BF16)
    return tri, sel, jnp.concatenate([e1, e2], axis=1)


def _suffix_matrix():
    j = jnp.arange(LANES)[:, None]
    s = jnp.arange(LANES)[None, :]
    u = jnp.concatenate([(j > s).astype(BF16), jnp.ones((LANES, LANES), BF16)], axis=1)
    return jnp.concatenate([u, u], axis=0)


def _inject_rows(ctx, nctx):
    nseq, _, c = ctx.shape
    body = jnp.concatenate([ctx[1:], jnp.zeros((1, nctx, c), F32)], axis=0)
    body = jnp.pad(body, ((0, 0), (SAMPLE_ROWS - nctx, 0), (0, 0))).reshape(nseq * SAMPLE_ROWS, c)
    head = jnp.pad(ctx[0], ((8 - nctx, 0), (0, 0)))
    return jnp.concatenate([head, body], axis=0)


def kernel(x_prompt, x_sample, c_prompt, c_sample, cache_k, cache_v, page_table, state_ssm, state_conv, state_ffn_conv, w_ada, b_ada, g_pre_mix, g_post_mix, g_pre_ffn, g_post_ffn, w_in, conv_w, conv_b, dt_bias, a_log, d_skip, g_ssd_norm, sb_bias, w_out, w_ffn_in, ffn_conv_w, ffn_conv_b, w_ffn_out):
    assert w_ada.shape[0] == 1, "one layer"
    nb, seq, d = x_prompt.shape
    ns, ls, _ = x_sample.shape
    n_pool = cache_k.shape[1]
    mp = nb * seq
    ms = ns * SAMPLE_ROWS

    w = w_in[0].astype(BF16)
    o_bc, o_dt, o_q = 2 * D_SSD, 2 * D_SSD + BC_DIM, 2 * D_SSD + BC_DIM + SSD_HEADS
    w_main = jnp.concatenate([w[:, :o_bc], w[:, o_q:]], axis=1)
    w_tail = jnp.concatenate([w[:, o_bc:o_q], jnp.zeros((d, DT_PAD - SSD_HEADS), BF16)], axis=1)
    w_out_b = w_out[0].astype(BF16)
    w_ffn_in_b = w_ffn_in[0].astype(BF16)
    w_ffn_out_b = w_ffn_out[0].astype(BF16)
    row = lambda v: v.reshape(1, -1)
    dtb = jnp.pad(dt_bias[0], (0, DT_PAD - SSD_HEADS)).reshape(1, DT_PAD)
    alog = jnp.pad(a_log[0], (0, DT_PAD - SSD_HEADS)).reshape(1, DT_PAD)
    dskip = jnp.repeat(d_skip[0], SSD_HEAD_DIM).reshape(1, D_SSD)
    gnorm = row(g_ssd_norm[0])
    cw, cb = conv_w[0], row(conv_b[0])
    fcw, fcb = ffn_conv_w[0], row(ffn_conv_b[0])
    uo = _suffix_matrix()

    n_c = nb + ns
    c_all = jnp.concatenate([c_prompt, c_sample, jnp.zeros((-n_c % 16, d), F32)], axis=0)
    mod = _ada(c_all, w_ada[0], row(b_ada[0]))
    mods_p = mod[:nb].reshape(nb, 1, 6 * d)
    mods_s = jnp.repeat(mod[nb:n_c], SAMPLE_ROWS, axis=0).reshape(1, ms, 6 * d)

    tm_p = 512
    tpg_p = seq // tm_p
    xp = x_prompt.reshape(mp, d)
    zx_p, qkv_p, tail_p, k_p, v_p = _inproj(xp, mods_p, row(g_pre_mix[0]), w_main, w_tail, tm_p, tpg_p, BF16)
    zx3 = zx_p.reshape(nb, seq, 2 * D_SSD)
    tail3 = tail_p.reshape(nb, seq, PROJ_TAIL)
    tri, sel, emat = _ssd_constants(SSD_CHUNK)
    y_ssd, ssm_p = _ssd_prompt(zx3, tail3, cw, cb, dtb, alog, dskip, gnorm, tri, sel, emat)
    o_att = _attn_prompt(qkv_p.reshape(nb, seq, 3 * D_ATT), sb_bias[0], uo)
    tm_e = 512
    x1_p, h2_p = _outproj(y_ssd.reshape(mp, D_SSD), o_att.reshape(mp, D_ATT), xp, mods_p, row(g_post_mix[0]),
                          row(g_pre_ffn[0]), w_out_b, tm_e, seq // tm_e)
    tm_f = 1024
    yp, ctxg_p, ctxv_p = _ffn(h2_p, x1_p, mods_p, row(g_post_ffn[0]), w_ffn_in_b, w_ffn_out_b, fcw, fcb,
                              tm_f, seq // tm_f)

    xs_pad = jnp.pad(x_sample, ((0, 0), (0, SAMPLE_ROWS - ls), (0, 0))).reshape(ms, d)
    zx_s, qkv_s, tail_s, k_s, v_s = _inproj(xs_pad, mods_s, row(g_pre_mix[0]), w_main, w_tail, ms, 1, F32)
    seq_per_tile = SSD_CHUNK // SAMPLE_ROWS
    conv_ctx = state_conv[0].reshape(ns // seq_per_tile, seq_per_tile, SSD_CONV - 1, -1)
    ctx_inj = jax.vmap(lambda c: _inject_rows(c, SSD_CONV - 1))(conv_ctx)
    tri_s, sel_s, _ = _ssd_constants(SAMPLE_ROWS)
    y_ssd_s, ssm_s = _ssd_sample(zx_s, tail_s, ctx_inj, state_ssm[0].reshape(ns, D_SSD, SSD_STATE), cw, cb, dtb, alog,
                                 dskip, gnorm, tri_s, sel_s, emat)
    bias_rows = jnp.broadcast_to(jnp.repeat(sb_bias[0], SAMPLE_ROWS)[:, None], (ATT_HEADS * SAMPLE_ROWS, LANES))
    o_att_s = _attn_sample(page_table, qkv_s.reshape(ns, SAMPLE_ROWS, 3 * D_ATT),
                           cache_k[0].reshape(n_pool, PAGE_SIZE * ATT_HEADS, ATT_HEAD_DIM),
                           cache_v[0].reshape(n_pool, PAGE_SIZE * ATT_HEADS, ATT_HEAD_DIM), bias_rows, uo)
    x1_s, h2_s = _outproj(y_ssd_s, o_att_s.reshape(ms, D_ATT), xs_pad, mods_s, row(g_post_mix[0]),
                          row(g_pre_ffn[0]), w_out_b, ms, 1)
    inj_f = _inject_rows(state_ffn_conv[0], FFN_CONV - 1)
    ys_pad, ug_s, uv_s = _ffn(h2_s, x1_s, mods_s, row(g_post_ffn[0]), w_ffn_in_b, w_ffn_out_b, fcw, fcb,
                              ms, 1, inj=inj_f)

    heads = (ATT_HEADS, ATT_HEAD_DIM)
    k_p = k_p.reshape(1, nb, seq, *heads)
    v_p = v_p.reshape(1, nb, seq, *heads)
    tail = slice(seq - (SSD_CONV - 1), seq)
    conv_p = jnp.concatenate([zx3[:, tail, D_SSD:], tail3[:, tail, :BC_DIM]], axis=-1)[None]
    ffn_p = jnp.concatenate([ctxg_p, ctxv_p], axis=-1)[seq // tm_f - 1::seq // tm_f, 8 - (FFN_CONV - 1):][None]
    k_s = k_s.reshape(ns, SAMPLE_ROWS, *heads)[None, :, :ls]
    v_s = v_s.reshape(ns, SAMPLE_ROWS, *heads)[None, :, :ls]
    raw_s = jnp.concatenate([zx_s.reshape(ns, SAMPLE_ROWS, -1)[:, :ls, D_SSD:],
                             tail_s.reshape(ns, SAMPLE_ROWS, -1)[:, :ls, :BC_DIM]], axis=-1)
    conv_s = jnp.concatenate([state_conv[0], raw_s], axis=1)[:, -(SSD_CONV - 1):][None]
    u_s = jnp.concatenate([ug_s, uv_s], axis=-1).reshape(ns, SAMPLE_ROWS, 2 * D_FF)[:, :ls]
    ffn_s = jnp.concatenate([state_ffn_conv[0], u_s], axis=1)[:, -(FFN_CONV - 1):][None]
    return (yp.reshape(nb, seq, d), ys_pad.reshape(ns, SAMPLE_ROWS, d)[:, :ls],
            k_p, v_p, ssm_p.reshape(1, nb, SSD_HEADS, SSD_HEAD_DIM, SSD_STATE), conv_p, ffn_p,
            k_s, v_s, ssm_s.reshape(1, ns, SSD_HEADS, SSD_HEAD_DIM, SSD_STATE), conv_s, ffn_s)
```

```python
import functools

import jax
import jax.numpy as jnp
from jax import lax
from jax.experimental import pallas as pl
from jax.experimental.pallas import tpu as pltpu

F32 = jnp.float32
BF16 = jnp.bfloat16

D_MODEL = 2048
D_SSD = 1024
SSD_HEAD_DIM = 64
SSD_HEADS = 16
SSD_GROUPS = 2
SSD_STATE = 128
SSD_CONV = 4
SSD_CHUNK = 128
BC_DIM = 2 * SSD_GROUPS * SSD_STATE
D_ATT = 1024
ATT_HEAD_DIM = 128
ATT_HEADS = 8
D_FF = 5632
FFN_CONV = 3
PAGE_SIZE = 128
EPS = 1e-6
ATT_SCALE = ATT_HEAD_DIM ** -0.5

PROJ_TN = 1024
PROJ_MAIN = 5 * PROJ_TN
DT_PAD = 128
PROJ_TAIL = BC_DIM + DT_PAD

SAMPLE_ROWS = 8
LANES = 128
VMEM_LIMIT = 56 * 1024 * 1024


def _dot(a, b):
    return jnp.dot(a, b, preferred_element_type=F32)


def _dot_nt(a, b):
    return lax.dot_general(a, b, (((1,), (1,)), ((), ())), preferred_element_type=F32)


def _sigmoid(x):
    return 1.0 / (1.0 + jnp.exp(-x))


def _silu(x):
    return x * _sigmoid(x)


def _softplus(x):
    return jnp.maximum(x, 0.0) + jnp.log1p(jnp.exp(-jnp.abs(x)))


def _rms(x, g):
    ms = jnp.mean(x * x, axis=-1, keepdims=True)
    return x * lax.rsqrt(ms + EPS) * g


def _split3(x):
    h = x.astype(BF16)
    r = x - h.astype(F32)
    m = r.astype(BF16)
    l = (r - m.astype(F32)).astype(BF16)
    return h, m, l


def _dot3_right(x, mat):
    h, m, l = _split3(x)
    return _dot(h, mat) + _dot(m, mat) + _dot(l, mat)


def _dot3_left(mat, x):
    h, m, l = _split3(x)
    return _dot(mat, h) + _dot(mat, m) + _dot(mat, l)


def _ada_kernel(c_ref, w_ref, b_ref, o_ref):
    s = _silu(c_ref[...]).astype(BF16)
    o_ref[...] = _dot(s, w_ref[...].astype(BF16)) + b_ref[...]


def _ada(c_all, w_ada, b_ada):
    m, d = c_all.shape
    n = w_ada.shape[1]
    tn = 1024
    return pl.pallas_call(
        _ada_kernel,
        grid=(n // tn,),
        in_specs=[pl.BlockSpec((m, d), lambda j: (0, 0)),
                  pl.BlockSpec((d, tn), lambda j: (0, j)),
                  pl.BlockSpec((1, tn), lambda j: (0, j))],
        out_specs=pl.BlockSpec((m, tn), lambda j: (0, j)),
        out_shape=jax.ShapeDtypeStruct((m, n), F32),
        compiler_params=pltpu.CompilerParams(dimension_semantics=("arbitrary",), vmem_limit_bytes=VMEM_LIMIT),
        name="ada_mod",
    )(c_all, w_ada, b_ada)


def _mod_spec(idx, rows, tiles_per_group):
    return pl.BlockSpec((1, rows, D_MODEL), lambda i, *_: (i // tiles_per_group, 0, idx))


def _prenorm_kernel(x_ref, sh_ref, sc_ref, g_ref, wt_ref, hn_ref, tail_ref):
    tm = x_ref.shape[0]
    chunk = min(tm, 128)
    g = g_ref[...]
    for r in range(0, tm, chunk):
        sc, sh = sc_ref[0], sh_ref[0]
        if sc.shape[0] != 1:
            sc, sh = sc[r:r + chunk], sh[r:r + chunk]
        h = (_rms(x_ref[r:r + chunk, :], g) * (1.0 + sc) + sh).astype(BF16)
        hn_ref[r:r + chunk, :] = h
        tail_ref[r:r + chunk, :] = _dot(h, wt_ref[...])


def _prenorm(x, mods, g, wt, tm, tiles_per_group):
    m = x.shape[0]
    rows = mods.shape[1]
    return pl.pallas_call(
        _prenorm_kernel,
        grid=(m // tm,),
        in_specs=[pl.BlockSpec((tm, D_MODEL), lambda i: (i, 0)),
                  _mod_spec(0, rows, tiles_per_group),
                  _mod_spec(1, rows, tiles_per_group),
                  _const_spec(g.shape), _const_spec(wt.shape)],
        out_specs=[pl.BlockSpec((tm, D_MODEL), lambda i: (i, 0)), pl.BlockSpec((tm, PROJ_TAIL), lambda i: (i, 0))],
        out_shape=[jax.ShapeDtypeStruct((m, D_MODEL), BF16), jax.ShapeDtypeStruct((m, PROJ_TAIL), F32)],
        compiler_params=pltpu.CompilerParams(dimension_semantics=("arbitrary",), vmem_limit_bytes=VMEM_LIMIT),
        name="pre_norm",
    )(x, mods, mods, g, wt)


def _inproj_kernel(hn_ref, w_ref, zx_ref, qkv_ref, k_ref, v_ref):
    j = pl.program_id(1)
    tm = hn_ref.shape[0]

    def project(state_ref):
        res = _dot(hn_ref[...], w_ref[...])
        qkv_ref[...] = res.astype(qkv_ref.dtype)
        if state_ref is not None:
            for h in range(ATT_HEADS):
                state_ref[pl.ds(h, tm, stride=ATT_HEADS), :] = res[:, h * ATT_HEAD_DIM:(h + 1) * ATT_HEAD_DIM]

    @pl.when(j < 2)
    def _():
        zx_ref[...] = _dot(hn_ref[...], w_ref[...])

    @pl.when(j == 2)
    def _():
        project(None)

    @pl.when(j == 3)
    def _():
        project(k_ref)

    @pl.when(j == 4)
    def _():
        project(v_ref)


def _inproj(hn, w, tm, qkv_dtype):
    m = hn.shape[0]
    kv_spec = pl.BlockSpec((tm * ATT_HEADS, ATT_HEAD_DIM), lambda i, j: (i, 0))
    kv_shape = jax.ShapeDtypeStruct((m * ATT_HEADS, ATT_HEAD_DIM), F32)
    return pl.pallas_call(
        _inproj_kernel,
        grid=(m // tm, PROJ_MAIN // PROJ_TN),
        in_specs=[pl.BlockSpec((tm, D_MODEL), lambda i, j: (i, 0)),
                  pl.BlockSpec((D_MODEL, PROJ_TN), lambda i, j: (0, j))],
        out_specs=[pl.BlockSpec((tm, PROJ_TN), lambda i, j: (i, jnp.minimum(j, 1))),
                   pl.BlockSpec((tm, PROJ_TN), lambda i, j: (i, jnp.clip(j - 2, 0, 2))),
                   kv_spec, kv_spec],
        out_shape=[jax.ShapeDtypeStruct((m, 2 * PROJ_TN), F32),
                   jax.ShapeDtypeStruct((m, 3 * PROJ_TN), qkv_dtype), kv_shape, kv_shape],
        compiler_params=pltpu.CompilerParams(dimension_semantics=("arbitrary", "arbitrary"),
                                             vmem_limit_bytes=VMEM_LIMIT),
        name="in_proj",
    )(hn, w)


def _shift_rows(u, prev8, s):
    r = pltpu.roll(u, s, 0)
    head = jnp.where(lax.broadcasted_iota(jnp.int32, (8, 1), 0) < s, pltpu.roll(prev8, s, 0), r[:8])
    return jnp.concatenate([head, r[8:]], axis=0)


def _conv_rows(u, prev8, w, b):
    taps = w.shape[0]
    y = _shift_rows(u, prev8, taps - 1) * w[0:1]
    for t in range(1, taps):
        s = taps - 1 - t
        y = y + (_shift_rows(u, prev8, s) if s else u) * w[t:t + 1]
    return y + b


def _ssd_tile(xs, bc, dtr, row_valid, causal, tri, sel, emat, dtb, alog):
    dt = _softplus(dtr + dtb)
    if row_valid is not None:
        dt = jnp.where(row_valid, dt, 0.0)
    a = -jnp.exp(alog)
    acum = _dot3_left(tri, dt * a)
    alast = _dot3_left(sel, acum)
    acx = _dot3_right(acum, emat)
    acum_x = acx[:, :D_SSD]
    acum_x2 = acx[:, D_SSD:]
    e1 = emat[:, :D_SSD]
    dt_x = _dot3_right(dt, e1)
    alast_x = _dot3_right(alast, e1)
    acum_t = acum.T

    xdt = xs * dt_x
    xdt_bf = xdt.astype(BF16)
    xdtd_bf = (xdt * jnp.exp(alast_x - acum_x)).astype(BF16)
    bm = bc[:, :SSD_GROUPS * SSD_STATE]
    cm = bc[:, SSD_GROUPS * SSD_STATE:]
    lane = lax.broadcasted_iota(jnp.int32, (SSD_CHUNK, LANES), 1)
    first_head = lane < SSD_HEAD_DIM

    pieces = []
    heads_per_group = SSD_HEADS // SSD_GROUPS
    for g in range(SSD_GROUPS):
        bg = bm[:, g * SSD_STATE:(g + 1) * SSD_STATE].astype(BF16)
        cg = cm[:, g * SSD_STATE:(g + 1) * SSD_STATE].astype(BF16)
        cb = _dot_nt(cg, bg)
        for pair in range(heads_per_group // 2):
            h0 = g * heads_per_group + 2 * pair
            ys = []
            for hh in (h0, h0 + 1):
                seg = acum_x2[:, hh * LANES:(hh + 1) * LANES] - acum_t[hh:hh + 1, :]
                decay = jnp.where(causal, jnp.exp(jnp.where(causal, seg, 0.0)), 0.0)
                sc = (cb * decay).astype(BF16)
                ys.append(_dot(sc, xdt_bf[:, (h0 // 2) * LANES:(h0 // 2 + 1) * LANES]))
            pieces.append(jnp.where(first_head, ys[0], ys[1]))
    y_diag = jnp.concatenate(pieces, axis=1)
    return y_diag, xdtd_bf, acum_x, bm, cm


def _ssd_finish(y, xs, z, dskip, gnorm):
    y = y + dskip * xs
    y = y * _silu(z)
    half = D_SSD // SSD_GROUPS
    outs = []
    for g in range(SSD_GROUPS):
        outs.append(_rms(y[:, g * half:(g + 1) * half], gnorm[:, g * half:(g + 1) * half]))
    return jnp.concatenate(outs, axis=1)


def _ssd_prompt_kernel(z_ref, xs_ref, bc_ref, dt_ref, cw_ref, cb_ref, dtb_ref, alog_ref, dskip_ref, gnorm_ref,
                       tri_ref, sel_ref, emat_ref, y_ref, st_ref, ht_ref, prev_xs, prev_bc):
    c = pl.program_id(1)
    rows = SSD_CHUNK

    @pl.when(c == 0)
    def _():
        ht_ref[...] = jnp.zeros_like(ht_ref)
        prev_xs[...] = jnp.zeros_like(prev_xs)
        prev_bc[...] = jnp.zeros_like(prev_bc)

    xr = xs_ref[0]
    br = bc_ref[0]
    cw = cw_ref[...]
    cb = cb_ref[...]
    xs = _silu(_conv_rows(xr, prev_xs[...], cw[:, :D_SSD], cb[:, :D_SSD]))
    bc = _silu(_conv_rows(br, prev_bc[...], cw[:, D_SSD:], cb[:, D_SSD:]))
    prev_xs[...] = xr[rows - 8:, :]
    prev_bc[...] = br[rows - 8:, :]

    r = lax.broadcasted_iota(jnp.int32, (rows, rows), 0)
    s = lax.broadcasted_iota(jnp.int32, (rows, rows), 1)
    causal = s <= r
    y_diag, xdtd_bf, acum_x, bm, cm = _ssd_tile(xs, bc, dt_ref[0], None, causal, tri_ref[...], sel_ref[...],
                                                 emat_ref[...], dtb_ref[...], alog_ref[...])
    ht = ht_ref[...]
    half = D_SSD // SSD_GROUPS
    y_off, states = [], []
    for g in range(SSD_GROUPS):
        bg = bm[:, g * SSD_STATE:(g + 1) * SSD_STATE]
        cg = cm[:, g * SSD_STATE:(g + 1) * SSD_STATE].astype(BF16)
        y_off.append(_dot(cg, ht[:, g * half:(g + 1) * half].astype(BF16)))
        states.append(_dot(bg.T.astype(BF16), xdtd_bf[:, g * half:(g + 1) * half]))
    y = y_diag + jnp.concatenate(y_off, axis=1) * jnp.exp(acum_x)
    ht_new = jnp.exp(acum_x[rows - 1:rows, :]) * ht + jnp.concatenate(states, axis=1)
    ht_ref[...] = ht_new
    y_ref[0] = _ssd_finish(y, xs, z_ref[0], dskip_ref[...], gnorm_ref[...]).astype(BF16)

    @pl.when(c == pl.num_programs(1) - 1)
    def _():
        for k in range(D_SSD // LANES):
            st_ref[0, k * LANES:(k + 1) * LANES, :] = ht_new[:, k * LANES:(k + 1) * LANES].T


def _const_spec(shape):
    nd = len(shape)
    return pl.BlockSpec(shape, lambda *_: (0,) * nd)


def _ssd_prompt(zx3, tail3, cw, cb, dtb, alog, dskip, gnorm, tri, sel, emat):
    nb, seq, _ = zx3.shape
    nc = seq // SSD_CHUNK
    return pl.pallas_call(
        _ssd_prompt_kernel,
        grid=(nb, nc),
        in_specs=[pl.BlockSpec((1, SSD_CHUNK, D_SSD), lambda b, c: (b, c, 0)),
                  pl.BlockSpec((1, SSD_CHUNK, D_SSD), lambda b, c: (b, c, 1)),
                  pl.BlockSpec((1, SSD_CHUNK, BC_DIM), lambda b, c: (b, c, 0)),
                  pl.BlockSpec((1, SSD_CHUNK, DT_PAD), lambda b, c: (b, c, BC_DIM // DT_PAD)),
                  _const_spec(cw.shape), _const_spec(cb.shape), _const_spec(dtb.shape), _const_spec(alog.shape),
                  _const_spec(dskip.shape), _const_spec(gnorm.shape), _const_spec(tri.shape),
                  _const_spec(sel.shape), _const_spec(emat.shape)],
        out_specs=[pl.BlockSpec((1, SSD_CHUNK, D_SSD), lambda b, c: (b, c, 0)),
                   pl.BlockSpec((1, D_SSD, SSD_STATE), lambda b, c: (b, 0, 0))],
        out_shape=[jax.ShapeDtypeStruct((nb, seq, D_SSD), BF16),
                   jax.ShapeDtypeStruct((nb, D_SSD, SSD_STATE), F32)],
        scratch_shapes=[pltpu.VMEM((SSD_STATE, D_SSD), F32),
                        pltpu.VMEM((8, D_SSD), F32),
                        pltpu.VMEM((8, BC_DIM), F32)],
        compiler_params=pltpu.CompilerParams(dimension_semantics=("arbitrary", "arbitrary"),
                                             vmem_limit_bytes=VMEM_LIMIT),
        name="ssd_prompt",
    )(zx3, zx3, tail3, tail3, cw, cb, dtb, alog, dskip, gnorm, tri, sel, emat)


def _ssd_sample_kernel(z_ref, xs_ref, bc_ref, dt_ref, ctx_ref, h0_ref, cw_ref, cb_ref, dtb_ref, alog_ref, dskip_ref,
                       gnorm_ref, tri_ref, sel_ref, emat_ref, y_ref, st_ref,
                       ybase, yoff, xs_s, xdtd_s, acumx_s, bm_s, cm_s):
    s_id = pl.program_id(1)
    rows = SSD_CHUNK
    nseq = rows // SAMPLE_ROWS
    rr = lax.broadcasted_iota(jnp.int32, (rows, 1), 0)

    @pl.when(s_id == 0)
    def _():
        ctx = ctx_ref[0]
        is_ctx = (rr % SAMPLE_ROWS) >= SAMPLE_ROWS - (SSD_CONV - 1)
        xr = jnp.where(is_ctx, ctx[8:, :D_SSD], xs_ref[...])
        br = jnp.where(is_ctx, ctx[8:, D_SSD:], bc_ref[...])
        cw = cw_ref[...]
        cb = cb_ref[...]
        xs = _silu(_conv_rows(xr, ctx[0:8, :D_SSD], cw[:, :D_SSD], cb[:, :D_SSD]))
        bc = _silu(_conv_rows(br, ctx[0:8, D_SSD:], cw[:, D_SSD:], cb[:, D_SSD:]))
        r = lax.broadcasted_iota(jnp.int32, (rows, rows), 0)
        c = lax.broadcasted_iota(jnp.int32, (rows, rows), 1)
        causal = (c <= r) & ((c // SAMPLE_ROWS) == (r // SAMPLE_ROWS))
        row_valid = (rr % SAMPLE_ROWS) < (SAMPLE_ROWS // 2)
        y_diag, xdtd_bf, acum_x, bm, cm = _ssd_tile(xs, bc, dt_ref[...], row_valid, causal, tri_ref[...],
                                                     sel_ref[...], emat_ref[...], dtb_ref[...], alog_ref[...])
        ybase[...] = y_diag
        yoff[...] = jnp.zeros_like(yoff)
        xs_s[...] = xs
        xdtd_s[...] = xdtd_bf
        acumx_s[...] = acum_x
        bm_s[...] = bm
        cm_s[...] = cm

    in_seq_rows = (rr // SAMPLE_ROWS) == s_id
    in_seq_lanes = (lax.broadcasted_iota(jnp.int32, (1, rows), 1) // SAMPLE_ROWS) == s_id
    h0 = h0_ref[0]
    ht = jnp.concatenate([h0[k * LANES:(k + 1) * LANES, :].T for k in range(D_SSD // LANES)], axis=1)
    half = D_SSD // SSD_GROUPS
    y_off, states = [], []
    for g in range(SSD_GROUPS):
        bg_t = jnp.where(in_seq_lanes, bm_s[:, g * SSD_STATE:(g + 1) * SSD_STATE].T, 0.0).astype(BF16)
        cg = cm_s[:, g * SSD_STATE:(g + 1) * SSD_STATE].astype(BF16)
        y_off.append(_dot(cg, ht[:, g * half:(g + 1) * half].astype(BF16)))
        states.append(_dot(bg_t, xdtd_s[:, g * half:(g + 1) * half]))
    yoff[...] = yoff[...] + jnp.where(in_seq_rows, jnp.concatenate(y_off, axis=1), 0.0)
    last = pl.multiple_of(s_id * SAMPLE_ROWS, SAMPLE_ROWS) + (SAMPLE_ROWS - 1)
    ht_new = jnp.exp(acumx_s[pl.ds(last, 1), :]) * ht + jnp.concatenate(states, axis=1)
    for k in range(D_SSD // LANES):
        st_ref[0, k * LANES:(k + 1) * LANES, :] = ht_new[:, k * LANES:(k + 1) * LANES].T

    @pl.when(s_id == nseq - 1)
    def _():
        y = ybase[...] + yoff[...] * jnp.exp(acumx_s[...])
        y_ref[...] = _ssd_finish(y, xs_s[...], z_ref[...], dskip_ref[...], gnorm_ref[...]).astype(BF16)


def _ssd_sample(zx, tail, ctx, h0, cw, cb, dtb, alog, dskip, gnorm, tri, sel, emat):
    m = zx.shape[0]
    nt = m // SSD_CHUNK
    nseq = SSD_CHUNK // SAMPLE_ROWS
    return pl.pallas_call(
        _ssd_sample_kernel,
        grid=(nt, nseq),
        in_specs=[pl.BlockSpec((SSD_CHUNK, D_SSD), lambda t, s: (t, 0)),
                  pl.BlockSpec((SSD_CHUNK, D_SSD), lambda t, s: (t, 1)),
                  pl.BlockSpec((SSD_CHUNK, BC_DIM), lambda t, s: (t, 0)),
                  pl.BlockSpec((SSD_CHUNK, DT_PAD), lambda t, s: (t, BC_DIM // DT_PAD)),
                  pl.BlockSpec((1, SSD_CHUNK + 8, D_SSD + BC_DIM), lambda t, s: (t, 0, 0)),
                  pl.BlockSpec((1, D_SSD, SSD_STATE), lambda t, s: (t * nseq + s, 0, 0)),
                  _const_spec(cw.shape), _const_spec(cb.shape), _const_spec(dtb.shape), _const_spec(alog.shape),
                  _const_spec(dskip.shape), _const_spec(gnorm.shape), _const_spec(tri.shape),
                  _const_spec(sel.shape), _const_spec(emat.shape)],
        out_specs=[pl.BlockSpec((SSD_CHUNK, D_SSD), lambda t, s: (t, 0)),
                   pl.BlockSpec((1, D_SSD, SSD_STATE), lambda t, s: (t * nseq + s, 0, 0))],
        out_shape=[jax.ShapeDtypeStruct((m, D_SSD), BF16),
                   jax.ShapeDtypeStruct(h0.shape, F32)],
        scratch_shapes=[pltpu.VMEM((SSD_CHUNK, D_SSD), F32),
                        pltpu.VMEM((SSD_CHUNK, D_SSD), F32),
                        pltpu.VMEM((SSD_CHUNK, D_SSD), F32),
                        pltpu.VMEM((SSD_CHUNK, D_SSD), BF16),
                        pltpu.VMEM((SSD_CHUNK, D_SSD), F32),
                        pltpu.VMEM((SSD_CHUNK, SSD_GROUPS * SSD_STATE), F32),
                        pltpu.VMEM((SSD_CHUNK, SSD_GROUPS * SSD_STATE), F32)],
        compiler_params=pltpu.CompilerParams(dimension_semantics=("arbitrary", "arbitrary"),
                                             vmem_limit_bytes=VMEM_LIMIT),
        name="ssd_sample",
    )(zx, zx, tail, tail, ctx, h0, cw, cb, dtb, alog, dskip, gnorm, tri, sel, emat)


def _sb_block(qb, kb, vb, bias, uo, run, mask, nearest_first):
    z = _dot_nt(qb, kb) * ATT_SCALE + bias
    nz = -z
    lg = jnp.log(1.0 + jnp.exp(jnp.minimum(z, nz)))
    ls_pos = jnp.minimum(z, 0.0) - lg
    lm = jnp.minimum(nz, 0.0) - lg
    if mask is not None:
        lm = jnp.where(mask, lm, 0.0)
    hi = lm.astype(BF16)
    lo = (lm - hi.astype(F32)).astype(BF16)
    nsub = z.shape[1] // LANES
    tails = [None] * nsub
    for c in (range(nsub) if nearest_first else reversed(range(nsub))):
        cols = slice(c * LANES, (c + 1) * LANES)
        st = _dot(jnp.concatenate([hi[:, cols], lo[:, cols]], axis=1), uo)
        tails[c] = run + st[:, :LANES]
        run = run + st[:, LANES:]
    w = jnp.exp(ls_pos + jnp.concatenate(tails, axis=1))
    if mask is not None:
        w = jnp.where(mask, w, 0.0)
    return _dot(w.astype(BF16), vb), run


ATT_TQ = 512
ATT_TK = ATT_TQ


def _attn_prompt_kernel(bias_ref, q_ref, k_ref, v_ref, uo_ref, o_ref):
    h = pl.program_id(1)
    qi = pl.program_id(2)
    qb = q_ref[0]
    kbf = k_ref.at[0]
    vbf = v_ref.at[0]
    bias = bias_ref[h]
    uo = uo_ref[...]
    q0 = pl.multiple_of(qi * ATT_TQ, ATT_TQ)

    def diagonal():
        accs, runs = [], []
        for r in range(ATT_TQ // LANES):
            nk = (r + 1) * LANES
            row = lax.broadcasted_iota(jnp.int32, (LANES, nk), 0) + r * LANES
            col = lax.broadcasted_iota(jnp.int32, (LANES, nk), 1)
            a, rn = _sb_block(qb[r * LANES:(r + 1) * LANES], kbf[pl.ds(q0, nk), :], vbf[pl.ds(q0, nk), :], bias, uo,
                              jnp.zeros((LANES, LANES), F32), col < row, False)
            accs.append(a)
            runs.append(rn)
        return jnp.concatenate(accs, axis=0), jnp.concatenate(runs, axis=0)

    def block(kb, acc, run):
        start = pl.multiple_of(kb * ATT_TK, ATT_TK)
        pv, run = _sb_block(qb, kbf[pl.ds(start, ATT_TK), :], vbf[pl.ds(start, ATT_TK), :], bias, uo, run, None, False)
        return acc + pv, run

    @pl.when(qi == 0)
    def _():
        acc, _ = diagonal()
        o_ref[0] = acc.astype(BF16)

    @pl.when(qi > 0)
    def _():
        acc, run = block(qi - 1, *diagonal())
        acc, _ = lax.fori_loop(0, qi - 1, lambda i, carry: block(qi - 2 - i, *carry), (acc, run))
        o_ref[0] = acc.astype(BF16)


def _attn_prompt(qkv3, sb_bias, uo):
    nb, seq, _ = qkv3.shape
    tq = ATT_TQ
    return pl.pallas_call(
        _attn_prompt_kernel,
        grid=(nb, ATT_HEADS, seq // tq),
        in_specs=[pl.BlockSpec(memory_space=pltpu.SMEM),
                  pl.BlockSpec((1, tq, ATT_HEAD_DIM), lambda b, h, i: (b, i, h)),
                  pl.BlockSpec((1, seq, ATT_HEAD_DIM), lambda b, h, i: (b, 0, ATT_HEADS + h)),
                  pl.BlockSpec((1, seq, ATT_HEAD_DIM), lambda b, h, i: (b, 0, 2 * ATT_HEADS + h)),
                  _const_spec(uo.shape)],
        out_specs=pl.BlockSpec((1, tq, ATT_HEAD_DIM), lambda b, h, i: (b, i, h)),
        out_shape=jax.ShapeDtypeStruct((nb, seq, D_ATT), BF16),
        compiler_params=pltpu.CompilerParams(dimension_semantics=("arbitrary", "arbitrary", "arbitrary"),
                                             vmem_limit_bytes=VMEM_LIMIT),
        name="attn_prompt",
    )(sb_bias, qkv3, qkv3, qkv3, uo)


PAGES_PER_STEP = 8
PAGE_SLOTS = 3


def _page_by_head(ref):
    heads = [ref[pl.ds(h, PAGE_SIZE, stride=ATT_HEADS), :] for h in range(ATT_HEADS)]
    return jnp.concatenate(heads, axis=1).astype(BF16)


def _attn_sample_kernel(nseq, steps, pt_ref, q_ref, kn_ref, vn_ref, bias_ref, uo_ref, ck_hbm, cv_hbm, o_ref,
                        qbd, knew, vnew, acc, run, kbuf, vbuf, sems):
    b = pl.program_id(0)
    j = pl.program_id(1)
    total = nseq * steps
    t = b * steps + j
    nrow = ATT_HEADS * SAMPLE_ROWS
    bias = bias_ref[...]
    uo = uo_ref[...]

    def page_copies(step):
        sb = step // steps
        sj = step - sb * steps
        slot = step % PAGE_SLOTS
        copies = []
        for g in range(PAGES_PER_STEP):
            page = pt_ref[sb, steps * PAGES_PER_STEP - 1 - (sj * PAGES_PER_STEP + g)]
            copies.append(pltpu.make_async_copy(ck_hbm.at[page], kbuf.at[slot, g], sems.at[0, slot]))
            copies.append(pltpu.make_async_copy(cv_hbm.at[page], vbuf.at[slot, g], sems.at[1, slot]))
        return copies

    @pl.when(t == 0)
    def _():
        for step in range(PAGE_SLOTS - 1):
            for c in page_copies(step):
                c.start()

    @pl.when(t + (PAGE_SLOTS - 1) < total)
    def _():
        for c in page_copies(t + (PAGE_SLOTS - 1)):
            c.start()

    @pl.when((b == 0) & (j == 0))
    def _():
        knew[...] = jnp.zeros_like(knew)
        vnew[...] = jnp.zeros_like(vnew)

    @pl.when(j == 0)
    def _():
        q8 = q_ref[0]
        rr = lax.broadcasted_iota(jnp.int32, (nrow, D_ATT), 0)
        cc = lax.broadcasted_iota(jnp.int32, (nrow, D_ATT), 1)
        qt = jnp.concatenate([q8] * ATT_HEADS, axis=0)
        qbd[...] = jnp.where((rr // SAMPLE_ROWS) == (cc // ATT_HEAD_DIM), qt, 0.0).astype(BF16)
        knew[0:SAMPLE_ROWS, :] = kn_ref[0]
        vnew[0:SAMPLE_ROWS, :] = vn_ref[0]
        qq = lax.broadcasted_iota(jnp.int32, (nrow, LANES), 0) % SAMPLE_ROWS
        kk = lax.broadcasted_iota(jnp.int32, (nrow, LANES), 1)
        mask = (kk < qq) & (kk < SAMPLE_ROWS // 2)
        pv, r = _sb_block(qbd[...], knew[...].astype(BF16), vnew[...].astype(BF16), bias, uo,
                          jnp.zeros((nrow, LANES), F32), mask, True)
        acc[...] = pv
        run[...] = r

    for c in page_copies(t):
        c.wait()
    slot = t % PAGE_SLOTS
    kcat = jnp.concatenate([_page_by_head(kbuf.at[slot, g]) for g in range(PAGES_PER_STEP)], axis=0)
    vcat = jnp.concatenate([_page_by_head(vbuf.at[slot, g]) for g in range(PAGES_PER_STEP)], axis=0)
    pv, r = _sb_block(qbd[...], kcat, vcat, jnp.concatenate([bias] * PAGES_PER_STEP, axis=1), uo, run[...], None, True)
    a = acc[...] + pv
    acc[...] = a
    run[...] = r

    @pl.when(j == steps - 1)
    def _():
        for h in range(ATT_HEADS):
            o_ref[0, :, h * ATT_HEAD_DIM:(h + 1) * ATT_HEAD_DIM] = (
                a[h * SAMPLE_ROWS:(h + 1) * SAMPLE_ROWS, h * ATT_HEAD_DIM:(h + 1) * ATT_HEAD_DIM])


def _attn_sample(page_table, projs3, cache_k, cache_v, bias_rows, uo):
    nseq, npages = page_table.shape
    steps = npages // PAGES_PER_STEP
    assert steps * PAGES_PER_STEP == npages and nseq * steps >= PAGE_SLOTS - 1
    page_buf = pltpu.VMEM((PAGE_SLOTS, PAGES_PER_STEP, PAGE_SIZE * ATT_HEADS, ATT_HEAD_DIM), F32)

    grid_spec = pltpu.PrefetchScalarGridSpec(
        num_scalar_prefetch=1,
        grid=(nseq, steps),
        in_specs=[pl.BlockSpec((1, SAMPLE_ROWS, D_ATT), lambda b, j, pt: (b, 0, 0)),
                  pl.BlockSpec((1, SAMPLE_ROWS, D_ATT), lambda b, j, pt: (b, 0, 1)),
                  pl.BlockSpec((1, SAMPLE_ROWS, D_ATT), lambda b, j, pt: (b, 0, 2)),
                  pl.BlockSpec(bias_rows.shape, lambda b, j, pt: (0, 0)),
                  pl.BlockSpec(uo.shape, lambda b, j, pt: (0, 0)),
                  pl.BlockSpec(memory_space=pl.ANY), pl.BlockSpec(memory_space=pl.ANY)],
        out_specs=pl.BlockSpec((1, SAMPLE_ROWS, D_ATT), lambda b, j, pt: (b, 0, 0)),
        scratch_shapes=[pltpu.VMEM((ATT_HEADS * SAMPLE_ROWS, D_ATT), BF16),
                        pltpu.VMEM((PAGE_SIZE, D_ATT), F32),
                        pltpu.VMEM((PAGE_SIZE, D_ATT), F32),
                        pltpu.VMEM((ATT_HEADS * SAMPLE_ROWS, D_ATT), F32),
                        pltpu.VMEM((ATT_HEADS * SAMPLE_ROWS, LANES), F32),
                        page_buf, page_buf, pltpu.SemaphoreType.DMA((2, PAGE_SLOTS))])
    return pl.pallas_call(
        functools.partial(_attn_sample_kernel, nseq, steps),
        grid_spec=grid_spec,
        out_shape=jax.ShapeDtypeStruct((nseq, SAMPLE_ROWS, D_ATT), F32),
        compiler_params=pltpu.CompilerParams(dimension_semantics=("arbitrary", "arbitrary"),
                                             vmem_limit_bytes=VMEM_LIMIT),
        name="attn_sample",
    )(page_table, projs3, projs3, projs3, bias_rows, uo, cache_k, cache_v)


def _outproj_kernel(y_ref, o_ref, x_ref, g1_ref, sh2_ref, sc2_ref, gpost_ref, gpre_ref, wy_ref, wo_ref, x1_ref, h2_ref):
    tm = x_ref.shape[0]
    chunk = min(tm, 256)
    for r in range(0, tm, chunk):
        rows = slice(r, r + chunk)
        g1, sh2, sc2 = (v if v.shape[0] == 1 else v[rows] for v in (g1_ref[0], sh2_ref[0], sc2_ref[0]))
        m = _dot(y_ref[rows, :], wy_ref[...]) + _dot(o_ref[rows, :].astype(BF16), wo_ref[...])
        x1 = x_ref[rows, :] + g1 * _rms(m, gpost_ref[...])
        x1_ref[rows, :] = x1
        h2 = _rms(x1, gpre_ref[...]) * (1.0 + sc2) + sh2
        h2_ref[rows, :] = h2.astype(BF16)


def _outproj(y, o, x, mods, gpost, gpre, w_out, tm, tiles_per_group):
    m = x.shape[0]
    rows = mods.shape[1]
    half = w_out.shape[0] // 2
    row = lambda i: (i, 0)
    return pl.pallas_call(
        _outproj_kernel,
        grid=(m // tm,),
        in_specs=[pl.BlockSpec((tm, D_SSD), row), pl.BlockSpec((tm, D_ATT), row), pl.BlockSpec((tm, D_MODEL), row),
                  _mod_spec(2, rows, tiles_per_group), _mod_spec(3, rows, tiles_per_group),
                  _mod_spec(4, rows, tiles_per_group),
                  _const_spec(gpost.shape), _const_spec(gpre.shape),
                  pl.BlockSpec((half, D_MODEL), lambda i: (0, 0)),
                  pl.BlockSpec((half, D_MODEL), lambda i: (1, 0))],
        out_specs=[pl.BlockSpec((tm, D_MODEL), row), pl.BlockSpec((tm, D_MODEL), row)],
        out_shape=[jax.ShapeDtypeStruct((m, D_MODEL), F32), jax.ShapeDtypeStruct((m, D_MODEL), BF16)],
        compiler_params=pltpu.CompilerParams(dimension_semantics=("arbitrary",), vmem_limit_bytes=VMEM_LIMIT),
        name="out_proj",
    )(y, o, x, mods, mods, mods, gpost, gpre, w_out, w_out)


FFN_CHUNK = 256


def _ffn_kernel(sample, tiles_per_group, *refs):
    if sample:
        (h_ref, x1_hbm, g2_ref, gpost_ref, wg_ref, wv_ref, wo_ref, cwg_ref, cwv_ref, cbg_ref, cbv_ref,
         injg_ref, injv_ref, y_ref, ug_ref, uv_ref, x1_buf, x1_sem) = refs
    else:
        (h_ref, x1_hbm, g2_ref, gpost_ref, wg_ref, wv_ref, wo_ref, cwg_ref, cwv_ref, cbg_ref, cbv_ref,
         y_ref, ug_ref, uv_ref, x1_buf, x1_sem, halog, halov) = refs
    i = pl.program_id(0)
    j = pl.program_id(1)
    tm = h_ref.shape[0]
    tf = wg_ref.shape[1]
    hn = h_ref[...]
    x1_copy = pltpu.make_async_copy(x1_hbm.at[pl.ds(pl.multiple_of(i * tm, tm), tm), :], x1_buf, x1_sem)

    @pl.when(j == 0)
    def _():
        x1_copy.start()
        y_ref[...] = jnp.zeros_like(y_ref)

    if sample:
        is_ctx = (lax.broadcasted_iota(jnp.int32, (tm, 1), 0) % SAMPLE_ROWS) >= SAMPLE_ROWS - (FFN_CONV - 1)
    else:
        @pl.when((i == 0) & (j == 0))
        def _():
            halog[...] = jnp.zeros_like(halog)
            halov[...] = jnp.zeros_like(halov)

        first = lax.rem(i, tiles_per_group) == 0

    def half(w_ref, cw_ref, cb_ref, u_ref, inj_ref, halo, cols):
        u = jnp.concatenate([_dot(hn[r:r + 128], w_ref[:, cols]) for r in range(0, tm, 128)], axis=0)
        if sample:
            u_ref[:, cols] = u
            prev = inj_ref[0:8, cols]
            u = jnp.where(is_ctx, inj_ref[8:, cols], u)
        else:
            prev = jnp.where(first, 0.0, halo[j, :, cols])
            halo[j, :, cols] = u[tm - 8:, :]
            u_ref[0, :, cols] = u[tm - 8:, :]
        return _conv_rows(u, prev, cw_ref[:, cols], cb_ref[:, cols])

    acts = []
    for c in range(tf // FFN_CHUNK):
        cols = slice(c * FFN_CHUNK, (c + 1) * FFN_CHUNK)
        gate = half(wg_ref, cwg_ref, cbg_ref, ug_ref, None if not sample else injg_ref, None if sample else halog, cols)
        val = half(wv_ref, cwv_ref, cbv_ref, uv_ref, None if not sample else injv_ref, None if sample else halov, cols)
        acts.append((_silu(gate) * val).astype(BF16))
    y_ref[...] += _dot(jnp.concatenate(acts, axis=1), wo_ref[...])

    @pl.when(j == pl.num_programs(1) - 1)
    def _():
        x1_copy.wait()
        chunk = min(tm, 256)
        for r in range(0, tm, chunk):
            rows = slice(r, r + chunk)
            g2 = g2_ref[0]
            g2 = g2 if g2.shape[0] == 1 else g2[rows]
            y_ref[rows, :] = x1_buf[rows, :] + g2 * _rms(y_ref[rows, :], gpost_ref[...])


def _ffn(h2, x1, mods, gpost, w_in, w_out, cw, cb, tm, tiles_per_group, inj=None):
    m = h2.shape[0]
    rows = mods.shape[1]
    tf = 512
    nf = D_FF // tf
    sample = inj is not None
    row = lambda i, j: (i, 0)
    gate_col = lambda i, j: (0, j)
    val_col = lambda i, j: (0, nf + j)
    in_specs = [pl.BlockSpec((tm, D_MODEL), row), pl.BlockSpec(memory_space=pl.ANY),
                _mod_spec(5, rows, tiles_per_group), _const_spec(gpost.shape),
                pl.BlockSpec((D_MODEL, tf), gate_col), pl.BlockSpec((D_MODEL, tf), val_col),
                pl.BlockSpec((tf, D_MODEL), lambda i, j: (j, 0)),
                pl.BlockSpec((FFN_CONV, tf), gate_col), pl.BlockSpec((FFN_CONV, tf), val_col),
                pl.BlockSpec((1, tf), gate_col), pl.BlockSpec((1, tf), val_col)]
    args = [h2, x1, mods, gpost, w_in, w_in, w_out, cw, cw, cb, cb]
    scratch = [pltpu.VMEM((tm, D_MODEL), F32), pltpu.SemaphoreType.DMA(())]
    if sample:
        in_specs += [pl.BlockSpec((m + 8, tf), gate_col), pl.BlockSpec((m + 8, tf), val_col)]
        args += [inj, inj]
        u_specs = [pl.BlockSpec((tm, tf), lambda i, j: (i, j))] * 2
        u_shapes = [jax.ShapeDtypeStruct((m, D_FF), F32)] * 2
    else:
        u_specs = [pl.BlockSpec((1, 8, tf), lambda i, j: (i, 0, j))] * 2
        u_shapes = [jax.ShapeDtypeStruct((m // tm, 8, D_FF), F32)] * 2
        scratch += [pltpu.VMEM((nf, 8, tf), F32), pltpu.VMEM((nf, 8, tf), F32)]
    return pl.pallas_call(
        functools.partial(_ffn_kernel, sample, tiles_per_group),
        grid=(m // tm, nf),
        in_specs=in_specs,
        out_specs=[pl.BlockSpec((tm, D_MODEL), row)] + u_specs,
        out_shape=[jax.ShapeDtypeStruct((m, D_MODEL), F32)] + u_shapes,
        scratch_shapes=scratch,
        compiler_params=pltpu.CompilerParams(dimension_semantics=("arbitrary", "arbitrary"),
                                             vmem_limit_bytes=VMEM_LIMIT),
        name="ffn_sample" if sample else "ffn_prompt",
    )(*args)


def _ssd_constants(block_rows):
    r = jnp.arange(SSD_CHUNK)[:, None]
    c = jnp.arange(SSD_CHUNK)[None, :]
    same = (r // block_rows) == (c // block_rows)
    tri = ((c <= r) & same).astype(BF16)
    sel = (c == (r // block_rows) * block_rows + block_rows - 1).astype(BF16)
    hrow = jnp.arange(LANES)[:, None]
    e1 = (hrow == jnp.arange(D_SSD)[None, :] // SSD_HEAD_DIM).astype(BF16)
    e2 = (hrow == jnp.arange(SSD_HEADS * LANES)[None, :] // LANES).astype(BF16)
    return tri, sel, jnp.concatenate([e1, e2], axis=1)


def _suffix_matrix():
    j = jnp.arange(LANES)[:, None]
    s = jnp.arange(LANES)[None, :]
    u = jnp.concatenate([(j > s).astype(BF16), jnp.ones((LANES, LANES), BF16)], axis=1)
    return jnp.concatenate([u, u], axis=0)


def _inject_rows(ctx, nctx):
    nseq, _, c = ctx.shape
    body = jnp.concatenate([ctx[1:], jnp.zeros((1, nctx, c), F32)], axis=0)
    body = jnp.pad(body, ((0, 0), (SAMPLE_ROWS - nctx, 0), (0, 0))).reshape(nseq * SAMPLE_ROWS, c)
    head = jnp.pad(ctx[0], ((8 - nctx, 0), (0, 0)))
    return jnp.concatenate([head, body], axis=0)


def kernel(x_prompt, x_sample, c_prompt, c_sample, cache_k, cache_v, page_table, state_ssm, state_conv, state_ffn_conv, w_ada, b_ada, g_pre_mix, g_post_mix, g_pre_ffn, g_post_ffn, w_in, conv_w, conv_b, dt_bias, a_log, d_skip, g_ssd_norm, sb_bias, w_out, w_ffn_in, ffn_conv_w, ffn_conv_b, w_ffn_out):
    assert w_ada.shape[0] == 1, "one layer"
    nb, seq, d = x_prompt.shape
    ns, ls, _ = x_sample.shape
    n_pool = cache_k.shape[1]
    mp = nb * seq
    ms = ns * SAMPLE_ROWS

    w = w_in[0].astype(BF16)
    o_bc, o_dt, o_q = 2 * D_SSD, 2 * D_SSD + BC_DIM, 2 * D_SSD + BC_DIM + SSD_HEADS
    w_main = jnp.concatenate([w[:, :o_bc], w[:, o_q:]], axis=1)
    w_tail = jnp.concatenate([w[:, o_bc:o_q], jnp.zeros((d, DT_PAD - SSD_HEADS), BF16)], axis=1)
    w_out_b = w_out[0].astype(BF16)
    w_ffn_in_b = w_ffn_in[0].astype(BF16)
    w_ffn_out_b = w_ffn_out[0].astype(BF16)
    row = lambda v: v.reshape(1, -1)
    dtb = jnp.pad(dt_bias[0], (0, DT_PAD - SSD_HEADS)).reshape(1, DT_PAD)
    alog = jnp.pad(a_log[0], (0, DT_PAD - SSD_HEADS)).reshape(1, DT_PAD)
    dskip = jnp.repeat(d_skip[0], SSD_HEAD_DIM).reshape(1, D_SSD)
    gnorm = row(g_ssd_norm[0])
    cw, cb = conv_w[0], row(conv_b[0])
    fcw, fcb = ffn_conv_w[0], row(ffn_conv_b[0])
    uo = _suffix_matrix()

    n_c = nb + ns
    c_all = jnp.concatenate([c_prompt, c_sample, jnp.zeros((-n_c % 16, d), F32)], axis=0)
    mod = _ada(c_all, w_ada[0], row(b_ada[0]))
    mods_p = mod[:nb].reshape(nb, 1, 6 * d)
    mods_s = jnp.repeat(mod[nb:n_c], SAMPLE_ROWS, axis=0).reshape(1, ms, 6 * d)

    tm_p = 512
    tpg_p = seq // tm_p
    xp = x_prompt.reshape(mp, d)
    hn_p, tail_p = _prenorm(xp, mods_p, row(g_pre_mix[0]), w_tail, tm_p, tpg_p)
    zx_p, qkv_p, k_p, v_p = _inproj(hn_p, w_main, 1024, BF16)
    zx3 = zx_p.reshape(nb, seq, 2 * D_SSD)
    tail3 = tail_p.reshape(nb, seq, PROJ_TAIL)
    tri, sel, emat = _ssd_constants(SSD_CHUNK)
    y_ssd, ssm_p = _ssd_prompt(zx3, tail3, cw, cb, dtb, alog, dskip, gnorm, tri, sel, emat)
    o_att = _attn_prompt(qkv_p.reshape(nb, seq, 3 * D_ATT), sb_bias[0], uo)
    tm_e = 512
    x1_p, h2_p = _outproj(y_ssd.reshape(mp, D_SSD), o_att.reshape(mp, D_ATT), xp, mods_p, row(g_post_mix[0]),
                          row(g_pre_ffn[0]), w_out_b, tm_e, seq // tm_e)
    tm_f = 1024
    yp, ctxg_p, ctxv_p = _ffn(h2_p, x1_p, mods_p, row(g_post_ffn[0]), w_ffn_in_b, w_ffn_out_b, fcw, fcb,
                              tm_f, seq // tm_f)

    xs_pad = jnp.pad(x_sample, ((0, 0), (0, SAMPLE_ROWS - ls), (0, 0))).reshape(ms, d)
    hn_s, tail_s = _prenorm(xs_pad, mods_s, row(g_pre_mix[0]), w_tail, ms, 1)
    zx_s, qkv_s, k_s, v_s = _inproj(hn_s, w_main, ms, F32)
    seq_per_tile = SSD_CHUNK // SAMPLE_ROWS
    conv_ctx = state_conv[0].reshape(ns // seq_per_tile, seq_per_tile, SSD_CONV - 1, -1)
    ctx_inj = jax.vmap(lambda c: _inject_rows(c, SSD_CONV - 1))(conv_ctx)
    tri_s, sel_s, _ = _ssd_constants(SAMPLE_ROWS)
    y_ssd_s, ssm_s = _ssd_sample(zx_s, tail_s, ctx_inj, state_ssm[0].reshape(ns, D_SSD, SSD_STATE), cw, cb, dtb, alog,
                                 dskip, gnorm, tri_s, sel_s, emat)
    bias_rows = jnp.broadcast_to(jnp.repeat(sb_bias[0], SAMPLE_ROWS)[:, None], (ATT_HEADS * SAMPLE_ROWS, LANES))
    o_att_s = _attn_sample(page_table, qkv_s.reshape(ns, SAMPLE_ROWS, 3 * D_ATT),
                           cache_k[0].reshape(n_pool, PAGE_SIZE * ATT_HEADS, ATT_HEAD_DIM),
                           cache_v[0].reshape(n_pool, PAGE_SIZE * ATT_HEADS, ATT_HEAD_DIM), bias_rows, uo)
    x1_s, h2_s = _outproj(y_ssd_s, o_att_s.reshape(ms, D_ATT), xs_pad, mods_s, row(g_post_mix[0]),
                          row(g_pre_ffn[0]), w_out_b, ms, 1)
    inj_f = _inject_rows(state_ffn_conv[0], FFN_CONV - 1)
    ys_pad, ug_s, uv_s = _ffn(h2_s, x1_s, mods_s, row(g_post_ffn[0]), w_ffn_in_b, w_ffn_out_b, fcw, fcb,
                              ms, 1, inj=inj_f)

    heads = (ATT_HEADS, ATT_HEAD_DIM)
    k_p = k_p.reshape(1, nb, seq, *heads)
    v_p = v_p.reshape(1, nb, seq, *heads)
    tail = slice(seq - (SSD_CONV - 1), seq)
    conv_p = jnp.concatenate([zx3[:, tail, D_SSD:], tail3[:, tail, :BC_DIM]], axis=-1)[None]
    ffn_p = jnp.concatenate([ctxg_p, ctxv_p], axis=-1)[seq // tm_f - 1::seq // tm_f, 8 - (FFN_CONV - 1):][None]
    k_s = k_s.reshape(ns, SAMPLE_ROWS, *heads)[None, :, :ls]
    v_s = v_s.reshape(ns, SAMPLE_ROWS, *heads)[None, :, :ls]
    raw_s = jnp.concatenate([zx_s.reshape(ns, SAMPLE_ROWS, -1)[:, :ls, D_SSD:],
                             tail_s.reshape(ns, SAMPLE_ROWS, -1)[:, :ls, :BC_DIM]], axis=-1)
    conv_s = jnp.concatenate([state_conv[0], raw_s], axis=1)[:, -(SSD_CONV - 1):][None]
    u_s = jnp.concatenate([ug_s, uv_s], axis=-1).reshape(ns, SAMPLE_ROWS, 2 * D_FF)[:, :ls]
    ffn_s = jnp.concatenate([state_ffn_conv[0], u_s], axis=1)[:, -(FFN_CONV - 1):][None]
    return (yp.reshape(nb, seq, d), ys_pad.reshape(ns, SAMPLE_ROWS, d)[:, :ls],
            k_p, v_p, ssm_p.reshape(1, nb, SSD_HEADS, SSD_HEAD_DIM, SSD_STATE), conv_p, ffn_p,
            k_s, v_s, ssm_s.reshape(1, ns, SSD_HEADS, SSD_HEAD_DIM, SSD_STATE), conv_s, ffn_s)
```

```python
import functools

import jax
import jax.numpy as jnp
from jax import lax
from jax.experimental import pallas as pl
from jax.experimental.pallas import tpu as pltpu

F32 = jnp.float32
BF16 = jnp.bfloat16

D_MODEL = 2048
D_SSD = 1024
SSD_HEAD_DIM = 64
SSD_HEADS = 16
SSD_GROUPS = 2
SSD_STATE = 128
SSD_CONV = 4
SSD_CHUNK = 128
SSD_STEP_CHUNKS = 4
BC_DIM = 2 * SSD_GROUPS * SSD_STATE
D_ATT = 1024
ATT_HEAD_DIM = 128
ATT_HEADS = 8
D_FF = 5632
FFN_CONV = 3
PAGE_SIZE = 128
EPS = 1e-6
ATT_SCALE = ATT_HEAD_DIM ** -0.5

PROJ_TN = 1024
PROJ_MAIN = 5 * PROJ_TN
DT_PAD = 128
PROJ_TAIL = BC_DIM + DT_PAD

SAMPLE_ROWS = 8
LANES = 128
VMEM_LIMIT = 56 * 1024 * 1024


def _dot(a, b):
    return jnp.dot(a, b, preferred_element_type=F32)


def _dot_nt(a, b):
    return lax.dot_general(a, b, (((1,), (1,)), ((), ())), preferred_element_type=F32)


def _sigmoid(x):
    return 1.0 / (1.0 + jnp.exp(-x))


def _silu(x):
    return x * _sigmoid(x)


def _softplus(x):
    return jnp.maximum(x, 0.0) + jnp.log1p(jnp.exp(-jnp.abs(x)))


def _rms(x, g):
    ms = jnp.mean(x * x, axis=-1, keepdims=True)
    return x * lax.rsqrt(ms + EPS) * g


def _split3(x):
    h = x.astype(BF16)
    r = x - h.astype(F32)
    m = r.astype(BF16)
    l = (r - m.astype(F32)).astype(BF16)
    return h, m, l


def _dot3_right(x, mat):
    return _dot(jnp.concatenate(_split3(x), axis=1), jnp.concatenate([mat] * 3, axis=0))


def _dot3_left(mat, x):
    return _dot(jnp.concatenate([mat] * 3, axis=1), jnp.concatenate(_split3(x), axis=0))


def _ada_kernel(c_ref, w_ref, b_ref, o_ref):
    s = _silu(c_ref[...]).astype(BF16)
    o_ref[...] = _dot(s, w_ref[...].astype(BF16)) + b_ref[...]


def _ada(c_all, w_ada, b_ada):
    m, d = c_all.shape
    n = w_ada.shape[1]
    tn = 1024
    return pl.pallas_call(
        _ada_kernel,
        grid=(n // tn,),
        in_specs=[pl.BlockSpec((m, d), lambda j: (0, 0)),
                  pl.BlockSpec((d, tn), lambda j: (0, j)),
                  pl.BlockSpec((1, tn), lambda j: (0, j))],
        out_specs=pl.BlockSpec((m, tn), lambda j: (0, j)),
        out_shape=jax.ShapeDtypeStruct((m, n), F32),
        compiler_params=pltpu.CompilerParams(dimension_semantics=("arbitrary",), vmem_limit_bytes=VMEM_LIMIT),
        name="ada_mod",
    )(c_all, w_ada, b_ada)


def _mod_spec(idx, rows, tiles_per_group):
    return pl.BlockSpec((1, rows, D_MODEL), lambda i, *_: (i // tiles_per_group, 0, idx))


def _prenorm_kernel(x_ref, sh_ref, sc_ref, g_ref, wt_ref, hn_ref, tail_ref):
    tm = x_ref.shape[0]
    chunk = min(tm, 128)
    g = g_ref[...]
    for r in range(0, tm, chunk):
        sc, sh = sc_ref[0], sh_ref[0]
        if sc.shape[0] != 1:
            sc, sh = sc[r:r + chunk], sh[r:r + chunk]
        h = (_rms(x_ref[r:r + chunk, :], g) * (1.0 + sc) + sh).astype(BF16)
        hn_ref[r:r + chunk, :] = h
        tail_ref[r:r + chunk, :] = _dot(h, wt_ref[...])


def _prenorm(x, mods, g, wt, tm, tiles_per_group):
    m = x.shape[0]
    rows = mods.shape[1]
    return pl.pallas_call(
        _prenorm_kernel,
        grid=(m // tm,),
        in_specs=[pl.BlockSpec((tm, D_MODEL), lambda i: (i, 0)),
                  _mod_spec(0, rows, tiles_per_group),
                  _mod_spec(1, rows, tiles_per_group),
                  _const_spec(g.shape), _const_spec(wt.shape)],
        out_specs=[pl.BlockSpec((tm, D_MODEL), lambda i: (i, 0)), pl.BlockSpec((tm, PROJ_TAIL), lambda i: (i, 0))],
        out_shape=[jax.ShapeDtypeStruct((m, D_MODEL), BF16), jax.ShapeDtypeStruct((m, PROJ_TAIL), F32)],
        compiler_params=pltpu.CompilerParams(dimension_semantics=("arbitrary",), vmem_limit_bytes=VMEM_LIMIT),
        name="pre_norm",
    )(x, mods, mods, g, wt)


def _inproj_kernel(hn_ref, wzx_ref, w_ref, zx_ref, qkv_ref, k_ref, v_ref):
    j = pl.program_id(1)
    tm = hn_ref.shape[0]

    def project(state_ref):
        res = _dot(hn_ref[...], w_ref[...])
        qkv_ref[...] = res.astype(qkv_ref.dtype)
        if state_ref is not None:
            for h in range(ATT_HEADS):
                state_ref[pl.ds(h, tm, stride=ATT_HEADS), :] = res[:, h * ATT_HEAD_DIM:(h + 1) * ATT_HEAD_DIM]

    @pl.when(j < 2)
    def _():
        zx_ref[...] = _dot(hn_ref[...], wzx_ref[...])

    @pl.when(j == 2)
    def _():
        project(None)

    @pl.when(j == 3)
    def _():
        project(k_ref)

    @pl.when(j == 4)
    def _():
        project(v_ref)


def _inproj(hn, w_zx, w_qkv, tm, qkv_dtype):
    m = hn.shape[0]
    kv_spec = pl.BlockSpec((tm * ATT_HEADS, ATT_HEAD_DIM), lambda i, j: (i, 0))
    kv_shape = jax.ShapeDtypeStruct((m * ATT_HEADS, ATT_HEAD_DIM), F32)
    return pl.pallas_call(
        _inproj_kernel,
        grid=(m // tm, PROJ_MAIN // PROJ_TN),
        in_specs=[pl.BlockSpec((tm, D_MODEL), lambda i, j: (i, 0)),
                  pl.BlockSpec((D_MODEL, PROJ_TN), lambda i, j: (0, jnp.minimum(j, 1))),
                  pl.BlockSpec((D_MODEL, PROJ_TN), lambda i, j: (0, jnp.clip(j - 2, 0, 2)))],
        out_specs=[pl.BlockSpec((tm, PROJ_TN), lambda i, j: (i, jnp.minimum(j, 1))),
                   pl.BlockSpec((tm, PROJ_TN), lambda i, j: (i, jnp.clip(j - 2, 0, 2))),
                   kv_spec, kv_spec],
        out_shape=[jax.ShapeDtypeStruct((m, 2 * PROJ_TN), F32),
                   jax.ShapeDtypeStruct((m, 3 * PROJ_TN), qkv_dtype), kv_shape, kv_shape],
        compiler_params=pltpu.CompilerParams(dimension_semantics=("arbitrary", "arbitrary"),
                                             vmem_limit_bytes=VMEM_LIMIT),
        name="in_proj",
    )(hn, w_zx, w_qkv)


def _shift_rows(u, prev8, s):
    r = pltpu.roll(u, s, 0)
    head = jnp.where(lax.broadcasted_iota(jnp.int32, (8, 1), 0) < s, pltpu.roll(prev8, s, 0), r[:8])
    return jnp.concatenate([head, r[8:]], axis=0)


def _conv_rows(u, prev8, w, b):
    taps = w.shape[0]
    y = _shift_rows(u, prev8, taps - 1) * w[0:1]
    for t in range(1, taps):
        s = taps - 1 - t
        y = y + (_shift_rows(u, prev8, s) if s else u) * w[t:t + 1]
    return y + b


def _ssd_tile(xs, bc, dtr, row_valid, causal, tri, sel, emat, dtb, alog):
    dt = _softplus(dtr + dtb)
    if row_valid is not None:
        dt = jnp.where(row_valid, dt, 0.0)
    a = -jnp.exp(alog)
    acum = _dot3_left(tri, dt * a)
    alast = _dot3_left(sel, acum)
    acx = _dot3_right(acum, emat)
    acum_x = acx[:, :D_SSD]
    acum_x2 = acx[:, D_SSD:]
    e1 = emat[:, :D_SSD]
    dt_x = _dot3_right(dt, e1)
    alast_x = _dot3_right(alast, e1)
    acum_t = acum.T

    xdt = xs * dt_x
    xdt_bf = xdt.astype(BF16)
    xdtd_bf = (xdt * jnp.exp(alast_x - acum_x)).astype(BF16)
    bm = bc[:, :SSD_GROUPS * SSD_STATE]
    cm = bc[:, SSD_GROUPS * SSD_STATE:]
    lane = lax.broadcasted_iota(jnp.int32, (SSD_CHUNK, LANES), 1)
    first_head = lane < SSD_HEAD_DIM

    pieces = []
    heads_per_group = SSD_HEADS // SSD_GROUPS
    for g in range(SSD_GROUPS):
        bg = bm[:, g * SSD_STATE:(g + 1) * SSD_STATE].astype(BF16)
        cg = cm[:, g * SSD_STATE:(g + 1) * SSD_STATE].astype(BF16)
        cb = _dot_nt(cg, bg)
        for pair in range(heads_per_group // 2):
            h0 = g * heads_per_group + 2 * pair
            ys = []
            for hh in (h0, h0 + 1):
                seg = acum_x2[:, hh * LANES:(hh + 1) * LANES] - acum_t[hh:hh + 1, :]
                decay = jnp.where(causal, jnp.exp(jnp.where(causal, seg, 0.0)), 0.0)
                sc = (cb * decay).astype(BF16)
                ys.append(_dot(sc, xdt_bf[:, (h0 // 2) * LANES:(h0 // 2 + 1) * LANES]))
            pieces.append(jnp.where(first_head, ys[0], ys[1]))
    y_diag = jnp.concatenate(pieces, axis=1)
    return y_diag, xdtd_bf, acum_x, bm, cm


def _ssd_finish(y, xs, z, dskip, gnorm):
    y = y + dskip * xs
    y = y * _silu(z)
    half = D_SSD // SSD_GROUPS
    outs = []
    for g in range(SSD_GROUPS):
        outs.append(_rms(y[:, g * half:(g + 1) * half], gnorm[:, g * half:(g + 1) * half]))
    return jnp.concatenate(outs, axis=1)


def _ssd_prompt_kernel(z_ref, xs_ref, bc_ref, dt_ref, cw_ref, cb_ref, dtb_ref, alog_ref, dskip_ref, gnorm_ref,
                       tri_ref, sel_ref, emat_ref, y_ref, st_ref, ht_ref, prev_xs, prev_bc):
    c = pl.program_id(1)
    rows = SSD_CHUNK

    @pl.when(c == 0)
    def _():
        ht_ref[...] = jnp.zeros_like(ht_ref)
        prev_xs[...] = jnp.zeros_like(prev_xs)
        prev_bc[...] = jnp.zeros_like(prev_bc)

    cw = cw_ref[...]
    cb = cb_ref[...]
    r = lax.broadcasted_iota(jnp.int32, (rows, rows), 0)
    s = lax.broadcasted_iota(jnp.int32, (rows, rows), 1)
    causal = s <= r
    half = D_SSD // SSD_GROUPS
    px = prev_xs[...]
    pb = prev_bc[...]
    ht_new = ht_ref[...]
    for sub in range(xs_ref.shape[1] // rows):
        rs = slice(sub * rows, (sub + 1) * rows)
        xr = xs_ref[0, rs, :]
        br = bc_ref[0, rs, :]
        xs = _silu(_conv_rows(xr, px, cw[:, :D_SSD], cb[:, :D_SSD]))
        bc = _silu(_conv_rows(br, pb, cw[:, D_SSD:], cb[:, D_SSD:]))
        px = xr[rows - 8:, :]
        pb = br[rows - 8:, :]
        y_diag, xdtd_bf, acum_x, bm, cm = _ssd_tile(xs, bc, dt_ref[0, rs, :], None, causal, tri_ref[...], sel_ref[...],
                                                     emat_ref[...], dtb_ref[...], alog_ref[...])
        ht = ht_new
        y_off, states = [], []
        for g in range(SSD_GROUPS):
            bg = bm[:, g * SSD_STATE:(g + 1) * SSD_STATE]
            cg = cm[:, g * SSD_STATE:(g + 1) * SSD_STATE].astype(BF16)
            y_off.append(_dot(cg, ht[:, g * half:(g + 1) * half].astype(BF16)))
            states.append(_dot(bg.T.astype(BF16), xdtd_bf[:, g * half:(g + 1) * half]))
        y = y_diag + jnp.concatenate(y_off, axis=1) * jnp.exp(acum_x)
        ht_new = jnp.exp(acum_x[rows - 1:rows, :]) * ht + jnp.concatenate(states, axis=1)
        y_ref[0, rs, :] = _ssd_finish(y, xs, z_ref[0, rs, :], dskip_ref[...], gnorm_ref[...]).astype(BF16)
    prev_xs[...] = px
    prev_bc[...] = pb
    ht_ref[...] = ht_new

    @pl.when(c == pl.num_programs(1) - 1)
    def _():
        for k in range(D_SSD // LANES):
            st_ref[0, k * LANES:(k + 1) * LANES, :] = ht_new[:, k * LANES:(k + 1) * LANES].T


def _const_spec(shape):
    nd = len(shape)
    return pl.BlockSpec(shape, lambda *_: (0,) * nd)


def _ssd_prompt(zx3, tail3, cw, cb, dtb, alog, dskip, gnorm, tri, sel, emat):
    nb, seq, _ = zx3.shape
    step = SSD_STEP_CHUNKS * SSD_CHUNK
    nc = seq // step
    return pl.pallas_call(
        _ssd_prompt_kernel,
        grid=(nb, nc),
        in_specs=[pl.BlockSpec((1, step, D_SSD), lambda b, c: (b, c, 0)),
                  pl.BlockSpec((1, step, D_SSD), lambda b, c: (b, c, 1)),
                  pl.BlockSpec((1, step, BC_DIM), lambda b, c: (b, c, 0)),
                  pl.BlockSpec((1, step, DT_PAD), lambda b, c: (b, c, BC_DIM // DT_PAD)),
                  _const_spec(cw.shape), _const_spec(cb.shape), _const_spec(dtb.shape), _const_spec(alog.shape),
                  _const_spec(dskip.shape), _const_spec(gnorm.shape), _const_spec(tri.shape),
                  _const_spec(sel.shape), _const_spec(emat.shape)],
        out_specs=[pl.BlockSpec((1, step, D_SSD), lambda b, c: (b, c, 0)),
                   pl.BlockSpec((1, D_SSD, SSD_STATE), lambda b, c: (b, 0, 0))],
        out_shape=[jax.ShapeDtypeStruct((nb, seq, D_SSD), BF16),
                   jax.ShapeDtypeStruct((nb, D_SSD, SSD_STATE), F32)],
        scratch_shapes=[pltpu.VMEM((SSD_STATE, D_SSD), F32),
                        pltpu.VMEM((8, D_SSD), F32),
                        pltpu.VMEM((8, BC_DIM), F32)],
        compiler_params=pltpu.CompilerParams(dimension_semantics=("arbitrary", "arbitrary"),
                                             vmem_limit_bytes=VMEM_LIMIT),
        name="ssd_prompt",
    )(zx3, zx3, tail3, tail3, cw, cb, dtb, alog, dskip, gnorm, tri, sel, emat)


def _ssd_sample_kernel(z_ref, xs_ref, bc_ref, dt_ref, ctx_ref, h0_ref, cw_ref, cb_ref, dtb_ref, alog_ref, dskip_ref,
                       gnorm_ref, tri_ref, sel_ref, emat_ref, y_ref, st_ref,
                       ybase, yoff, xs_s, xdtd_s, acumx_s, bm_s, cm_s):
    s_id = pl.program_id(1)
    rows = SSD_CHUNK
    nseq = rows // SAMPLE_ROWS
    rr = lax.broadcasted_iota(jnp.int32, (rows, 1), 0)

    @pl.when(s_id == 0)
    def _():
        ctx = ctx_ref[0]
        is_ctx = (rr % SAMPLE_ROWS) >= SAMPLE_ROWS - (SSD_CONV - 1)
        xr = jnp.where(is_ctx, ctx[8:, :D_SSD], xs_ref[...])
        br = jnp.where(is_ctx, ctx[8:, D_SSD:], bc_ref[...])
        cw = cw_ref[...]
        cb = cb_ref[...]
        xs = _silu(_conv_rows(xr, ctx[0:8, :D_SSD], cw[:, :D_SSD], cb[:, :D_SSD]))
        bc = _silu(_conv_rows(br, ctx[0:8, D_SSD:], cw[:, D_SSD:], cb[:, D_SSD:]))
        r = lax.broadcasted_iota(jnp.int32, (rows, rows), 0)
        c = lax.broadcasted_iota(jnp.int32, (rows, rows), 1)
        causal = (c <= r) & ((c // SAMPLE_ROWS) == (r // SAMPLE_ROWS))
        row_valid = (rr % SAMPLE_ROWS) < (SAMPLE_ROWS // 2)
        y_diag, xdtd_bf, acum_x, bm, cm = _ssd_tile(xs, bc, dt_ref[...], row_valid, causal, tri_ref[...],
                                                     sel_ref[...], emat_ref[...], dtb_ref[...], alog_ref[...])
        ybase[...] = y_diag
        yoff[...] = jnp.zeros_like(yoff)
        xs_s[...] = xs
        xdtd_s[...] = xdtd_bf
        acumx_s[...] = acum_x
        bm_s[...] = bm
        cm_s[...] = cm

    in_seq_rows = (rr // SAMPLE_ROWS) == s_id
    in_seq_lanes = (lax.broadcasted_iota(jnp.int32, (1, rows), 1) // SAMPLE_ROWS) == s_id
    h0 = h0_ref[0]
    ht = jnp.concatenate([h0[k * LANES:(k + 1) * LANES, :].T for k in range(D_SSD // LANES)], axis=1)
    half = D_SSD // SSD_GROUPS
    y_off, states = [], []
    for g in range(SSD_GROUPS):
        bg_t = jnp.where(in_seq_lanes, bm_s[:, g * SSD_STATE:(g + 1) * SSD_STATE].T, 0.0).astype(BF16)
        cg = cm_s[:, g * SSD_STATE:(g + 1) * SSD_STATE].astype(BF16)
        y_off.append(_dot(cg, ht[:, g * half:(g + 1) * half].astype(BF16)))
        states.append(_dot(bg_t, xdtd_s[:, g * half:(g + 1) * half]))
    yoff[...] = yoff[...] + jnp.where(in_seq_rows, jnp.concatenate(y_off, axis=1), 0.0)
    last = pl.multiple_of(s_id * SAMPLE_ROWS, SAMPLE_ROWS) + (SAMPLE_ROWS - 1)
    ht_new = jnp.exp(acumx_s[pl.ds(last, 1), :]) * ht + jnp.concatenate(states, axis=1)
    for k in range(D_SSD // LANES):
        st_ref[0, k * LANES:(k + 1) * LANES, :] = ht_new[:, k * LANES:(k + 1) * LANES].T

    @pl.when(s_id == nseq - 1)
    def _():
        y = ybase[...] + yoff[...] * jnp.exp(acumx_s[...])
        y_ref[...] = _ssd_finish(y, xs_s[...], z_ref[...], dskip_ref[...], gnorm_ref[...]).astype(BF16)


def _ssd_sample(zx, tail, ctx, h0, cw, cb, dtb, alog, dskip, gnorm, tri, sel, emat):
    m = zx.shape[0]
    nt = m // SSD_CHUNK
    nseq = SSD_CHUNK // SAMPLE_ROWS
    return pl.pallas_call(
        _ssd_sample_kernel,
        grid=(nt, nseq),
        in_specs=[pl.BlockSpec((SSD_CHUNK, D_SSD), lambda t, s: (t, 0)),
                  pl.BlockSpec((SSD_CHUNK, D_SSD), lambda t, s: (t, 1)),
                  pl.BlockSpec((SSD_CHUNK, BC_DIM), lambda t, s: (t, 0)),
                  pl.BlockSpec((SSD_CHUNK, DT_PAD), lambda t, s: (t, BC_DIM // DT_PAD)),
                  pl.BlockSpec((1, SSD_CHUNK + 8, D_SSD + BC_DIM), lambda t, s: (t, 0, 0)),
                  pl.BlockSpec((1, D_SSD, SSD_STATE), lambda t, s: (t * nseq + s, 0, 0)),
                  _const_spec(cw.shape), _const_spec(cb.shape), _const_spec(dtb.shape), _const_spec(alog.shape),
                  _const_spec(dskip.shape), _const_spec(gnorm.shape), _const_spec(tri.shape),
                  _const_spec(sel.shape), _const_spec(emat.shape)],
        out_specs=[pl.BlockSpec((SSD_CHUNK, D_SSD), lambda t, s: (t, 0)),
                   pl.BlockSpec((1, D_SSD, SSD_STATE), lambda t, s: (t * nseq + s, 0, 0))],
        out_shape=[jax.ShapeDtypeStruct((m, D_SSD), BF16),
                   jax.ShapeDtypeStruct(h0.shape, F32)],
        scratch_shapes=[pltpu.VMEM((SSD_CHUNK, D_SSD), F32),
                        pltpu.VMEM((SSD_CHUNK, D_SSD), F32),
                        pltpu.VMEM((SSD_CHUNK, D_SSD), F32),
                        pltpu.VMEM((SSD_CHUNK, D_SSD), BF16),
                        pltpu.VMEM((SSD_CHUNK, D_SSD), F32),
                        pltpu.VMEM((SSD_CHUNK, SSD_GROUPS * SSD_STATE), F32),
                        pltpu.VMEM((SSD_CHUNK, SSD_GROUPS * SSD_STATE), F32)],
        compiler_params=pltpu.CompilerParams(dimension_semantics=("arbitrary", "arbitrary"),
                                             vmem_limit_bytes=VMEM_LIMIT),
        name="ssd_sample",
    )(zx, zx, tail, tail, ctx, h0, cw, cb, dtb, alog, dskip, gnorm, tri, sel, emat)


def _sb_block(qb, kb, vb, bias, uo, run, mask, nearest_first):
    z = _dot_nt(qb, kb) * ATT_SCALE + bias
    nz = -z
    lg = jnp.log(1.0 + jnp.exp(jnp.minimum(z, nz)))
    ls_pos = jnp.minimum(z, 0.0) - lg
    lm = jnp.minimum(nz, 0.0) - lg
    if mask is not None:
        lm = jnp.where(mask, lm, 0.0)
    hi = lm.astype(BF16)
    lo = (lm - hi.astype(F32)).astype(BF16)
    nsub = z.shape[1] // LANES
    tails = [None] * nsub
    for c in (range(nsub) if nearest_first else reversed(range(nsub))):
        cols = slice(c * LANES, (c + 1) * LANES)
        st = _dot(jnp.concatenate([hi[:, cols], lo[:, cols]], axis=1), uo)
        tails[c] = run + st[:, :LANES]
        run = run + st[:, LANES:]
    w = jnp.exp(ls_pos + jnp.concatenate(tails, axis=1))
    if mask is not None:
        w = jnp.where(mask, w, 0.0)
    return _dot(w.astype(BF16), vb), run


ATT_TQ = 512
ATT_TK = ATT_TQ


def _attn_prompt_kernel(bias_ref, q_ref, k_ref, v_ref, uo_ref, o_ref):
    h = pl.program_id(1)
    qi = pl.program_id(2)
    qb = q_ref[0]
    kbf = k_ref.at[0]
    vbf = v_ref.at[0]
    bias = bias_ref[h]
    uo = uo_ref[...]
    q0 = pl.multiple_of(qi * ATT_TQ, ATT_TQ)

    def diagonal():
        accs, runs = [], []
        for r in range(ATT_TQ // LANES):
            nk = (r + 1) * LANES
            row = lax.broadcasted_iota(jnp.int32, (LANES, nk), 0) + r * LANES
            col = lax.broadcasted_iota(jnp.int32, (LANES, nk), 1)
            a, rn = _sb_block(qb[r * LANES:(r + 1) * LANES], kbf[pl.ds(q0, nk), :], vbf[pl.ds(q0, nk), :], bias, uo,
                              jnp.zeros((LANES, LANES), F32), col < row, False)
            accs.append(a)
            runs.append(rn)
        return jnp.concatenate(accs, axis=0), jnp.concatenate(runs, axis=0)

    def block(kb, acc, run):
        start = pl.multiple_of(kb * ATT_TK, ATT_TK)
        pv, run = _sb_block(qb, kbf[pl.ds(start, ATT_TK), :], vbf[pl.ds(start, ATT_TK), :], bias, uo, run, None, False)
        return acc + pv, run

    @pl.when(qi == 0)
    def _():
        acc, _ = diagonal()
        o_ref[0] = acc.astype(BF16)

    @pl.when(qi > 0)
    def _():
        acc, run = block(qi - 1, *diagonal())
        acc, _ = lax.fori_loop(0, qi - 1, lambda i, carry: block(qi - 2 - i, *carry), (acc, run))
        o_ref[0] = acc.astype(BF16)


def _attn_prompt(qkv3, sb_bias, uo):
    nb, seq, _ = qkv3.shape
    tq = ATT_TQ
    return pl.pallas_call(
        _attn_prompt_kernel,
        grid=(nb, ATT_HEADS, seq // tq),
        in_specs=[pl.BlockSpec(memory_space=pltpu.SMEM),
                  pl.BlockSpec((1, tq, ATT_HEAD_DIM), lambda b, h, i: (b, i, h)),
                  pl.BlockSpec((1, seq, ATT_HEAD_DIM), lambda b, h, i: (b, 0, ATT_HEADS + h)),
                  pl.BlockSpec((1, seq, ATT_HEAD_DIM), lambda b, h, i: (b, 0, 2 * ATT_HEADS + h)),
                  _const_spec(uo.shape)],
        out_specs=pl.BlockSpec((1, tq, ATT_HEAD_DIM), lambda b, h, i: (b, i, h)),
        out_shape=jax.ShapeDtypeStruct((nb, seq, D_ATT), BF16),
        compiler_params=pltpu.CompilerParams(dimension_semantics=("arbitrary", "arbitrary", "arbitrary"),
                                             vmem_limit_bytes=VMEM_LIMIT),
        name="attn_prompt",
    )(sb_bias, qkv3, qkv3, qkv3, uo)


PAGES_PER_STEP = 8
PAGE_SLOTS = 3


def _page_by_head(ref):
    heads = [ref[pl.ds(h, PAGE_SIZE, stride=ATT_HEADS), :] for h in range(ATT_HEADS)]
    return jnp.concatenate(heads, axis=1).astype(BF16)


def _attn_sample_kernel(nseq, steps, pt_ref, q_ref, kn_ref, vn_ref, bias_ref, uo_ref, ck_hbm, cv_hbm, o_ref,
                        qbd, knew, vnew, acc, run, kbuf, vbuf, sems):
    b = pl.program_id(0)
    j = pl.program_id(1)
    total = nseq * steps
    t = b * steps + j
    nrow = ATT_HEADS * SAMPLE_ROWS
    bias = bias_ref[...]
    uo = uo_ref[...]

    def page_copies(step):
        sb = step // steps
        sj = step - sb * steps
        slot = step % PAGE_SLOTS
        copies = []
        for g in range(PAGES_PER_STEP):
            page = pt_ref[sb, steps * PAGES_PER_STEP - 1 - (sj * PAGES_PER_STEP + g)]
            copies.append(pltpu.make_async_copy(ck_hbm.at[page], kbuf.at[slot, g], sems.at[0, slot]))
            copies.append(pltpu.make_async_copy(cv_hbm.at[page], vbuf.at[slot, g], sems.at[1, slot]))
        return copies

    @pl.when(t == 0)
    def _():
        for step in range(PAGE_SLOTS - 1):
            for c in page_copies(step):
                c.start()

    @pl.when(t + (PAGE_SLOTS - 1) < total)
    def _():
        for c in page_copies(t + (PAGE_SLOTS - 1)):
            c.start()

    @pl.when((b == 0) & (j == 0))
    def _():
        knew[...] = jnp.zeros_like(knew)
        vnew[...] = jnp.zeros_like(vnew)

    @pl.when(j == 0)
    def _():
        q8 = q_ref[0]
        rr = lax.broadcasted_iota(jnp.int32, (nrow, D_ATT), 0)
        cc = lax.broadcasted_iota(jnp.int32, (nrow, D_ATT), 1)
        qt = jnp.concatenate([q8] * ATT_HEADS, axis=0)
        qbd[...] = jnp.where((rr // SAMPLE_ROWS) == (cc // ATT_HEAD_DIM), qt, 0.0).astype(BF16)
        knew[0:SAMPLE_ROWS, :] = kn_ref[0]
        vnew[0:SAMPLE_ROWS, :] = vn_ref[0]
        qq = lax.broadcasted_iota(jnp.int32, (nrow, LANES), 0) % SAMPLE_ROWS
        kk = lax.broadcasted_iota(jnp.int32, (nrow, LANES), 1)
        mask = (kk < qq) & (kk < SAMPLE_ROWS // 2)
        pv, r = _sb_block(qbd[...], knew[...].astype(BF16), vnew[...].astype(BF16), bias, uo,
                          jnp.zeros((nrow, LANES), F32), mask, True)
        acc[...] = pv
        run[...] = r

    for c in page_copies(t):
        c.wait()
    slot = t % PAGE_SLOTS
    kcat = jnp.concatenate([_page_by_head(kbuf.at[slot, g]) for g in range(PAGES_PER_STEP)], axis=0)
    vcat = jnp.concatenate([_page_by_head(vbuf.at[slot, g]) for g in range(PAGES_PER_STEP)], axis=0)
    pv, r = _sb_block(qbd[...], kcat, vcat, jnp.concatenate([bias] * PAGES_PER_STEP, axis=1), uo, run[...], None, True)
    a = acc[...] + pv
    acc[...] = a
    run[...] = r

    @pl.when(j == steps - 1)
    def _():
        for h in range(ATT_HEADS):
            o_ref[0, :, h * ATT_HEAD_DIM:(h + 1) * ATT_HEAD_DIM] = (
                a[h * SAMPLE_ROWS:(h + 1) * SAMPLE_ROWS, h * ATT_HEAD_DIM:(h + 1) * ATT_HEAD_DIM])


def _attn_sample(page_table, projs3, cache_k, cache_v, bias_rows, uo):
    nseq, npages = page_table.shape
    steps = npages // PAGES_PER_STEP
    assert steps * PAGES_PER_STEP == npages and nseq * steps >= PAGE_SLOTS - 1
    page_buf = pltpu.VMEM((PAGE_SLOTS, PAGES_PER_STEP, PAGE_SIZE * ATT_HEADS, ATT_HEAD_DIM), F32)

    grid_spec = pltpu.PrefetchScalarGridSpec(
        num_scalar_prefetch=1,
        grid=(nseq, steps),
        in_specs=[pl.BlockSpec((1, SAMPLE_ROWS, D_ATT), lambda b, j, pt: (b, 0, 0)),
                  pl.BlockSpec((1, SAMPLE_ROWS, D_ATT), lambda b, j, pt: (b, 0, 1)),
                  pl.BlockSpec((1, SAMPLE_ROWS, D_ATT), lambda b, j, pt: (b, 0, 2)),
                  pl.BlockSpec(bias_rows.shape, lambda b, j, pt: (0, 0)),
                  pl.BlockSpec(uo.shape, lambda b, j, pt: (0, 0)),
                  pl.BlockSpec(memory_space=pl.ANY), pl.BlockSpec(memory_space=pl.ANY)],
        out_specs=pl.BlockSpec((1, SAMPLE_ROWS, D_ATT), lambda b, j, pt: (b, 0, 0)),
        scratch_shapes=[pltpu.VMEM((ATT_HEADS * SAMPLE_ROWS, D_ATT), BF16),
                        pltpu.VMEM((PAGE_SIZE, D_ATT), F32),
                        pltpu.VMEM((PAGE_SIZE, D_ATT), F32),
                        pltpu.VMEM((ATT_HEADS * SAMPLE_ROWS, D_ATT), F32),
                        pltpu.VMEM((ATT_HEADS * SAMPLE_ROWS, LANES), F32),
                        page_buf, page_buf, pltpu.SemaphoreType.DMA((2, PAGE_SLOTS))])
    return pl.pallas_call(
        functools.partial(_attn_sample_kernel, nseq, steps),
        grid_spec=grid_spec,
        out_shape=jax.ShapeDtypeStruct((nseq, SAMPLE_ROWS, D_ATT), F32),
        compiler_params=pltpu.CompilerParams(dimension_semantics=("arbitrary", "arbitrary"),
                                             vmem_limit_bytes=VMEM_LIMIT),
        name="attn_sample",
    )(page_table, projs3, projs3, projs3, bias_rows, uo, cache_k, cache_v)


def _outproj_kernel(y_ref, o_ref, x_ref, g1_ref, sh2_ref, sc2_ref, gpost_ref, gpre_ref, wy_ref, wo_ref, x1_ref, h2_ref):
    tm = x_ref.shape[0]
    chunk = min(tm, 256)
    for r in range(0, tm, chunk):
        rows = slice(r, r + chunk)
        g1, sh2, sc2 = (v if v.shape[0] == 1 else v[rows] for v in (g1_ref[0], sh2_ref[0], sc2_ref[0]))
        m = _dot(y_ref[rows, :], wy_ref[...]) + _dot(o_ref[rows, :].astype(BF16), wo_ref[...])
        x1 = x_ref[rows, :] + g1 * _rms(m, gpost_ref[...])
        x1_ref[rows, :] = x1
        h2 = _rms(x1, gpre_ref[...]) * (1.0 + sc2) + sh2
        h2_ref[rows, :] = h2.astype(BF16)


def _outproj(y, o, x, mods, gpost, gpre, w_out, tm, tiles_per_group):
    m = x.shape[0]
    rows = mods.shape[1]
    half = w_out.shape[0] // 2
    row = lambda i: (i, 0)
    return pl.pallas_call(
        _outproj_kernel,
        grid=(m // tm,),
        in_specs=[pl.BlockSpec((tm, D_SSD), row), pl.BlockSpec((tm, D_ATT), row), pl.BlockSpec((tm, D_MODEL), row),
                  _mod_spec(2, rows, tiles_per_group), _mod_spec(3, rows, tiles_per_group),
                  _mod_spec(4, rows, tiles_per_group),
                  _const_spec(gpost.shape), _const_spec(gpre.shape),
                  pl.BlockSpec((half, D_MODEL), lambda i: (0, 0)),
                  pl.BlockSpec((half, D_MODEL), lambda i: (1, 0))],
        out_specs=[pl.BlockSpec((tm, D_MODEL), row), pl.BlockSpec((tm, D_MODEL), row)],
        out_shape=[jax.ShapeDtypeStruct((m, D_MODEL), F32), jax.ShapeDtypeStruct((m, D_MODEL), BF16)],
        compiler_params=pltpu.CompilerParams(dimension_semantics=("arbitrary",), vmem_limit_bytes=VMEM_LIMIT),
        name="out_proj",
    )(y, o, x, mods, mods, mods, gpost, gpre, w_out, w_out)


FFN_CHUNK = 256


def _ffn_kernel(sample, tiles_per_group, *refs):
    if sample:
        (h_ref, x1_hbm, g2_ref, gpost_ref, wg_ref, wv_ref, wo_ref, cwg_ref, cwv_ref, cbg_ref, cbv_ref,
         injg_ref, injv_ref, y_ref, ug_ref, uv_ref, x1_buf, x1_sem) = refs
    else:
        (h_ref, x1_hbm, g2_ref, gpost_ref, wg_ref, wv_ref, wo_ref, cwg_ref, cwv_ref, cbg_ref, cbv_ref,
         y_ref, ug_ref, uv_ref, x1_buf, x1_sem, halog, halov) = refs
    i = pl.program_id(0)
    j = pl.program_id(1)
    tm = h_ref.shape[0]
    tf = wg_ref.shape[1]
    hn = h_ref[...]
    x1_copy = pltpu.make_async_copy(x1_hbm.at[pl.ds(pl.multiple_of(i * tm, tm), tm), :], x1_buf, x1_sem)

    @pl.when(j == 0)
    def _():
        x1_copy.start()
        y_ref[...] = jnp.zeros_like(y_ref)

    if sample:
        is_ctx = (lax.broadcasted_iota(jnp.int32, (tm, 1), 0) % SAMPLE_ROWS) >= SAMPLE_ROWS - (FFN_CONV - 1)
    else:
        @pl.when((i == 0) & (j == 0))
        def _():
            halog[...] = jnp.zeros_like(halog)
            halov[...] = jnp.zeros_like(halov)

        first = lax.rem(i, tiles_per_group) == 0

    def half(w_ref, cw_ref, cb_ref, u_ref, inj_ref, halo, cols):
        u = jnp.concatenate([_dot(hn[r:r + 128], w_ref[:, cols]) for r in range(0, tm, 128)], axis=0)
        if sample:
            u_ref[:, cols] = u
            prev = inj_ref[0:8, cols]
            u = jnp.where(is_ctx, inj_ref[8:, cols], u)
        else:
            prev = jnp.where(first, 0.0, halo[j, :, cols])
            halo[j, :, cols] = u[tm - 8:, :]
            u_ref[0, :, cols] = u[tm - 8:, :]
        return _conv_rows(u, prev, cw_ref[:, cols], cb_ref[:, cols])

    acts = []
    for c in range(tf // FFN_CHUNK):
        cols = slice(c * FFN_CHUNK, (c + 1) * FFN_CHUNK)
        gate = half(wg_ref, cwg_ref, cbg_ref, ug_ref, None if not sample else injg_ref, None if sample else halog, cols)
        val = half(wv_ref, cwv_ref, cbv_ref, uv_ref, None if not sample else injv_ref, None if sample else halov, cols)
        acts.append((_silu(gate) * val).astype(BF16))
    y_ref[...] += _dot(jnp.concatenate(acts, axis=1), wo_ref[...])

    @pl.when(j == pl.num_programs(1) - 1)
    def _():
        x1_copy.wait()
        chunk = min(tm, 256)
        for r in range(0, tm, chunk):
            rows = slice(r, r + chunk)
            g2 = g2_ref[0]
            g2 = g2 if g2.shape[0] == 1 else g2[rows]
            y_ref[rows, :] = x1_buf[rows, :] + g2 * _rms(y_ref[rows, :], gpost_ref[...])


def _ffn(h2, x1, mods, gpost, w_in, w_out, cw, cb, tm, tiles_per_group, inj=None):
    m = h2.shape[0]
    rows = mods.shape[1]
    tf = 512
    nf = D_FF // tf
    sample = inj is not None
    row = lambda i, j: (i, 0)
    gate_col = lambda i, j: (0, j)
    val_col = lambda i, j: (0, nf + j)
    in_specs = [pl.BlockSpec((tm, D_MODEL), row), pl.BlockSpec(memory_space=pl.ANY),
                _mod_spec(5, rows, tiles_per_group), _const_spec(gpost.shape),
                pl.BlockSpec((D_MODEL, tf), gate_col), pl.BlockSpec((D_MODEL, tf), val_col),
                pl.BlockSpec((tf, D_MODEL), lambda i, j: (j, 0)),
                pl.BlockSpec((FFN_CONV, tf), gate_col), pl.BlockSpec((FFN_CONV, tf), val_col),
                pl.BlockSpec((1, tf), gate_col), pl.BlockSpec((1, tf), val_col)]
    args = [h2, x1, mods, gpost, w_in, w_in, w_out, cw, cw, cb, cb]
    scratch = [pltpu.VMEM((tm, D_MODEL), F32), pltpu.SemaphoreType.DMA(())]
    if sample:
        in_specs += [pl.BlockSpec((m + 8, tf), gate_col), pl.BlockSpec((m + 8, tf), val_col)]
        args += [inj, inj]
        u_specs = [pl.BlockSpec((tm, tf), lambda i, j: (i, j))] * 2
        u_shapes = [jax.ShapeDtypeStruct((m, D_FF), F32)] * 2
    else:
        u_specs = [pl.BlockSpec((1, 8, tf), lambda i, j: (i, 0, j))] * 2
        u_shapes = [jax.ShapeDtypeStruct((m // tm, 8, D_FF), F32)] * 2
        scratch += [pltpu.VMEM((nf, 8, tf), F32), pltpu.VMEM((nf, 8, tf), F32)]
    return pl.pallas_call(
        functools.partial(_ffn_kernel, sample, tiles_per_group),
        grid=(m // tm, nf),
        in_specs=in_specs,
        out_specs=[pl.BlockSpec((tm, D_MODEL), row)] + u_specs,
        out_shape=[jax.ShapeDtypeStruct((m, D_MODEL), F32)] + u_shapes,
        scratch_shapes=scratch,
        compiler_params=pltpu.CompilerParams(dimension_semantics=("arbitrary", "arbitrary"),
                                             vmem_limit_bytes=VMEM_LIMIT),
        name="ffn_sample" if sample else "ffn_prompt",
    )(*args)


def _ssd_constants(block_rows):
    r = jnp.arange(SSD_CHUNK)[:, None]
    c = jnp.arange(SSD_CHUNK)[None, :]
    same = (r // block_rows) == (c // block_rows)
    tri = ((c <= r) & same).astype(BF16)
    sel = (c == (r // block_rows) * block_rows + block_rows - 1).astype(BF16)
    hrow = jnp.arange(LANES)[:, None]
    e1 = (hrow == jnp.arange(D_SSD)[None, :] // SSD_HEAD_DIM).astype(BF16)
    e2 = (hrow == jnp.arange(SSD_HEADS * LANES)[None, :] // LANES).astype(BF16)
    return tri, sel, jnp.concatenate([e1, e2], axis=1)


def _suffix_matrix():
    j = jnp.arange(LANES)[:, None]
    s = jnp.arange(LANES)[None, :]
    u = jnp.concatenate([(j > s).astype(BF16), jnp.ones((LANES, LANES), BF16)], axis=1)
    return jnp.concatenate([u, u], axis=0)


def _inject_rows(ctx, nctx):
    nseq, _, c = ctx.shape
    body = jnp.concatenate([ctx[1:], jnp.zeros((1, nctx, c), F32)], axis=0)
    body = jnp.pad(body, ((0, 0), (SAMPLE_ROWS - nctx, 0), (0, 0))).reshape(nseq * SAMPLE_ROWS, c)
    head = jnp.pad(ctx[0], ((8 - nctx, 0), (0, 0)))
    return jnp.concatenate([head, body], axis=0)


def kernel(x_prompt, x_sample, c_prompt, c_sample, cache_k, cache_v, page_table, state_ssm, state_conv, state_ffn_conv, w_ada, b_ada, g_pre_mix, g_post_mix, g_pre_ffn, g_post_ffn, w_in, conv_w, conv_b, dt_bias, a_log, d_skip, g_ssd_norm, sb_bias, w_out, w_ffn_in, ffn_conv_w, ffn_conv_b, w_ffn_out):
    assert w_ada.shape[0] == 1, "one layer"
    nb, seq, d = x_prompt.shape
    ns, ls, _ = x_sample.shape
    n_pool = cache_k.shape[1]
    mp = nb * seq
    ms = ns * SAMPLE_ROWS

    w = w_in[0]
    o_bc, o_q = 2 * D_SSD, 2 * D_SSD + BC_DIM + SSD_HEADS
    w_zx = w[:, :o_bc].astype(BF16)
    w_qkv = w[:, o_q:].astype(BF16)
    w_tail = jnp.pad(w[:, o_bc:o_q], ((0, 0), (0, DT_PAD - SSD_HEADS))).astype(BF16)
    w_out_b = w_out[0].astype(BF16)
    w_ffn_in_b = w_ffn_in[0].astype(BF16)
    w_ffn_out_b = w_ffn_out[0].astype(BF16)
    row = lambda v: v.reshape(1, -1)
    dtb = jnp.pad(dt_bias[0], (0, DT_PAD - SSD_HEADS)).reshape(1, DT_PAD)
    alog = jnp.pad(a_log[0], (0, DT_PAD - SSD_HEADS)).reshape(1, DT_PAD)
    dskip = jnp.repeat(d_skip[0], SSD_HEAD_DIM).reshape(1, D_SSD)
    gnorm = row(g_ssd_norm[0])
    cw, cb = conv_w[0], row(conv_b[0])
    fcw, fcb = ffn_conv_w[0], row(ffn_conv_b[0])
    uo = _suffix_matrix()

    n_c = nb + ns
    c_all = jnp.concatenate([c_prompt, c_sample, jnp.zeros((-n_c % 16, d), F32)], axis=0)
    mod = _ada(c_all, w_ada[0], row(b_ada[0]))
    mods_p = mod[:nb].reshape(nb, 1, 6 * d)
    mods_s = jnp.repeat(mod[nb:n_c], SAMPLE_ROWS, axis=0).reshape(1, ms, 6 * d)

    tm_p = 512
    tpg_p = seq // tm_p
    xp = x_prompt.reshape(mp, d)
    hn_p, tail_p = _prenorm(xp, mods_p, row(g_pre_mix[0]), w_tail, tm_p, tpg_p)
    zx_p, qkv_p, k_p, v_p = _inproj(hn_p, w_zx, w_qkv, 1024, BF16)
    zx3 = zx_p.reshape(nb, seq, 2 * D_SSD)
    tail3 = tail_p.reshape(nb, seq, PROJ_TAIL)
    tri, sel, emat = _ssd_constants(SSD_CHUNK)
    y_ssd, ssm_p = _ssd_prompt(zx3, tail3, cw, cb, dtb, alog, dskip, gnorm, tri, sel, emat)
    o_att = _attn_prompt(qkv_p.reshape(nb, seq, 3 * D_ATT), sb_bias[0], uo)
    tm_e = 512
    x1_p, h2_p = _outproj(y_ssd.reshape(mp, D_SSD), o_att.reshape(mp, D_ATT), xp, mods_p, row(g_post_mix[0]),
                          row(g_pre_ffn[0]), w_out_b, tm_e, seq // tm_e)
    tm_f = 1024
    yp, ctxg_p, ctxv_p = _ffn(h2_p, x1_p, mods_p, row(g_post_ffn[0]), w_ffn_in_b, w_ffn_out_b, fcw, fcb,
                              tm_f, seq // tm_f)

    xs_pad = jnp.pad(x_sample, ((0, 0), (0, SAMPLE_ROWS - ls), (0, 0))).reshape(ms, d)
    hn_s, tail_s = _prenorm(xs_pad, mods_s, row(g_pre_mix[0]), w_tail, ms, 1)
    zx_s, qkv_s, k_s, v_s = _inproj(hn_s, w_zx, w_qkv, ms, F32)
    seq_per_tile = SSD_CHUNK // SAMPLE_ROWS
    conv_ctx = state_conv[0].reshape(ns // seq_per_tile, seq_per_tile, SSD_CONV - 1, -1)
    ctx_inj = jax.vmap(lambda c: _inject_rows(c, SSD_CONV - 1))(conv_ctx)
    tri_s, sel_s, _ = _ssd_constants(SAMPLE_ROWS)
    y_ssd_s, ssm_s = _ssd_sample(zx_s, tail_s, ctx_inj, state_ssm[0].reshape(ns, D_SSD, SSD_STATE), cw, cb, dtb, alog,
                                 dskip, gnorm, tri_s, sel_s, emat)
    bias_rows = jnp.broadcast_to(jnp.repeat(sb_bias[0], SAMPLE_ROWS)[:, None], (ATT_HEADS * SAMPLE_ROWS, LANES))
    o_att_s = _attn_sample(page_table, qkv_s.reshape(ns, SAMPLE_ROWS, 3 * D_ATT),
                           cache_k[0].reshape(n_pool, PAGE_SIZE * ATT_HEADS, ATT_HEAD_DIM),
                           cache_v[0].reshape(n_pool, PAGE_SIZE * ATT_HEADS, ATT_HEAD_DIM), bias_rows, uo)
    x1_s, h2_s = _outproj(y_ssd_s, o_att_s.reshape(ms, D_ATT), xs_pad, mods_s, row(g_post_mix[0]),
                          row(g_pre_ffn[0]), w_out_b, ms, 1)
    inj_f = _inject_rows(state_ffn_conv[0], FFN_CONV - 1)
    ys_pad, ug_s, uv_s = _ffn(h2_s, x1_s, mods_s, row(g_post_ffn[0]), w_ffn_in_b, w_ffn_out_b, fcw, fcb,
                              ms, 1, inj=inj_f)

    heads = (ATT_HEADS, ATT_HEAD_DIM)
    k_p = k_p.reshape(1, nb, seq, *heads)
    v_p = v_p.reshape(1, nb, seq, *heads)
    tail = slice(seq - (SSD_CONV - 1), seq)
    conv_p = jnp.concatenate([zx3[:, tail, D_SSD:], tail3[:, tail, :BC_DIM]], axis=-1)[None]
    ffn_p = jnp.concatenate([ctxg_p, ctxv_p], axis=-1)[seq // tm_f - 1::seq // tm_f, 8 - (FFN_CONV - 1):][None]
    k_s = k_s.reshape(ns, SAMPLE_ROWS, *heads)[None, :, :ls]
    v_s = v_s.reshape(ns, SAMPLE_ROWS, *heads)[None, :, :ls]
    raw_s = jnp.concatenate([zx_s.reshape(ns, SAMPLE_ROWS, -1)[:, :ls, D_SSD:],
                             tail_s.reshape(ns, SAMPLE_ROWS, -1)[:, :ls, :BC_DIM]], axis=-1)
    conv_s = jnp.concatenate([state_conv[0], raw_s], axis=1)[:, -(SSD_CONV - 1):][None]
    u_s = jnp.concatenate([ug_s, uv_s], axis=-1).reshape(ns, SAMPLE_ROWS, 2 * D_FF)[:, :ls]
    ffn_s = jnp.concatenate([state_ffn_conv[0], u_s], axis=1)[:, -(FFN_CONV - 1):][None]
    return (yp.reshape(nb, seq, d), ys_pad.reshape(ns, SAMPLE_ROWS, d)[:, :ls],
            k_p, v_p, ssm_p.reshape(1, nb, SSD_HEADS, SSD_HEAD_DIM, SSD_STATE), conv_p, ffn_p,
            k_s, v_s, ssm_s.reshape(1, ns, SSD_HEADS, SSD_HEAD_DIM, SSD_STATE), conv_s, ffn_s)
```

```python
import functools

import jax
import jax.numpy as jnp
from jax import lax
from jax.experimental import pallas as pl
from jax.experimental.pallas import tpu as pltpu

F32 = jnp.float32
BF16 = jnp.bfloat16

D_MODEL = 2048
D_SSD = 1024
SSD_HEAD_DIM = 64
SSD_HEADS = 16
SSD_GROUPS = 2
SSD_STATE = 128
SSD_CONV = 4
SSD_CHUNK = 128
SSD_STEP_CHUNKS = 4
BC_DIM = 2 * SSD_GROUPS * SSD_STATE
D_ATT = 1024
ATT_HEAD_DIM = 128
ATT_HEADS = 8
D_FF = 5632
FFN_CONV = 3
PAGE_SIZE = 128
EPS = 1e-6
ATT_SCALE = ATT_HEAD_DIM ** -0.5

PROJ_TN = 1024
PROJ_MAIN = 5 * PROJ_TN
DT_PAD = 128
PROJ_TAIL = BC_DIM + DT_PAD

SAMPLE_ROWS = 8
LANES = 128
VMEM_LIMIT = 56 * 1024 * 1024


def _dot(a, b):
    return jnp.dot(a, b, preferred_element_type=F32)


def _dot_nt(a, b):
    return lax.dot_general(a, b, (((1,), (1,)), ((), ())), preferred_element_type=F32)


def _sigmoid(x):
    return 1.0 / (1.0 + jnp.exp(-x))


def _silu(x):
    return x * _sigmoid(x)


def _softplus(x):
    return jnp.maximum(x, 0.0) + jnp.log1p(jnp.exp(-jnp.abs(x)))


def _rms(x, g):
    ms = jnp.mean(x * x, axis=-1, keepdims=True)
    return x * lax.rsqrt(ms + EPS) * g


def _split3(x):
    h = x.astype(BF16)
    r = x - h.astype(F32)
    m = r.astype(BF16)
    l = (r - m.astype(F32)).astype(BF16)
    return h, m, l


def _dot3_right(x, mat):
    return _dot(jnp.concatenate(_split3(x), axis=1), jnp.concatenate([mat] * 3, axis=0))


def _dot3_left(mat, x):
    return _dot(jnp.concatenate([mat] * 3, axis=1), jnp.concatenate(_split3(x), axis=0))


def _ada_kernel(c_ref, w_ref, b_ref, o_ref):
    s = _silu(c_ref[...]).astype(BF16)
    o_ref[...] = _dot(s, w_ref[...].astype(BF16)) + b_ref[...]


def _ada(c_all, w_ada, b_ada):
    m, d = c_all.shape
    n = w_ada.shape[1]
    tn = 1024
    return pl.pallas_call(
        _ada_kernel,
        grid=(n // tn,),
        in_specs=[pl.BlockSpec((m, d), lambda j: (0, 0)),
                  pl.BlockSpec((d, tn), lambda j: (0, j)),
                  pl.BlockSpec((1, tn), lambda j: (0, j))],
        out_specs=pl.BlockSpec((m, tn), lambda j: (0, j)),
        out_shape=jax.ShapeDtypeStruct((m, n), F32),
        compiler_params=pltpu.CompilerParams(dimension_semantics=("arbitrary",), vmem_limit_bytes=VMEM_LIMIT),
        name="ada_mod",
    )(c_all, w_ada, b_ada)


def _mod_spec(idx, rows, tiles_per_group):
    return pl.BlockSpec((1, rows, D_MODEL), lambda i, *_: (i // tiles_per_group, 0, idx))


def _prenorm_kernel(x_ref, sh_ref, sc_ref, g_ref, wt_ref, hn_ref, tail_ref):
    tm = x_ref.shape[0]
    chunk = min(tm, 128)
    g = g_ref[...]
    for r in range(0, tm, chunk):
        sc, sh = sc_ref[0], sh_ref[0]
        if sc.shape[0] != 1:
            sc, sh = sc[r:r + chunk], sh[r:r + chunk]
        h = (_rms(x_ref[r:r + chunk, :], g) * (1.0 + sc) + sh).astype(BF16)
        hn_ref[r:r + chunk, :] = h
        tail_ref[r:r + chunk, :] = _dot(h, wt_ref[...])


def _prenorm(x, mods, g, wt, tm, tiles_per_group):
    m = x.shape[0]
    rows = mods.shape[1]
    return pl.pallas_call(
        _prenorm_kernel,
        grid=(m // tm,),
        in_specs=[pl.BlockSpec((tm, D_MODEL), lambda i: (i, 0)),
                  _mod_spec(0, rows, tiles_per_group),
                  _mod_spec(1, rows, tiles_per_group),
                  _const_spec(g.shape), _const_spec(wt.shape)],
        out_specs=[pl.BlockSpec((tm, D_MODEL), lambda i: (i, 0)), pl.BlockSpec((tm, PROJ_TAIL), lambda i: (i, 0))],
        out_shape=[jax.ShapeDtypeStruct((m, D_MODEL), BF16), jax.ShapeDtypeStruct((m, PROJ_TAIL), F32)],
        compiler_params=pltpu.CompilerParams(dimension_semantics=("arbitrary",), vmem_limit_bytes=VMEM_LIMIT),
        name="pre_norm",
    )(x, mods, mods, g, wt)


def _inproj_kernel(hn_ref, wzx_ref, w_ref, zx_ref, qkv_ref, k_ref, v_ref):
    j = pl.program_id(1)
    tm = hn_ref.shape[0]

    def project(state_ref):
        res = _dot(hn_ref[...], w_ref[...])
        qkv_ref[...] = res.astype(qkv_ref.dtype)
        if state_ref is not None:
            for h in range(ATT_HEADS):
                state_ref[pl.ds(h, tm, stride=ATT_HEADS), :] = res[:, h * ATT_HEAD_DIM:(h + 1) * ATT_HEAD_DIM]

    @pl.when(j < 2)
    def _():
        zx_ref[...] = _dot(hn_ref[...], wzx_ref[...])

    @pl.when(j == 2)
    def _():
        project(None)

    @pl.when(j == 3)
    def _():
        project(k_ref)

    @pl.when(j == 4)
    def _():
        project(v_ref)


def _inproj(hn, w_zx, w_qkv, tm, qkv_dtype):
    m = hn.shape[0]
    kv_spec = pl.BlockSpec((tm * ATT_HEADS, ATT_HEAD_DIM), lambda i, j: (i, 0))
    kv_shape = jax.ShapeDtypeStruct((m * ATT_HEADS, ATT_HEAD_DIM), F32)
    return pl.pallas_call(
        _inproj_kernel,
        grid=(m // tm, PROJ_MAIN // PROJ_TN),
        in_specs=[pl.BlockSpec((tm, D_MODEL), lambda i, j: (i, 0)),
                  pl.BlockSpec((D_MODEL, PROJ_TN), lambda i, j: (0, jnp.minimum(j, 1))),
                  pl.BlockSpec((D_MODEL, PROJ_TN), lambda i, j: (0, jnp.clip(j - 2, 0, 2)))],
        out_specs=[pl.BlockSpec((tm, PROJ_TN), lambda i, j: (i, jnp.minimum(j, 1))),
                   pl.BlockSpec((tm, PROJ_TN), lambda i, j: (i, jnp.clip(j - 2, 0, 2))),
                   kv_spec, kv_spec],
        out_shape=[jax.ShapeDtypeStruct((m, 2 * PROJ_TN), F32),
                   jax.ShapeDtypeStruct((m, 3 * PROJ_TN), qkv_dtype), kv_shape, kv_shape],
        compiler_params=pltpu.CompilerParams(dimension_semantics=("arbitrary", "arbitrary"),
                                             vmem_limit_bytes=VMEM_LIMIT),
        name="in_proj",
    )(hn, w_zx, w_qkv)


def _shift_rows(u, prev8, s):
    r = pltpu.roll(u, s, 0)
    head = jnp.where(lax.broadcasted_iota(jnp.int32, (8, 1), 0) < s, pltpu.roll(prev8, s, 0), r[:8])
    return jnp.concatenate([head, r[8:]], axis=0)


def _conv_rows(u, prev8, w, b):
    taps = w.shape[0]
    y = _shift_rows(u, prev8, taps - 1) * w[0:1]
    for t in range(1, taps):
        s = taps - 1 - t
        y = y + (_shift_rows(u, prev8, s) if s else u) * w[t:t + 1]
    return y + b


def _ssd_tile(xs, bc, dtr, row_valid, causal, tri, sel, emat, dtb, alog):
    dt = _softplus(dtr + dtb)
    if row_valid is not None:
        dt = jnp.where(row_valid, dt, 0.0)
    a = -jnp.exp(alog)
    acum = _dot3_left(tri, dt * a)
    alast = _dot3_left(sel, acum)
    acx = _dot3_right(acum, emat)
    acum_x = acx[:, :D_SSD]
    acum_x2 = acx[:, D_SSD:]
    e1 = emat[:, :D_SSD]
    dt_x = _dot3_right(dt, e1)
    alast_x = _dot3_right(alast, e1)
    acum_t = acum.T

    xdt = xs * dt_x
    xdt_bf = xdt.astype(BF16)
    xdtd_bf = (xdt * jnp.exp(alast_x - acum_x)).astype(BF16)
    bm = bc[:, :SSD_GROUPS * SSD_STATE]
    cm = bc[:, SSD_GROUPS * SSD_STATE:]
    lane = lax.broadcasted_iota(jnp.int32, (SSD_CHUNK, LANES), 1)
    first_head = lane < SSD_HEAD_DIM

    pieces = []
    heads_per_group = SSD_HEADS // SSD_GROUPS
    for g in range(SSD_GROUPS):
        bg = bm[:, g * SSD_STATE:(g + 1) * SSD_STATE].astype(BF16)
        cg = cm[:, g * SSD_STATE:(g + 1) * SSD_STATE].astype(BF16)
        cb = _dot_nt(cg, bg)
        for pair in range(heads_per_group // 2):
            h0 = g * heads_per_group + 2 * pair
            ys = []
            for hh in (h0, h0 + 1):
                seg = acum_x2[:, hh * LANES:(hh + 1) * LANES] - acum_t[hh:hh + 1, :]
                decay = jnp.where(causal, jnp.exp(jnp.where(causal, seg, 0.0)), 0.0)
                sc = (cb * decay).astype(BF16)
                ys.append(_dot(sc, xdt_bf[:, (h0 // 2) * LANES:(h0 // 2 + 1) * LANES]))
            pieces.append(jnp.where(first_head, ys[0], ys[1]))
    y_diag = jnp.concatenate(pieces, axis=1)
    return y_diag, xdtd_bf, acum_x, bm, cm


def _ssd_finish(y, xs, z, dskip, gnorm):
    y = y + dskip * xs
    y = y * _silu(z)
    half = D_SSD // SSD_GROUPS
    outs = []
    for g in range(SSD_GROUPS):
        outs.append(_rms(y[:, g * half:(g + 1) * half], gnorm[:, g * half:(g + 1) * half]))
    return jnp.concatenate(outs, axis=1)


def _ssd_prompt_kernel(z_ref, xs_ref, bc_ref, dt_ref, cw_ref, cb_ref, dtb_ref, alog_ref, dskip_ref, gnorm_ref,
                       tri_ref, sel_ref, emat_ref, y_ref, st_ref, ht_ref, prev_xs, prev_bc):
    c = pl.program_id(1)
    rows = SSD_CHUNK

    @pl.when(c == 0)
    def _():
        ht_ref[...] = jnp.zeros_like(ht_ref)
        prev_xs[...] = jnp.zeros_like(prev_xs)
        prev_bc[...] = jnp.zeros_like(prev_bc)

    cw = cw_ref[...]
    cb = cb_ref[...]
    r = lax.broadcasted_iota(jnp.int32, (rows, rows), 0)
    s = lax.broadcasted_iota(jnp.int32, (rows, rows), 1)
    causal = s <= r
    half = D_SSD // SSD_GROUPS
    px = prev_xs[...]
    pb = prev_bc[...]
    ht_new = ht_ref[...]
    for sub in range(xs_ref.shape[1] // rows):
        rs = slice(sub * rows, (sub + 1) * rows)
        xr = xs_ref[0, rs, :]
        br = bc_ref[0, rs, :]
        xs = _silu(_conv_rows(xr, px, cw[:, :D_SSD], cb[:, :D_SSD]))
        bc = _silu(_conv_rows(br, pb, cw[:, D_SSD:], cb[:, D_SSD:]))
        px = xr[rows - 8:, :]
        pb = br[rows - 8:, :]
        y_diag, xdtd_bf, acum_x, bm, cm = _ssd_tile(xs, bc, dt_ref[0, rs, :], None, causal, tri_ref[...], sel_ref[...],
                                                     emat_ref[...], dtb_ref[...], alog_ref[...])
        ht = ht_new
        y_off, states = [], []
        for g in range(SSD_GROUPS):
            bg = bm[:, g * SSD_STATE:(g + 1) * SSD_STATE]
            cg = cm[:, g * SSD_STATE:(g + 1) * SSD_STATE].astype(BF16)
            y_off.append(_dot(cg, ht[:, g * half:(g + 1) * half].astype(BF16)))
            states.append(_dot(bg.T.astype(BF16), xdtd_bf[:, g * half:(g + 1) * half]))
        y = y_diag + jnp.concatenate(y_off, axis=1) * jnp.exp(acum_x)
        ht_new = jnp.exp(acum_x[rows - 1:rows, :]) * ht + jnp.concatenate(states, axis=1)
        y_ref[0, rs, :] = _ssd_finish(y, xs, z_ref[0, rs, :], dskip_ref[...], gnorm_ref[...]).astype(BF16)
    prev_xs[...] = px
    prev_bc[...] = pb
    ht_ref[...] = ht_new

    @pl.when(c == pl.num_programs(1) - 1)
    def _():
        for k in range(D_SSD // LANES):
            st_ref[0, k * LANES:(k + 1) * LANES, :] = ht_new[:, k * LANES:(k + 1) * LANES].T


def _const_spec(shape):
    nd = len(shape)
    return pl.BlockSpec(shape, lambda *_: (0,) * nd)


def _ssd_prompt(zx3, tail3, cw, cb, dtb, alog, dskip, gnorm, tri, sel, emat):
    nb, seq, _ = zx3.shape
    step = SSD_STEP_CHUNKS * SSD_CHUNK
    nc = seq // step
    return pl.pallas_call(
        _ssd_prompt_kernel,
        grid=(nb, nc),
        in_specs=[pl.BlockSpec((1, step, D_SSD), lambda b, c: (b, c, 0)),
                  pl.BlockSpec((1, step, D_SSD), lambda b, c: (b, c, 1)),
                  pl.BlockSpec((1, step, BC_DIM), lambda b, c: (b, c, 0)),
                  pl.BlockSpec((1, step, DT_PAD), lambda b, c: (b, c, BC_DIM // DT_PAD)),
                  _const_spec(cw.shape), _const_spec(cb.shape), _const_spec(dtb.shape), _const_spec(alog.shape),
                  _const_spec(dskip.shape), _const_spec(gnorm.shape), _const_spec(tri.shape),
                  _const_spec(sel.shape), _const_spec(emat.shape)],
        out_specs=[pl.BlockSpec((1, step, D_SSD), lambda b, c: (b, c, 0)),
                   pl.BlockSpec((1, D_SSD, SSD_STATE), lambda b, c: (b, 0, 0))],
        out_shape=[jax.ShapeDtypeStruct((nb, seq, D_SSD), BF16),
                   jax.ShapeDtypeStruct((nb, D_SSD, SSD_STATE), F32)],
        scratch_shapes=[pltpu.VMEM((SSD_STATE, D_SSD), F32),
                        pltpu.VMEM((8, D_SSD), F32),
                        pltpu.VMEM((8, BC_DIM), F32)],
        compiler_params=pltpu.CompilerParams(dimension_semantics=("arbitrary", "arbitrary"),
                                             vmem_limit_bytes=VMEM_LIMIT),
        name="ssd_prompt",
    )(zx3, zx3, tail3, tail3, cw, cb, dtb, alog, dskip, gnorm, tri, sel, emat)


def _ssd_sample_kernel(z_ref, xs_ref, bc_ref, dt_ref, ctx_ref, h0_ref, cw_ref, cb_ref, dtb_ref, alog_ref, dskip_ref,
                       gnorm_ref, tri_ref, sel_ref, emat_ref, y_ref, st_ref,
                       ybase, yoff, xs_s, xdtd_s, acumx_s, bm_s, cm_s):
    s_id = pl.program_id(1)
    rows = SSD_CHUNK
    nseq = rows // SAMPLE_ROWS
    rr = lax.broadcasted_iota(jnp.int32, (rows, 1), 0)

    @pl.when(s_id == 0)
    def _():
        ctx = ctx_ref[0]
        is_ctx = (rr % SAMPLE_ROWS) >= SAMPLE_ROWS - (SSD_CONV - 1)
        xr = jnp.where(is_ctx, ctx[8:, :D_SSD], xs_ref[...])
        br = jnp.where(is_ctx, ctx[8:, D_SSD:], bc_ref[...])
        cw = cw_ref[...]
        cb = cb_ref[...]
        xs = _silu(_conv_rows(xr, ctx[0:8, :D_SSD], cw[:, :D_SSD], cb[:, :D_SSD]))
        bc = _silu(_conv_rows(br, ctx[0:8, D_SSD:], cw[:, D_SSD:], cb[:, D_SSD:]))
        r = lax.broadcasted_iota(jnp.int32, (rows, rows), 0)
        c = lax.broadcasted_iota(jnp.int32, (rows, rows), 1)
        causal = (c <= r) & ((c // SAMPLE_ROWS) == (r // SAMPLE_ROWS))
        row_valid = (rr % SAMPLE_ROWS) < (SAMPLE_ROWS // 2)
        y_diag, xdtd_bf, acum_x, bm, cm = _ssd_tile(xs, bc, dt_ref[...], row_valid, causal, tri_ref[...],
                                                     sel_ref[...], emat_ref[...], dtb_ref[...], alog_ref[...])
        ybase[...] = y_diag
        yoff[...] = jnp.zeros_like(yoff)
        xs_s[...] = xs
        xdtd_s[...] = xdtd_bf
        acumx_s[...] = acum_x
        bm_s[...] = bm
        cm_s[...] = cm

    in_seq_rows = (rr // SAMPLE_ROWS) == s_id
    in_seq_lanes = (lax.broadcasted_iota(jnp.int32, (1, rows), 1) // SAMPLE_ROWS) == s_id
    h0 = h0_ref[0]
    ht = jnp.concatenate([h0[k * LANES:(k + 1) * LANES, :].T for k in range(D_SSD // LANES)], axis=1)
    half = D_SSD // SSD_GROUPS
    y_off, states = [], []
    for g in range(SSD_GROUPS):
        bg_t = jnp.where(in_seq_lanes, bm_s[:, g * SSD_STATE:(g + 1) * SSD_STATE].T, 0.0).astype(BF16)
        cg = cm_s[:, g * SSD_STATE:(g + 1) * SSD_STATE].astype(BF16)
        y_off.append(_dot(cg, ht[:, g * half:(g + 1) * half].astype(BF16)))
        states.append(_dot(bg_t, xdtd_s[:, g * half:(g + 1) * half]))
    yoff[...] = yoff[...] + jnp.where(in_seq_rows, jnp.concatenate(y_off, axis=1), 0.0)
    last = pl.multiple_of(s_id * SAMPLE_ROWS, SAMPLE_ROWS) + (SAMPLE_ROWS - 1)
    ht_new = jnp.exp(acumx_s[pl.ds(last, 1), :]) * ht + jnp.concatenate(states, axis=1)
    for k in range(D_SSD // LANES):
        st_ref[0, k * LANES:(k + 1) * LANES, :] = ht_new[:, k * LANES:(k + 1) * LANES].T

    @pl.when(s_id == nseq - 1)
    def _():
        y = ybase[...] + yoff[...] * jnp.exp(acumx_s[...])
        y_ref[...] = _ssd_finish(y, xs_s[...], z_ref[...], dskip_ref[...], gnorm_ref[...]).astype(BF16)


def _ssd_sample(zx, tail, ctx, h0, cw, cb, dtb, alog, dskip, gnorm, tri, sel, emat):
    m = zx.shape[0]
    nt = m // SSD_CHUNK
    nseq = SSD_CHUNK // SAMPLE_ROWS
    return pl.pallas_call(
        _ssd_sample_kernel,
        grid=(nt, nseq),
        in_specs=[pl.BlockSpec((SSD_CHUNK, D_SSD), lambda t, s: (t, 0)),
                  pl.BlockSpec((SSD_CHUNK, D_SSD), lambda t, s: (t, 1)),
                  pl.BlockSpec((SSD_CHUNK, BC_DIM), lambda t, s: (t, 0)),
                  pl.BlockSpec((SSD_CHUNK, DT_PAD), lambda t, s: (t, BC_DIM // DT_PAD)),
                  pl.BlockSpec((1, SSD_CHUNK + 8, D_SSD + BC_DIM), lambda t, s: (t, 0, 0)),
                  pl.BlockSpec((1, D_SSD, SSD_STATE), lambda t, s: (t * nseq + s, 0, 0)),
                  _const_spec(cw.shape), _const_spec(cb.shape), _const_spec(dtb.shape), _const_spec(alog.shape),
                  _const_spec(dskip.shape), _const_spec(gnorm.shape), _const_spec(tri.shape),
                  _const_spec(sel.shape), _const_spec(emat.shape)],
        out_specs=[pl.BlockSpec((SSD_CHUNK, D_SSD), lambda t, s: (t, 0)),
                   pl.BlockSpec((1, D_SSD, SSD_STATE), lambda t, s: (t * nseq + s, 0, 0))],
        out_shape=[jax.ShapeDtypeStruct((m, D_SSD), BF16),
                   jax.ShapeDtypeStruct(h0.shape, F32)],
        scratch_shapes=[pltpu.VMEM((SSD_CHUNK, D_SSD), F32),
                        pltpu.VMEM((SSD_CHUNK, D_SSD), F32),
                        pltpu.VMEM((SSD_CHUNK, D_SSD), F32),
                        pltpu.VMEM((SSD_CHUNK, D_SSD), BF16),
                        pltpu.VMEM((SSD_CHUNK, D_SSD), F32),
                        pltpu.VMEM((SSD_CHUNK, SSD_GROUPS * SSD_STATE), F32),
                        pltpu.VMEM((SSD_CHUNK, SSD_GROUPS * SSD_STATE), F32)],
        compiler_params=pltpu.CompilerParams(dimension_semantics=("arbitrary", "arbitrary"),
                                             vmem_limit_bytes=VMEM_LIMIT),
        name="ssd_sample",
    )(zx, zx, tail, tail, ctx, h0, cw, cb, dtb, alog, dskip, gnorm, tri, sel, emat)


def _sb_block(qb, kb, vb, bias, uo, run, mask, nearest_first):
    z = _dot_nt(qb, kb) * ATT_SCALE + bias
    nz = -z
    lg = jnp.log(1.0 + jnp.exp(jnp.minimum(z, nz)))
    lm = jnp.minimum(nz, 0.0) - lg
    ls_pos = lm + z
    if mask is not None:
        lm = jnp.where(mask, lm, 0.0)
    hi = lm.astype(BF16)
    lo = (lm - hi.astype(F32)).astype(BF16)
    nsub = z.shape[1] // LANES
    tails = [None] * nsub
    for c in (range(nsub) if nearest_first else reversed(range(nsub))):
        cols = slice(c * LANES, (c + 1) * LANES)
        st = _dot(jnp.concatenate([hi[:, cols], lo[:, cols]], axis=1), uo)
        tails[c] = run + st[:, :LANES]
        run = run + st[:, LANES:]
    w = jnp.exp(ls_pos + jnp.concatenate(tails, axis=1))
    if mask is not None:
        w = jnp.where(mask, w, 0.0)
    return _dot(w.astype(BF16), vb), run


ATT_TQ = 512
ATT_TK = ATT_TQ
ATT_HEADS_PER_STEP = 1


def _attn_prompt_kernel(bias_ref, q_ref, k_ref, v_ref, uo_ref, *refs):
    nw = (len(refs) - 1) // 2
    for w32_ref, wbf_ref in zip(refs[:nw], refs[nw + 1:]):
        wbf_ref[...] = w32_ref[...].astype(BF16)
    o_ref = refs[nw]
    hg = pl.program_id(1)
    qi = pl.program_id(2)
    uo = uo_ref[...]
    heads = range(ATT_HEADS_PER_STEP)
    cols = [slice(h * ATT_HEAD_DIM, (h + 1) * ATT_HEAD_DIM) for h in heads]
    qbs = [q_ref[0, :, cols[h]] for h in heads]
    biases = [bias_ref[hg * ATT_HEADS_PER_STEP + h] for h in heads]
    q0 = pl.multiple_of(qi * ATT_TQ, ATT_TQ)

    def keys_values(h, start, n):
        return k_ref[0, pl.ds(start, n), cols[h]], v_ref[0, pl.ds(start, n), cols[h]]

    def diagonal(h):
        accs, runs = [], []
        for r in range(ATT_TQ // LANES):
            nk = (r + 1) * LANES
            row = lax.broadcasted_iota(jnp.int32, (LANES, nk), 0) + r * LANES
            col = lax.broadcasted_iota(jnp.int32, (LANES, nk), 1)
            kb, vb = keys_values(h, q0, nk)
            a, rn = _sb_block(qbs[h][r * LANES:(r + 1) * LANES], kb, vb, biases[h], uo,
                              jnp.zeros((LANES, LANES), F32), col < row, False)
            accs.append(a)
            runs.append(rn)
        return jnp.concatenate(accs, axis=0), jnp.concatenate(runs, axis=0)

    def block(kb_index, carry):
        start = pl.multiple_of(kb_index * ATT_TK, ATT_TK)
        out = []
        for h in heads:
            acc, run = carry[2 * h], carry[2 * h + 1]
            kb, vb = keys_values(h, start, ATT_TK)
            pv, run = _sb_block(qbs[h], kb, vb, biases[h], uo, run, None, False)
            out += [acc + pv, run]
        return tuple(out)

    def store(carry):
        for h in heads:
            o_ref[0, :, cols[h]] = carry[2 * h].astype(BF16)

    @pl.when(qi == 0)
    def _():
        store(sum((diagonal(h) for h in heads), ()))

    @pl.when(qi > 0)
    def _():
        carry = block(qi - 1, sum((diagonal(h) for h in heads), ()))
        store(lax.fori_loop(0, qi - 1, lambda i, c: block(qi - 2 - i, c), carry))


def _attn_prompt(qkv3, sb_bias, uo, weights):
    nb, seq, _ = qkv3.shape
    tq = ATT_TQ
    width = ATT_HEADS_PER_STEP * ATT_HEAD_DIM
    groups = ATT_HEADS // ATT_HEADS_PER_STEP
    nq = seq // tq
    steps = nb * groups * nq
    w_specs = []
    for w in weights:
        r, c = w.shape
        if r % (16 * steps) == 0:
            w_specs.append(pl.BlockSpec((r // steps, c), lambda b, g, i: ((b * groups + g) * nq + i, 0)))
        else:
            cb = c // LANES // 2
            rb = steps // cb
            assert r % (16 * rb) == 0 and rb * cb == steps, (w.shape, steps)
            w_specs.append(pl.BlockSpec((r // rb, c // cb),
                                        lambda b, g, i, cb=cb: (((b * groups + g) * nq + i) // cb,
                                                                ((b * groups + g) * nq + i) % cb)))
    return pl.pallas_call(
        _attn_prompt_kernel,
        grid=(nb, groups, nq),
        in_specs=[pl.BlockSpec(memory_space=pltpu.SMEM),
                  pl.BlockSpec((1, tq, width), lambda b, g, i: (b, i, g)),
                  pl.BlockSpec((1, seq, width), lambda b, g, i: (b, 0, groups + g)),
                  pl.BlockSpec((1, seq, width), lambda b, g, i: (b, 0, 2 * groups + g)),
                  _const_spec(uo.shape)] + w_specs,
        out_specs=[pl.BlockSpec((1, tq, width), lambda b, g, i: (b, i, g))] + w_specs,
        out_shape=[jax.ShapeDtypeStruct((nb, seq, D_ATT), BF16)]
                  + [jax.ShapeDtypeStruct(w.shape, BF16) for w in weights],
        compiler_params=pltpu.CompilerParams(dimension_semantics=("arbitrary", "arbitrary", "arbitrary"),
                                             vmem_limit_bytes=VMEM_LIMIT),
        name="attn_prompt",
    )(sb_bias, qkv3, qkv3, qkv3, uo, *weights)


PAGES_PER_STEP = 8
PAGE_SLOTS = 3


def _page_by_head(ref):
    heads = [ref[pl.ds(h, PAGE_SIZE, stride=ATT_HEADS), :] for h in range(ATT_HEADS)]
    return jnp.concatenate(heads, axis=1).astype(BF16)


def _attn_sample_kernel(nseq, steps, pt_ref, q_ref, kn_ref, vn_ref, bias_ref, uo_ref, ck_hbm, cv_hbm, o_ref,
                        qbd, knew, vnew, acc, run, kbuf, vbuf, sems):
    b = pl.program_id(0)
    j = pl.program_id(1)
    total = nseq * steps
    t = b * steps + j
    nrow = ATT_HEADS * SAMPLE_ROWS
    bias = bias_ref[...]
    uo = uo_ref[...]

    def page_copies(step):
        sb = step // steps
        sj = step - sb * steps
        slot = step % PAGE_SLOTS
        copies = []
        for g in range(PAGES_PER_STEP):
            page = pt_ref[sb, steps * PAGES_PER_STEP - 1 - (sj * PAGES_PER_STEP + g)]
            copies.append(pltpu.make_async_copy(ck_hbm.at[page], kbuf.at[slot, g], sems.at[0, slot]))
            copies.append(pltpu.make_async_copy(cv_hbm.at[page], vbuf.at[slot, g], sems.at[1, slot]))
        return copies

    @pl.when(t == 0)
    def _():
        for step in range(PAGE_SLOTS - 1):
            for c in page_copies(step):
                c.start()

    @pl.when(t + (PAGE_SLOTS - 1) < total)
    def _():
        for c in page_copies(t + (PAGE_SLOTS - 1)):
            c.start()

    @pl.when((b == 0) & (j == 0))
    def _():
        knew[...] = jnp.zeros_like(knew)
        vnew[...] = jnp.zeros_like(vnew)

    @pl.when(j == 0)
    def _():
        q8 = q_ref[0]
        rr = lax.broadcasted_iota(jnp.int32, (nrow, D_ATT), 0)
        cc = lax.broadcasted_iota(jnp.int32, (nrow, D_ATT), 1)
        qt = jnp.concatenate([q8] * ATT_HEADS, axis=0)
        qbd[...] = jnp.where((rr // SAMPLE_ROWS) == (cc // ATT_HEAD_DIM), qt, 0.0).astype(BF16)
        knew[0:SAMPLE_ROWS, :] = kn_ref[0]
        vnew[0:SAMPLE_ROWS, :] = vn_ref[0]
        qq = lax.broadcasted_iota(jnp.int32, (nrow, LANES), 0) % SAMPLE_ROWS
        kk = lax.broadcasted_iota(jnp.int32, (nrow, LANES), 1)
        mask = (kk < qq) & (kk < SAMPLE_ROWS // 2)
        pv, r = _sb_block(qbd[...], knew[...].astype(BF16), vnew[...].astype(BF16), bias, uo,
                          jnp.zeros((nrow, LANES), F32), mask, True)
        acc[...] = pv
        run[...] = r

    for c in page_copies(t):
        c.wait()
    slot = t % PAGE_SLOTS
    kcat = jnp.concatenate([_page_by_head(kbuf.at[slot, g]) for g in range(PAGES_PER_STEP)], axis=0)
    vcat = jnp.concatenate([_page_by_head(vbuf.at[slot, g]) for g in range(PAGES_PER_STEP)], axis=0)
    pv, r = _sb_block(qbd[...], kcat, vcat, jnp.concatenate([bias] * PAGES_PER_STEP, axis=1), uo, run[...], None, True)
    a = acc[...] + pv
    acc[...] = a
    run[...] = r

    @pl.when(j == steps - 1)
    def _():
        for h in range(ATT_HEADS):
            o_ref[0, :, h * ATT_HEAD_DIM:(h + 1) * ATT_HEAD_DIM] = (
                a[h * SAMPLE_ROWS:(h + 1) * SAMPLE_ROWS, h * ATT_HEAD_DIM:(h + 1) * ATT_HEAD_DIM])


def _attn_sample(page_table, projs3, cache_k, cache_v, bias_rows, uo):
    nseq, npages = page_table.shape
    steps = npages // PAGES_PER_STEP
    assert steps * PAGES_PER_STEP == npages and nseq * steps >= PAGE_SLOTS - 1
    page_buf = pltpu.VMEM((PAGE_SLOTS, PAGES_PER_STEP, PAGE_SIZE * ATT_HEADS, ATT_HEAD_DIM), F32)

    grid_spec = pltpu.PrefetchScalarGridSpec(
        num_scalar_prefetch=1,
        grid=(nseq, steps),
        in_specs=[pl.BlockSpec((1, SAMPLE_ROWS, D_ATT), lambda b, j, pt: (b, 0, 0)),
                  pl.BlockSpec((1, SAMPLE_ROWS, D_ATT), lambda b, j, pt: (b, 0, 1)),
                  pl.BlockSpec((1, SAMPLE_ROWS, D_ATT), lambda b, j, pt: (b, 0, 2)),
                  pl.BlockSpec(bias_rows.shape, lambda b, j, pt: (0, 0)),
                  pl.BlockSpec(uo.shape, lambda b, j, pt: (0, 0)),
                  pl.BlockSpec(memory_space=pl.ANY), pl.BlockSpec(memory_space=pl.ANY)],
        out_specs=pl.BlockSpec((1, SAMPLE_ROWS, D_ATT), lambda b, j, pt: (b, 0, 0)),
        scratch_shapes=[pltpu.VMEM((ATT_HEADS * SAMPLE_ROWS, D_ATT), BF16),
                        pltpu.VMEM((PAGE_SIZE, D_ATT), F32),
                        pltpu.VMEM((PAGE_SIZE, D_ATT), F32),
                        pltpu.VMEM((ATT_HEADS * SAMPLE_ROWS, D_ATT), F32),
                        pltpu.VMEM((ATT_HEADS * SAMPLE_ROWS, LANES), F32),
                        page_buf, page_buf, pltpu.SemaphoreType.DMA((2, PAGE_SLOTS))])
    return pl.pallas_call(
        functools.partial(_attn_sample_kernel, nseq, steps),
        grid_spec=grid_spec,
        out_shape=jax.ShapeDtypeStruct((nseq, SAMPLE_ROWS, D_ATT), F32),
        compiler_params=pltpu.CompilerParams(dimension_semantics=("arbitrary", "arbitrary"),
                                             vmem_limit_bytes=VMEM_LIMIT),
        name="attn_sample",
    )(page_table, projs3, projs3, projs3, bias_rows, uo, cache_k, cache_v)


def _outproj_kernel(y_ref, o_ref, x_ref, g1_ref, sh2_ref, sc2_ref, gpost_ref, gpre_ref, wy_ref, wo_ref, x1_ref, h2_ref):
    tm = x_ref.shape[0]
    chunk = min(tm, 256)
    for r in range(0, tm, chunk):
        rows = slice(r, r + chunk)
        g1, sh2, sc2 = (v if v.shape[0] == 1 else v[rows] for v in (g1_ref[0], sh2_ref[0], sc2_ref[0]))
        m = _dot(y_ref[rows, :], wy_ref[...]) + _dot(o_ref[rows, :].astype(BF16), wo_ref[...])
        x1 = x_ref[rows, :] + g1 * _rms(m, gpost_ref[...])
        x1_ref[rows, :] = x1
        h2 = _rms(x1, gpre_ref[...]) * (1.0 + sc2) + sh2
        h2_ref[rows, :] = h2.astype(BF16)


def _outproj(y, o, x, mods, gpost, gpre, w_out, tm, tiles_per_group):
    m = x.shape[0]
    rows = mods.shape[1]
    half = w_out.shape[0] // 2
    row = lambda i: (i, 0)
    return pl.pallas_call(
        _outproj_kernel,
        grid=(m // tm,),
        in_specs=[pl.BlockSpec((tm, D_SSD), row), pl.BlockSpec((tm, D_ATT), row), pl.BlockSpec((tm, D_MODEL), row),
                  _mod_spec(2, rows, tiles_per_group), _mod_spec(3, rows, tiles_per_group),
                  _mod_spec(4, rows, tiles_per_group),
                  _const_spec(gpost.shape), _const_spec(gpre.shape),
                  pl.BlockSpec((half, D_MODEL), lambda i: (0, 0)),
                  pl.BlockSpec((half, D_MODEL), lambda i: (1, 0))],
        out_specs=[pl.BlockSpec((tm, D_MODEL), row), pl.BlockSpec((tm, D_MODEL), row)],
        out_shape=[jax.ShapeDtypeStruct((m, D_MODEL), F32), jax.ShapeDtypeStruct((m, D_MODEL), BF16)],
        compiler_params=pltpu.CompilerParams(dimension_semantics=("arbitrary",), vmem_limit_bytes=VMEM_LIMIT),
        name="out_proj",
    )(y, o, x, mods, mods, mods, gpost, gpre, w_out, w_out)


FFN_CHUNK = 256


def _ffn_kernel(sample, tiles_per_group, *refs):
    if sample:
        (h_ref, x1_hbm, g2_ref, gpost_ref, wg_ref, wv_ref, wo_ref, cwg_ref, cwv_ref, cbg_ref, cbv_ref,
         injg_ref, injv_ref, y_ref, ug_ref, uv_ref, x1_buf, x1_sem) = refs
    else:
        (h_ref, x1_hbm, g2_ref, gpost_ref, wg_ref, wv_ref, wo_ref, cwg_ref, cwv_ref, cbg_ref, cbv_ref,
         y_ref, ug_ref, uv_ref, x1_buf, x1_sem, halog, halov) = refs
    i = pl.program_id(0)
    j = pl.program_id(1)
    tm = h_ref.shape[0]
    tf = wg_ref.shape[1]
    hn = h_ref[...]
    x1_copy = pltpu.make_async_copy(x1_hbm.at[pl.ds(pl.multiple_of(i * tm, tm), tm), :], x1_buf, x1_sem)

    @pl.when(j == 0)
    def _():
        x1_copy.start()
        y_ref[...] = jnp.zeros_like(y_ref)

    if sample:
        is_ctx = (lax.broadcasted_iota(jnp.int32, (tm, 1), 0) % SAMPLE_ROWS) >= SAMPLE_ROWS - (FFN_CONV - 1)
    else:
        @pl.when((i == 0) & (j == 0))
        def _():
            halog[...] = jnp.zeros_like(halog)
            halov[...] = jnp.zeros_like(halov)

        first = lax.rem(i, tiles_per_group) == 0

    def half(w_ref, cw_ref, cb_ref, u_ref, inj_ref, halo, cols):
        u = jnp.concatenate([_dot(hn[r:r + 128], w_ref[:, cols]) for r in range(0, tm, 128)], axis=0)
        if sample:
            u_ref[:, cols] = u
            prev = inj_ref[0:8, cols]
            u = jnp.where(is_ctx, inj_ref[8:, cols], u)
        else:
            prev = jnp.where(first, 0.0, halo[j, :, cols])
            halo[j, :, cols] = u[tm - 8:, :]
            u_ref[0, :, cols] = u[tm - 8:, :]
        return _conv_rows(u, prev, cw_ref[:, cols], cb_ref[:, cols])

    acts = []
    for c in range(tf // FFN_CHUNK):
        cols = slice(c * FFN_CHUNK, (c + 1) * FFN_CHUNK)
        gate = half(wg_ref, cwg_ref, cbg_ref, ug_ref, None if not sample else injg_ref, None if sample else halog, cols)
        val = half(wv_ref, cwv_ref, cbv_ref, uv_ref, None if not sample else injv_ref, None if sample else halov, cols)
        acts.append((_silu(gate) * val).astype(BF16))
    y_ref[...] += _dot(jnp.concatenate(acts, axis=1), wo_ref[...])

    @pl.when(j == pl.num_programs(1) - 1)
    def _():
        x1_copy.wait()
        chunk = min(tm, 256)
        for r in range(0, tm, chunk):
            rows = slice(r, r + chunk)
            g2 = g2_ref[0]
            g2 = g2 if g2.shape[0] == 1 else g2[rows]
            y_ref[rows, :] = x1_buf[rows, :] + g2 * _rms(y_ref[rows, :], gpost_ref[...])


def _ffn(h2, x1, mods, gpost, w_in, w_out, cw, cb, tm, tiles_per_group, inj=None):
    m = h2.shape[0]
    rows = mods.shape[1]
    tf = 512
    nf = D_FF // tf
    sample = inj is not None
    row = lambda i, j: (i, 0)
    gate_col = lambda i, j: (0, j)
    val_col = lambda i, j: (0, nf + j)
    in_specs = [pl.BlockSpec((tm, D_MODEL), row), pl.BlockSpec(memory_space=pl.ANY),
                _mod_spec(5, rows, tiles_per_group), _const_spec(gpost.shape),
                pl.BlockSpec((D_MODEL, tf), gate_col), pl.BlockSpec((D_MODEL, tf), val_col),
                pl.BlockSpec((tf, D_MODEL), lambda i, j: (j, 0)),
                pl.BlockSpec((FFN_CONV, tf), gate_col), pl.BlockSpec((FFN_CONV, tf), val_col),
                pl.BlockSpec((1, tf), gate_col), pl.BlockSpec((1, tf), val_col)]
    args = [h2, x1, mods, gpost, w_in, w_in, w_out, cw, cw, cb, cb]
    scratch = [pltpu.VMEM((tm, D_MODEL), F32), pltpu.SemaphoreType.DMA(())]
    if sample:
        in_specs += [pl.BlockSpec((m + 8, tf), gate_col), pl.BlockSpec((m + 8, tf), val_col)]
        args += [inj, inj]
        u_specs = [pl.BlockSpec((tm, tf), lambda i, j: (i, j))] * 2
        u_shapes = [jax.ShapeDtypeStruct((m, D_FF), F32)] * 2
    else:
        u_specs = [pl.BlockSpec((1, 8, tf), lambda i, j: (i, 0, j))] * 2
        u_shapes = [jax.ShapeDtypeStruct((m // tm, 8, D_FF), F32)] * 2
        scratch += [pltpu.VMEM((nf, 8, tf), F32), pltpu.VMEM((nf, 8, tf), F32)]
    return pl.pallas_call(
        functools.partial(_ffn_kernel, sample, tiles_per_group),
        grid=(m // tm, nf),
        in_specs=in_specs,
        out_specs=[pl.BlockSpec((tm, D_MODEL), row)] + u_specs,
        out_shape=[jax.ShapeDtypeStruct((m, D_MODEL), F32)] + u_shapes,
        scratch_shapes=scratch,
        compiler_params=pltpu.CompilerParams(dimension_semantics=("arbitrary", "arbitrary"),
                                             vmem_limit_bytes=VMEM_LIMIT),
        name="ffn_sample" if sample else "ffn_prompt",
    )(*args)


def _ssd_constants(block_rows):
    r = jnp.arange(SSD_CHUNK)[:, None]
    c = jnp.arange(SSD_CHUNK)[None, :]
    same = (r // block_rows) == (c // block_rows)
    tri = ((c <= r) & same).astype(BF16)
    sel = (c == (r // block_rows) * block_rows + block_rows - 1).astype(BF16)
    hrow = jnp.arange(LANES)[:, None]
    e1 = (hrow == jnp.arange(D_SSD)[None, :] // SSD_HEAD_DIM).astype(BF16)
    e2 = (hrow == jnp.arange(SSD_HEADS * LANES)[None, :] // LANES).astype(BF16)
    return tri, sel, jnp.concatenate([e1, e2], axis=1)


def _suffix_matrix():
    j = jnp.arange(LANES)[:, None]
    s = jnp.arange(LANES)[None, :]
    u = jnp.concatenate([(j > s).astype(BF16), jnp.ones((LANES, LANES), BF16)], axis=1)
    return jnp.concatenate([u, u], axis=0)


def _inject_rows(ctx, nctx):
    nseq, _, c = ctx.shape
    body = jnp.concatenate([ctx[1:], jnp.zeros((1, nctx, c), F32)], axis=0)
    body = jnp.pad(body, ((0, 0), (SAMPLE_ROWS - nctx, 0), (0, 0))).reshape(nseq * SAMPLE_ROWS, c)
    head = jnp.pad(ctx[0], ((8 - nctx, 0), (0, 0)))
    return jnp.concatenate([head, body], axis=0)


def kernel(x_prompt, x_sample, c_prompt, c_sample, cache_k, cache_v, page_table, state_ssm, state_conv, state_ffn_conv, w_ada, b_ada, g_pre_mix, g_post_mix, g_pre_ffn, g_post_ffn, w_in, conv_w, conv_b, dt_bias, a_log, d_skip, g_ssd_norm, sb_bias, w_out, w_ffn_in, ffn_conv_w, ffn_conv_b, w_ffn_out):
    assert w_ada.shape[0] == 1, "one layer"
    nb, seq, d = x_prompt.shape
    ns, ls, _ = x_sample.shape
    n_pool = cache_k.shape[1]
    mp = nb * seq
    ms = ns * SAMPLE_ROWS

    w = w_in[0]
    o_bc, o_q = 2 * D_SSD, 2 * D_SSD + BC_DIM + SSD_HEADS
    w_zx = w[:, :o_bc].astype(BF16)
    w_qkv = w[:, o_q:].astype(BF16)
    w_tail = jnp.pad(w[:, o_bc:o_q], ((0, 0), (0, DT_PAD - SSD_HEADS))).astype(BF16)
    row = lambda v: v.reshape(1, -1)
    dtb = jnp.pad(dt_bias[0], (0, DT_PAD - SSD_HEADS)).reshape(1, DT_PAD)
    alog = jnp.pad(a_log[0], (0, DT_PAD - SSD_HEADS)).reshape(1, DT_PAD)
    dskip = jnp.repeat(d_skip[0], SSD_HEAD_DIM).reshape(1, D_SSD)
    gnorm = row(g_ssd_norm[0])
    cw, cb = conv_w[0], row(conv_b[0])
    fcw, fcb = ffn_conv_w[0], row(ffn_conv_b[0])
    uo = _suffix_matrix()

    n_c = nb + ns
    c_all = jnp.concatenate([c_prompt, c_sample, jnp.zeros((-n_c % 16, d), F32)], axis=0)
    mod = _ada(c_all, w_ada[0], row(b_ada[0]))
    mods_p = mod[:nb].reshape(nb, 1, 6 * d)
    mods_s = jnp.repeat(mod[nb:n_c], SAMPLE_ROWS, axis=0).reshape(1, ms, 6 * d)

    tm_p = 512
    tpg_p = seq // tm_p
    xp = x_prompt.reshape(mp, d)
    hn_p, tail_p = _prenorm(xp, mods_p, row(g_pre_mix[0]), w_tail, tm_p, tpg_p)
    zx_p, qkv_p, k_p, v_p = _inproj(hn_p, w_zx, w_qkv, 1024, BF16)
    zx3 = zx_p.reshape(nb, seq, 2 * D_SSD)
    tail3 = tail_p.reshape(nb, seq, PROJ_TAIL)
    tri, sel, emat = _ssd_constants(SSD_CHUNK)
    y_ssd, ssm_p = _ssd_prompt(zx3, tail3, cw, cb, dtb, alog, dskip, gnorm, tri, sel, emat)
    o_att, w_out_b, w_ffn_in_b, w_ffn_out_b = _attn_prompt(qkv_p.reshape(nb, seq, 3 * D_ATT), sb_bias[0], uo,
                                                           [w_out[0], w_ffn_in[0], w_ffn_out[0]])
    tm_e = 512
    x1_p, h2_p = _outproj(y_ssd.reshape(mp, D_SSD), o_att.reshape(mp, D_ATT), xp, mods_p, row(g_post_mix[0]),
                          row(g_pre_ffn[0]), w_out_b, tm_e, seq // tm_e)
    tm_f = 1024
    yp, ctxg_p, ctxv_p = _ffn(h2_p, x1_p, mods_p, row(g_post_ffn[0]), w_ffn_in_b, w_ffn_out_b, fcw, fcb,
                              tm_f, seq // tm_f)

    xs_pad = jnp.pad(x_sample, ((0, 0), (0, SAMPLE_ROWS - ls), (0, 0))).reshape(ms, d)
    hn_s, tail_s = _prenorm(xs_pad, mods_s, row(g_pre_mix[0]), w_tail, ms, 1)
    zx_s, qkv_s, k_s, v_s = _inproj(hn_s, w_zx, w_qkv, ms, F32)
    seq_per_tile = SSD_CHUNK // SAMPLE_ROWS
    conv_ctx = state_conv[0].reshape(ns // seq_per_tile, seq_per_tile, SSD_CONV - 1, -1)
    ctx_inj = jax.vmap(lambda c: _inject_rows(c, SSD_CONV - 1))(conv_ctx)
    tri_s, sel_s, _ = _ssd_constants(SAMPLE_ROWS)
    y_ssd_s, ssm_s = _ssd_sample(zx_s, tail_s, ctx_inj, state_ssm[0].reshape(ns, D_SSD, SSD_STATE), cw, cb, dtb, alog,
                                 dskip, gnorm, tri_s, sel_s, emat)
    bias_rows = jnp.broadcast_to(jnp.repeat(sb_bias[0], SAMPLE_ROWS)[:, None], (ATT_HEADS * SAMPLE_ROWS, LANES))
    o_att_s = _attn_sample(page_table, qkv_s.reshape(ns, SAMPLE_ROWS, 3 * D_ATT),
                           cache_k[0].reshape(n_pool, PAGE_SIZE * ATT_HEADS, ATT_HEAD_DIM),
                           cache_v[0].reshape(n_pool, PAGE_SIZE * ATT_HEADS, ATT_HEAD_DIM), bias_rows, uo)
    x1_s, h2_s = _outproj(y_ssd_s, o_att_s.reshape(ms, D_ATT), xs_pad, mods_s, row(g_post_mix[0]),
                          row(g_pre_ffn[0]), w_out_b, ms, 1)
    inj_f = _inject_rows(state_ffn_conv[0], FFN_CONV - 1)
    ys_pad, ug_s, uv_s = _ffn(h2_s, x1_s, mods_s, row(g_post_ffn[0]), w_ffn_in_b, w_ffn_out_b, fcw, fcb,
                              ms, 1, inj=inj_f)

    heads = (ATT_HEADS, ATT_HEAD_DIM)
    k_p = k_p.reshape(1, nb, seq, *heads)
    v_p = v_p.reshape(1, nb, seq, *heads)
    tail = slice(seq - (SSD_CONV - 1), seq)
    conv_p = jnp.concatenate([zx3[:, tail, D_SSD:], tail3[:, tail, :BC_DIM]], axis=-1)[None]
    ffn_p = jnp.concatenate([ctxg_p, ctxv_p], axis=-1)[seq // tm_f - 1::seq // tm_f, 8 - (FFN_CONV - 1):][None]
    k_s = k_s.reshape(ns, SAMPLE_ROWS, *heads)[None, :, :ls]
    v_s = v_s.reshape(ns, SAMPLE_ROWS, *heads)[None, :, :ls]
    raw_s = jnp.concatenate([zx_s.reshape(ns, SAMPLE_ROWS, -1)[:, :ls, D_SSD:],
                             tail_s.reshape(ns, SAMPLE_ROWS, -1)[:, :ls, :BC_DIM]], axis=-1)
    conv_s = jnp.concatenate([state_conv[0], raw_s], axis=1)[:, -(SSD_CONV - 1):][None]
    u_s = jnp.concatenate([ug_s, uv_s], axis=-1).reshape(ns, SAMPLE_ROWS, 2 * D_FF)[:, :ls]
    ffn_s = jnp.concatenate([state_ffn_conv[0], u_s], axis=1)[:, -(FFN_CONV - 1):][None]
    return (yp.reshape(nb, seq, d), ys_pad.reshape(ns, SAMPLE_ROWS, d)[:, :ls],
            k_p, v_p, ssm_p.reshape(1, nb, SSD_HEADS, SSD_HEAD_DIM, SSD_STATE), conv_p, ffn_p,
            k_s, v_s, ssm_s.reshape(1, ns, SSD_HEADS, SSD_HEAD_DIM, SSD_STATE), conv_s, ffn_s)
```

```python
import functools

import jax
import jax.numpy as jnp
from jax import lax
from jax.experimental import pallas as pl
from jax.experimental.pallas import tpu as pltpu

F32 = jnp.float32
BF16 = jnp.bfloat16

D_MODEL = 2048
D_SSD = 1024
SSD_HEAD_DIM = 64
SSD_HEADS = 16
SSD_GROUPS = 2
SSD_STATE = 128
SSD_CONV = 4
SSD_CHUNK = 128
SSD_STEP_CHUNKS = 4
BC_DIM = 2 * SSD_GROUPS * SSD_STATE
D_ATT = 1024
ATT_HEAD_DIM = 128
ATT_HEADS = 8
D_FF = 5632
FFN_CONV = 3
PAGE_SIZE = 128
EPS = 1e-6
ATT_SCALE = ATT_HEAD_DIM ** -0.5

PROJ_TN = 1024
PROJ_MAIN = 5 * PROJ_TN
DT_PAD = 128
PROJ_TAIL = BC_DIM + DT_PAD

SAMPLE_ROWS = 8
LANES = 128
VMEM_LIMIT = 56 * 1024 * 1024


def _dot(a, b):
    return jnp.dot(a, b, preferred_element_type=F32)


def _dot_nt(a, b):
    return lax.dot_general(a, b, (((1,), (1,)), ((), ())), preferred_element_type=F32)


def _sigmoid(x):
    return 1.0 / (1.0 + jnp.exp(-x))


def _silu(x):
    return x * _sigmoid(x)


def _softplus(x):
    return jnp.maximum(x, 0.0) + jnp.log1p(jnp.exp(-jnp.abs(x)))


def _rms(x, g):
    ms = jnp.mean(x * x, axis=-1, keepdims=True)
    return x * lax.rsqrt(ms + EPS) * g


def _split3(x):
    h = x.astype(BF16)
    r = x - h.astype(F32)
    m = r.astype(BF16)
    l = (r - m.astype(F32)).astype(BF16)
    return h, m, l


def _dot3_right(x, mat):
    return _dot(jnp.concatenate(_split3(x), axis=1), jnp.concatenate([mat] * 3, axis=0))


def _dot3_left(mat, x):
    return _dot(jnp.concatenate([mat] * 3, axis=1), jnp.concatenate(_split3(x), axis=0))


def _ada_kernel(c_ref, w_ref, b_ref, o_ref):
    s = _silu(c_ref[...]).astype(BF16)
    o_ref[...] = _dot(s, w_ref[...].astype(BF16)) + b_ref[...]


def _ada(c_all, w_ada, b_ada):
    m, d = c_all.shape
    n = w_ada.shape[1]
    tn = 1024
    return pl.pallas_call(
        _ada_kernel,
        grid=(n // tn,),
        in_specs=[pl.BlockSpec((m, d), lambda j: (0, 0)),
                  pl.BlockSpec((d, tn), lambda j: (0, j)),
                  pl.BlockSpec((1, tn), lambda j: (0, j))],
        out_specs=pl.BlockSpec((m, tn), lambda j: (0, j)),
        out_shape=jax.ShapeDtypeStruct((m, n), F32),
        compiler_params=pltpu.CompilerParams(dimension_semantics=("arbitrary",), vmem_limit_bytes=VMEM_LIMIT),
        name="ada_mod",
    )(c_all, w_ada, b_ada)


def _mod_spec(idx, rows, tiles_per_group):
    return pl.BlockSpec((1, rows, D_MODEL), lambda i, *_: (i // tiles_per_group, 0, idx))


def _prenorm_kernel(x_ref, sh_ref, sc_ref, g_ref, wt_ref, hn_ref, tail_ref):
    tm = x_ref.shape[0]
    chunk = min(tm, 128)
    g = g_ref[...]
    for r in range(0, tm, chunk):
        sc, sh = sc_ref[0], sh_ref[0]
        if sc.shape[0] != 1:
            sc, sh = sc[r:r + chunk], sh[r:r + chunk]
        h = (_rms(x_ref[r:r + chunk, :], g) * (1.0 + sc) + sh).astype(BF16)
        hn_ref[r:r + chunk, :] = h
        tail_ref[r:r + chunk, :] = _dot(h, wt_ref[...])


def _prenorm(x, mods, g, wt, tm, tiles_per_group):
    m = x.shape[0]
    rows = mods.shape[1]
    return pl.pallas_call(
        _prenorm_kernel,
        grid=(m // tm,),
        in_specs=[pl.BlockSpec((tm, D_MODEL), lambda i: (i, 0)),
                  _mod_spec(0, rows, tiles_per_group),
                  _mod_spec(1, rows, tiles_per_group),
                  _const_spec(g.shape), _const_spec(wt.shape)],
        out_specs=[pl.BlockSpec((tm, D_MODEL), lambda i: (i, 0)), pl.BlockSpec((tm, PROJ_TAIL), lambda i: (i, 0))],
        out_shape=[jax.ShapeDtypeStruct((m, D_MODEL), BF16), jax.ShapeDtypeStruct((m, PROJ_TAIL), F32)],
        compiler_params=pltpu.CompilerParams(dimension_semantics=("arbitrary",), vmem_limit_bytes=VMEM_LIMIT),
        name="pre_norm",
    )(x, mods, mods, g, wt)


def _inproj_kernel(hn_ref, wzx_ref, w_ref, zx_ref, qkv_ref, k_ref, v_ref):
    j = pl.program_id(1)
    tm = hn_ref.shape[0]

    def project(state_ref):
        res = _dot(hn_ref[...], w_ref[...])
        qkv_ref[...] = res.astype(qkv_ref.dtype)
        if state_ref is not None:
            for h in range(ATT_HEADS):
                state_ref[pl.ds(h, tm, stride=ATT_HEADS), :] = res[:, h * ATT_HEAD_DIM:(h + 1) * ATT_HEAD_DIM]

    @pl.when(j < 2)
    def _():
        zx_ref[...] = _dot(hn_ref[...], wzx_ref[...])

    @pl.when(j == 2)
    def _():
        project(None)

    @pl.when(j == 3)
    def _():
        project(k_ref)

    @pl.when(j == 4)
    def _():
        project(v_ref)


def _inproj(hn, w_zx, w_qkv, tm, qkv_dtype):
    m = hn.shape[0]
    kv_spec = pl.BlockSpec((tm * ATT_HEADS, ATT_HEAD_DIM), lambda i, j: (i, 0))
    kv_shape = jax.ShapeDtypeStruct((m * ATT_HEADS, ATT_HEAD_DIM), F32)
    return pl.pallas_call(
        _inproj_kernel,
        grid=(m // tm, PROJ_MAIN // PROJ_TN),
        in_specs=[pl.BlockSpec((tm, D_MODEL), lambda i, j: (i, 0)),
                  pl.BlockSpec((D_MODEL, PROJ_TN), lambda i, j: (0, jnp.minimum(j, 1))),
                  pl.BlockSpec((D_MODEL, PROJ_TN), lambda i, j: (0, jnp.clip(j - 2, 0, 2)))],
        out_specs=[pl.BlockSpec((tm, PROJ_TN), lambda i, j: (i, jnp.minimum(j, 1))),
                   pl.BlockSpec((tm, PROJ_TN), lambda i, j: (i, jnp.clip(j - 2, 0, 2))),
                   kv_spec, kv_spec],
        out_shape=[jax.ShapeDtypeStruct((m, 2 * PROJ_TN), F32),
                   jax.ShapeDtypeStruct((m, 3 * PROJ_TN), qkv_dtype), kv_shape, kv_shape],
        compiler_params=pltpu.CompilerParams(dimension_semantics=("arbitrary", "arbitrary"),
                                             vmem_limit_bytes=VMEM_LIMIT),
        name="in_proj",
    )(hn, w_zx, w_qkv)


def _shift_rows(u, prev8, s):
    r = pltpu.roll(u, s, 0)
    head = jnp.where(lax.broadcasted_iota(jnp.int32, (8, 1), 0) < s, pltpu.roll(prev8, s, 0), r[:8])
    return jnp.concatenate([head, r[8:]], axis=0)


def _conv_rows(u, prev8, w, b):
    taps = w.shape[0]
    y = _shift_rows(u, prev8, taps - 1) * w[0:1]
    for t in range(1, taps):
        s = taps - 1 - t
        y = y + (_shift_rows(u, prev8, s) if s else u) * w[t:t + 1]
    return y + b


def _ssd_tile(xs, bc, dtr, row_valid, causal, tri, sel, emat, dtb, alog):
    dt = _softplus(dtr + dtb)
    if row_valid is not None:
        dt = jnp.where(row_valid, dt, 0.0)
    a = -jnp.exp(alog)
    acum = _dot3_left(tri, dt * a)
    alast = _dot3_left(sel, acum)
    acx = _dot3_right(acum, emat)
    acum_x = acx[:, :D_SSD]
    acum_x2 = acx[:, D_SSD:]
    e1 = emat[:, :D_SSD]
    dt_x = _dot3_right(dt, e1)
    alast_x = _dot3_right(alast, e1)
    acum_t = acum.T

    xdt = xs * dt_x
    xdt_bf = xdt.astype(BF16)
    xdtd_bf = (xdt * jnp.exp(alast_x - acum_x)).astype(BF16)
    bm = bc[:, :SSD_GROUPS * SSD_STATE]
    cm = bc[:, SSD_GROUPS * SSD_STATE:]
    lane = lax.broadcasted_iota(jnp.int32, (SSD_CHUNK, LANES), 1)
    first_head = lane < SSD_HEAD_DIM

    pieces = []
    heads_per_group = SSD_HEADS // SSD_GROUPS
    for g in range(SSD_GROUPS):
        bg = bm[:, g * SSD_STATE:(g + 1) * SSD_STATE].astype(BF16)
        cg = cm[:, g * SSD_STATE:(g + 1) * SSD_STATE].astype(BF16)
        cb = _dot_nt(cg, bg)
        for pair in range(heads_per_group // 2):
            h0 = g * heads_per_group + 2 * pair
            ys = []
            for hh in (h0, h0 + 1):
                seg = acum_x2[:, hh * LANES:(hh + 1) * LANES] - acum_t[hh:hh + 1, :]
                decay = jnp.where(causal, jnp.exp(jnp.where(causal, seg, 0.0)), 0.0)
                sc = (cb * decay).astype(BF16)
                ys.append(_dot(sc, xdt_bf[:, (h0 // 2) * LANES:(h0 // 2 + 1) * LANES]))
            pieces.append(jnp.where(first_head, ys[0], ys[1]))
    y_diag = jnp.concatenate(pieces, axis=1)
    return y_diag, xdtd_bf, acum_x, bm, cm


def _ssd_finish(y, xs, z, dskip, gnorm):
    y = y + dskip * xs
    y = y * _silu(z)
    half = D_SSD // SSD_GROUPS
    outs = []
    for g in range(SSD_GROUPS):
        outs.append(_rms(y[:, g * half:(g + 1) * half], gnorm[:, g * half:(g + 1) * half]))
    return jnp.concatenate(outs, axis=1)


def _ssd_prompt_kernel(z_ref, xs_ref, bc_ref, dt_ref, cw_ref, cb_ref, dtb_ref, alog_ref, dskip_ref, gnorm_ref,
                       tri_ref, sel_ref, emat_ref, y_ref, st_ref, ht_ref, prev_xs, prev_bc):
    c = pl.program_id(1)
    rows = SSD_CHUNK

    @pl.when(c == 0)
    def _():
        ht_ref[...] = jnp.zeros_like(ht_ref)
        prev_xs[...] = jnp.zeros_like(prev_xs)
        prev_bc[...] = jnp.zeros_like(prev_bc)

    cw = cw_ref[...]
    cb = cb_ref[...]
    r = lax.broadcasted_iota(jnp.int32, (rows, rows), 0)
    s = lax.broadcasted_iota(jnp.int32, (rows, rows), 1)
    causal = s <= r
    half = D_SSD // SSD_GROUPS
    px = prev_xs[...]
    pb = prev_bc[...]
    ht_new = ht_ref[...]
    for sub in range(xs_ref.shape[1] // rows):
        rs = slice(sub * rows, (sub + 1) * rows)
        xr = xs_ref[0, rs, :]
        br = bc_ref[0, rs, :]
        xs = _silu(_conv_rows(xr, px, cw[:, :D_SSD], cb[:, :D_SSD]))
        bc = _silu(_conv_rows(br, pb, cw[:, D_SSD:], cb[:, D_SSD:]))
        px = xr[rows - 8:, :]
        pb = br[rows - 8:, :]
        y_diag, xdtd_bf, acum_x, bm, cm = _ssd_tile(xs, bc, dt_ref[0, rs, :], None, causal, tri_ref[...], sel_ref[...],
                                                     emat_ref[...], dtb_ref[...], alog_ref[...])
        ht = ht_new
        y_off, states = [], []
        for g in range(SSD_GROUPS):
            bg = bm[:, g * SSD_STATE:(g + 1) * SSD_STATE]
            cg = cm[:, g * SSD_STATE:(g + 1) * SSD_STATE].astype(BF16)
            y_off.append(_dot(cg, ht[:, g * half:(g + 1) * half].astype(BF16)))
            states.append(_dot(bg.T.astype(BF16), xdtd_bf[:, g * half:(g + 1) * half]))
        y = y_diag + jnp.concatenate(y_off, axis=1) * jnp.exp(acum_x)
        ht_new = jnp.exp(acum_x[rows - 1:rows, :]) * ht + jnp.concatenate(states, axis=1)
        y_ref[0, rs, :] = _ssd_finish(y, xs, z_ref[0, rs, :], dskip_ref[...], gnorm_ref[...]).astype(BF16)
    prev_xs[...] = px
    prev_bc[...] = pb
    ht_ref[...] = ht_new

    @pl.when(c == pl.num_programs(1) - 1)
    def _():
        for k in range(D_SSD // LANES):
            st_ref[0, k * LANES:(k + 1) * LANES, :] = ht_new[:, k * LANES:(k + 1) * LANES].T


def _const_spec(shape):
    nd = len(shape)
    return pl.BlockSpec(shape, lambda *_: (0,) * nd)


def _ssd_prompt(zx3, tail3, cw, cb, dtb, alog, dskip, gnorm, tri, sel, emat):
    nb, seq, _ = zx3.shape
    step = SSD_STEP_CHUNKS * SSD_CHUNK
    nc = seq // step
    return pl.pallas_call(
        _ssd_prompt_kernel,
        grid=(nb, nc),
        in_specs=[pl.BlockSpec((1, step, D_SSD), lambda b, c: (b, c, 0)),
                  pl.BlockSpec((1, step, D_SSD), lambda b, c: (b, c, 1)),
                  pl.BlockSpec((1, step, BC_DIM), lambda b, c: (b, c, 0)),
                  pl.BlockSpec((1, step, DT_PAD), lambda b, c: (b, c, BC_DIM // DT_PAD)),
                  _const_spec(cw.shape), _const_spec(cb.shape), _const_spec(dtb.shape), _const_spec(alog.shape),
                  _const_spec(dskip.shape), _const_spec(gnorm.shape), _const_spec(tri.shape),
                  _const_spec(sel.shape), _const_spec(emat.shape)],
        out_specs=[pl.BlockSpec((1, step, D_SSD), lambda b, c: (b, c, 0)),
                   pl.BlockSpec((1, D_SSD, SSD_STATE), lambda b, c: (b, 0, 0))],
        out_shape=[jax.ShapeDtypeStruct((nb, seq, D_SSD), BF16),
                   jax.ShapeDtypeStruct((nb, D_SSD, SSD_STATE), F32)],
        scratch_shapes=[pltpu.VMEM((SSD_STATE, D_SSD), F32),
                        pltpu.VMEM((8, D_SSD), F32),
                        pltpu.VMEM((8, BC_DIM), F32)],
        compiler_params=pltpu.CompilerParams(dimension_semantics=("arbitrary", "arbitrary"),
                                             vmem_limit_bytes=VMEM_LIMIT),
        name="ssd_prompt",
    )(zx3, zx3, tail3, tail3, cw, cb, dtb, alog, dskip, gnorm, tri, sel, emat)


def _ssd_sample_kernel(z_ref, xs_ref, bc_ref, dt_ref, ctx_ref, h0_ref, cw_ref, cb_ref, dtb_ref, alog_ref, dskip_ref,
                       gnorm_ref, tri_ref, sel_ref, emat_ref, y_ref, st_ref,
                       ybase, yoff, xs_s, xdtd_s, acumx_s, bm_s, cm_s):
    s_id = pl.program_id(1)
    rows = SSD_CHUNK
    nseq = rows // SAMPLE_ROWS
    rr = lax.broadcasted_iota(jnp.int32, (rows, 1), 0)

    @pl.when(s_id == 0)
    def _():
        ctx = ctx_ref[0]
        is_ctx = (rr % SAMPLE_ROWS) >= SAMPLE_ROWS - (SSD_CONV - 1)
        xr = jnp.where(is_ctx, ctx[8:, :D_SSD], xs_ref[...])
        br = jnp.where(is_ctx, ctx[8:, D_SSD:], bc_ref[...])
        cw = cw_ref[...]
        cb = cb_ref[...]
        xs = _silu(_conv_rows(xr, ctx[0:8, :D_SSD], cw[:, :D_SSD], cb[:, :D_SSD]))
        bc = _silu(_conv_rows(br, ctx[0:8, D_SSD:], cw[:, D_SSD:], cb[:, D_SSD:]))
        r = lax.broadcasted_iota(jnp.int32, (rows, rows), 0)
        c = lax.broadcasted_iota(jnp.int32, (rows, rows), 1)
        causal = (c <= r) & ((c // SAMPLE_ROWS) == (r // SAMPLE_ROWS))
        row_valid = (rr % SAMPLE_ROWS) < (SAMPLE_ROWS // 2)
        y_diag, xdtd_bf, acum_x, bm, cm = _ssd_tile(xs, bc, dt_ref[...], row_valid, causal, tri_ref[...],
                                                     sel_ref[...], emat_ref[...], dtb_ref[...], alog_ref[...])
        ybase[...] = y_diag
        yoff[...] = jnp.zeros_like(yoff)
        xs_s[...] = xs
        xdtd_s[...] = xdtd_bf
        acumx_s[...] = acum_x
        bm_s[...] = bm
        cm_s[...] = cm

    in_seq_rows = (rr // SAMPLE_ROWS) == s_id
    in_seq_lanes = (lax.broadcasted_iota(jnp.int32, (1, rows), 1) // SAMPLE_ROWS) == s_id
    h0 = h0_ref[0]
    ht = jnp.concatenate([h0[k * LANES:(k + 1) * LANES, :].T for k in range(D_SSD // LANES)], axis=1)
    half = D_SSD // SSD_GROUPS
    y_off, states = [], []
    for g in range(SSD_GROUPS):
        bg_t = jnp.where(in_seq_lanes, bm_s[:, g * SSD_STATE:(g + 1) * SSD_STATE].T, 0.0).astype(BF16)
        cg = cm_s[:, g * SSD_STATE:(g + 1) * SSD_STATE].astype(BF16)
        y_off.append(_dot(cg, ht[:, g * half:(g + 1) * half].astype(BF16)))
        states.append(_dot(bg_t, xdtd_s[:, g * half:(g + 1) * half]))
    yoff[...] = yoff[...] + jnp.where(in_seq_rows, jnp.concatenate(y_off, axis=1), 0.0)
    last = pl.multiple_of(s_id * SAMPLE_ROWS, SAMPLE_ROWS) + (SAMPLE_ROWS - 1)
    ht_new = jnp.exp(acumx_s[pl.ds(last, 1), :]) * ht + jnp.concatenate(states, axis=1)
    for k in range(D_SSD // LANES):
        st_ref[0, k * LANES:(k + 1) * LANES, :] = ht_new[:, k * LANES:(k + 1) * LANES].T

    @pl.when(s_id == nseq - 1)
    def _():
        y = ybase[...] + yoff[...] * jnp.exp(acumx_s[...])
        y_ref[...] = _ssd_finish(y, xs_s[...], z_ref[...], dskip_ref[...], gnorm_ref[...]).astype(BF16)


def _ssd_sample(zx, tail, ctx, h0, cw, cb, dtb, alog, dskip, gnorm, tri, sel, emat):
    m = zx.shape[0]
    nt = m // SSD_CHUNK
    nseq = SSD_CHUNK // SAMPLE_ROWS
    return pl.pallas_call(
        _ssd_sample_kernel,
        grid=(nt, nseq),
        in_specs=[pl.BlockSpec((SSD_CHUNK, D_SSD), lambda t, s: (t, 0)),
                  pl.BlockSpec((SSD_CHUNK, D_SSD), lambda t, s: (t, 1)),
                  pl.BlockSpec((SSD_CHUNK, BC_DIM), lambda t, s: (t, 0)),
                  pl.BlockSpec((SSD_CHUNK, DT_PAD), lambda t, s: (t, BC_DIM // DT_PAD)),
                  pl.BlockSpec((1, SSD_CHUNK + 8, D_SSD + BC_DIM), lambda t, s: (t, 0, 0)),
                  pl.BlockSpec((1, D_SSD, SSD_STATE), lambda t, s: (t * nseq + s, 0, 0)),
                  _const_spec(cw.shape), _const_spec(cb.shape), _const_spec(dtb.shape), _const_spec(alog.shape),
                  _const_spec(dskip.shape), _const_spec(gnorm.shape), _const_spec(tri.shape),
                  _const_spec(sel.shape), _const_spec(emat.shape)],
        out_specs=[pl.BlockSpec((SSD_CHUNK, D_SSD), lambda t, s: (t, 0)),
                   pl.BlockSpec((1, D_SSD, SSD_STATE), lambda t, s: (t * nseq + s, 0, 0))],
        out_shape=[jax.ShapeDtypeStruct((m, D_SSD), BF16),
                   jax.ShapeDtypeStruct(h0.shape, F32)],
        scratch_shapes=[pltpu.VMEM((SSD_CHUNK, D_SSD), F32),
                        pltpu.VMEM((SSD_CHUNK, D_SSD), F32),
                        pltpu.VMEM((SSD_CHUNK, D_SSD), F32),
                        pltpu.VMEM((SSD_CHUNK, D_SSD), BF16),
                        pltpu.VMEM((SSD_CHUNK, D_SSD), F32),
                        pltpu.VMEM((SSD_CHUNK, SSD_GROUPS * SSD_STATE), F32),
                        pltpu.VMEM((SSD_CHUNK, SSD_GROUPS * SSD_STATE), F32)],
        compiler_params=pltpu.CompilerParams(dimension_semantics=("arbitrary", "arbitrary"),
                                             vmem_limit_bytes=VMEM_LIMIT),
        name="ssd_sample",
    )(zx, zx, tail, tail, ctx, h0, cw, cb, dtb, alog, dskip, gnorm, tri, sel, emat)


def _sb_block(qb, kb, vb, bias, uo, run, mask, nearest_first):
    z = _dot_nt(qb, kb) * ATT_SCALE + bias
    nz = -z
    lg = jnp.log(1.0 + jnp.exp(jnp.minimum(z, nz)))
    lm = jnp.minimum(nz, 0.0) - lg
    ls_pos = lm + z
    if mask is not None:
        lm = jnp.where(mask, lm, 0.0)
    hi = lm.astype(BF16)
    lo = (lm - hi.astype(F32)).astype(BF16)
    nsub = z.shape[1] // LANES
    tails = [None] * nsub
    for c in (range(nsub) if nearest_first else reversed(range(nsub))):
        cols = slice(c * LANES, (c + 1) * LANES)
        st = _dot(jnp.concatenate([hi[:, cols], lo[:, cols]], axis=1), uo)
        tails[c] = run + st[:, :LANES]
        run = run + st[:, LANES:]
    w = jnp.exp(ls_pos + jnp.concatenate(tails, axis=1))
    if mask is not None:
        w = jnp.where(mask, w, 0.0)
    return _dot(w.astype(BF16), vb), run


ATT_TQ = 512
ATT_TK = ATT_TQ
ATT_HEADS_PER_STEP = 1


def _attn_prompt_kernel(bias_ref, q_ref, k_ref, v_ref, uo_ref, *refs):
    nw = (len(refs) - 1) // 2
    o_ref = refs[nw]

    def convert_weights():
        for w32_ref, wbf_ref in zip(refs[:nw], refs[nw + 1:]):
            wbf_ref[...] = w32_ref[...].astype(BF16)

    hg = pl.program_id(1)
    qi = pl.program_id(2)
    uo = uo_ref[...]
    heads = range(ATT_HEADS_PER_STEP)
    cols = [slice(h * ATT_HEAD_DIM, (h + 1) * ATT_HEAD_DIM) for h in heads]
    qbs = [q_ref[0, :, cols[h]] for h in heads]
    biases = [bias_ref[hg * ATT_HEADS_PER_STEP + h] for h in heads]
    q0 = pl.multiple_of(qi * ATT_TQ, ATT_TQ)

    def keys_values(h, start, n):
        return k_ref[0, pl.ds(start, n), cols[h]], v_ref[0, pl.ds(start, n), cols[h]]

    def diagonal(h):
        accs, runs = [], []
        for r in range(ATT_TQ // LANES):
            nk = (r + 1) * LANES
            row = lax.broadcasted_iota(jnp.int32, (LANES, nk), 0) + r * LANES
            col = lax.broadcasted_iota(jnp.int32, (LANES, nk), 1)
            kb, vb = keys_values(h, q0, nk)
            a, rn = _sb_block(qbs[h][r * LANES:(r + 1) * LANES], kb, vb, biases[h], uo,
                              jnp.zeros((LANES, LANES), F32), col < row, False)
            accs.append(a)
            runs.append(rn)
        return jnp.concatenate(accs, axis=0), jnp.concatenate(runs, axis=0)

    def block(kb_index, carry):
        start = pl.multiple_of(kb_index * ATT_TK, ATT_TK)
        out = []
        for h in heads:
            acc, run = carry[2 * h], carry[2 * h + 1]
            kb, vb = keys_values(h, start, ATT_TK)
            pv, run = _sb_block(qbs[h], kb, vb, biases[h], uo, run, None, False)
            out += [acc + pv, run]
        return tuple(out)

    def store(carry):
        for h in heads:
            o_ref[0, :, cols[h]] = carry[2 * h].astype(BF16)

    @pl.when(qi == 0)
    def _():
        convert_weights()
        store(sum((diagonal(h) for h in heads), ()))

    @pl.when(qi > 0)
    def _():
        convert_weights()
        carry = block(qi - 1, sum((diagonal(h) for h in heads), ()))
        store(lax.fori_loop(0, qi - 1, lambda i, c: block(qi - 2 - i, c), carry))


def _attn_prompt(qkv3, sb_bias, uo, weights):
    nb, seq, _ = qkv3.shape
    tq = ATT_TQ
    width = ATT_HEADS_PER_STEP * ATT_HEAD_DIM
    groups = ATT_HEADS // ATT_HEADS_PER_STEP
    nq = seq // tq
    steps = nb * groups * nq
    w_specs = []
    for w in weights:
        r, c = w.shape
        if r % (16 * steps) == 0:
            w_specs.append(pl.BlockSpec((r // steps, c), lambda b, g, i: ((b * groups + g) * nq + i, 0)))
        else:
            cb = c // LANES // 2
            rb = steps // cb
            assert r % (16 * rb) == 0 and rb * cb == steps, (w.shape, steps)
            w_specs.append(pl.BlockSpec((r // rb, c // cb),
                                        lambda b, g, i, cb=cb: (((b * groups + g) * nq + i) // cb,
                                                                ((b * groups + g) * nq + i) % cb)))
    return pl.pallas_call(
        _attn_prompt_kernel,
        grid=(nb, groups, nq),
        in_specs=[pl.BlockSpec(memory_space=pltpu.SMEM),
                  pl.BlockSpec((1, tq, width), lambda b, g, i: (b, i, g)),
                  pl.BlockSpec((1, seq, width), lambda b, g, i: (b, 0, groups + g)),
                  pl.BlockSpec((1, seq, width), lambda b, g, i: (b, 0, 2 * groups + g)),
                  _const_spec(uo.shape)] + w_specs,
        out_specs=[pl.BlockSpec((1, tq, width), lambda b, g, i: (b, i, g))] + w_specs,
        out_shape=[jax.ShapeDtypeStruct((nb, seq, D_ATT), BF16)]
                  + [jax.ShapeDtypeStruct(w.shape, BF16) for w in weights],
        compiler_params=pltpu.CompilerParams(dimension_semantics=("arbitrary", "arbitrary", "arbitrary"),
                                             vmem_limit_bytes=VMEM_LIMIT),
        name="attn_prompt",
    )(sb_bias, qkv3, qkv3, qkv3, uo, *weights)


PAGES_PER_STEP = 8
PAGE_SLOTS = 3


def _page_by_head(ref):
    heads = [ref[pl.ds(h, PAGE_SIZE, stride=ATT_HEADS), :] for h in range(ATT_HEADS)]
    return jnp.concatenate(heads, axis=1).astype(BF16)


def _attn_sample_kernel(nseq, steps, pt_ref, q_ref, kn_ref, vn_ref, bias_ref, uo_ref, ck_hbm, cv_hbm, o_ref,
                        qbd, knew, vnew, acc, run, kbuf, vbuf, sems):
    b = pl.program_id(0)
    j = pl.program_id(1)
    total = nseq * steps
    t = b * steps + j
    nrow = ATT_HEADS * SAMPLE_ROWS
    bias = bias_ref[...]
    uo = uo_ref[...]

    def page_copies(step):
        sb = step // steps
        sj = step - sb * steps
        slot = step % PAGE_SLOTS
        copies = []
        for g in range(PAGES_PER_STEP):
            page = pt_ref[sb, steps * PAGES_PER_STEP - 1 - (sj * PAGES_PER_STEP + g)]
            copies.append(pltpu.make_async_copy(ck_hbm.at[page], kbuf.at[slot, g], sems.at[0, slot]))
            copies.append(pltpu.make_async_copy(cv_hbm.at[page], vbuf.at[slot, g], sems.at[1, slot]))
        return copies

    @pl.when(t == 0)
    def _():
        for step in range(PAGE_SLOTS - 1):
            for c in page_copies(step):
                c.start()

    @pl.when(t + (PAGE_SLOTS - 1) < total)
    def _():
        for c in page_copies(t + (PAGE_SLOTS - 1)):
            c.start()

    @pl.when((b == 0) & (j == 0))
    def _():
        knew[...] = jnp.zeros_like(knew)
        vnew[...] = jnp.zeros_like(vnew)

    @pl.when(j == 0)
    def _():
        q8 = q_ref[0]
        rr = lax.broadcasted_iota(jnp.int32, (nrow, D_ATT), 0)
        cc = lax.broadcasted_iota(jnp.int32, (nrow, D_ATT), 1)
        qt = jnp.concatenate([q8] * ATT_HEADS, axis=0)
        qbd[...] = jnp.where((rr // SAMPLE_ROWS) == (cc // ATT_HEAD_DIM), qt, 0.0).astype(BF16)
        knew[0:SAMPLE_ROWS, :] = kn_ref[0]
        vnew[0:SAMPLE_ROWS, :] = vn_ref[0]
        qq = lax.broadcasted_iota(jnp.int32, (nrow, LANES), 0) % SAMPLE_ROWS
        kk = lax.broadcasted_iota(jnp.int32, (nrow, LANES), 1)
        mask = (kk < qq) & (kk < SAMPLE_ROWS // 2)
        pv, r = _sb_block(qbd[...], knew[...].astype(BF16), vnew[...].astype(BF16), bias, uo,
                          jnp.zeros((nrow, LANES), F32), mask, True)
        acc[...] = pv
        run[...] = r

    for c in page_copies(t):
        c.wait()
    slot = t % PAGE_SLOTS
    kcat = jnp.concatenate([_page_by_head(kbuf.at[slot, g]) for g in range(PAGES_PER_STEP)], axis=0)
    vcat = jnp.concatenate([_page_by_head(vbuf.at[slot, g]) for g in range(PAGES_PER_STEP)], axis=0)
    pv, r = _sb_block(qbd[...], kcat, vcat, jnp.concatenate([bias] * PAGES_PER_STEP, axis=1), uo, run[...], None, True)
    a = acc[...] + pv
    acc[...] = a
    run[...] = r

    @pl.when(j == steps - 1)
    def _():
        for h in range(ATT_HEADS):
            o_ref[0, :, h * ATT_HEAD_DIM:(h + 1) * ATT_HEAD_DIM] = (
                a[h * SAMPLE_ROWS:(h + 1) * SAMPLE_ROWS, h * ATT_HEAD_DIM:(h + 1) * ATT_HEAD_DIM])


def _attn_sample(page_table, projs3, cache_k, cache_v, bias_rows, uo):
    nseq, npages = page_table.shape
    steps = npages // PAGES_PER_STEP
    assert steps * PAGES_PER_STEP == npages and nseq * steps >= PAGE_SLOTS - 1
    page_buf = pltpu.VMEM((PAGE_SLOTS, PAGES_PER_STEP, PAGE_SIZE * ATT_HEADS, ATT_HEAD_DIM), F32)

    grid_spec = pltpu.PrefetchScalarGridSpec(
        num_scalar_prefetch=1,
        grid=(nseq, steps),
        in_specs=[pl.BlockSpec((1, SAMPLE_ROWS, D_ATT), lambda b, j, pt: (b, 0, 0)),
                  pl.BlockSpec((1, SAMPLE_ROWS, D_ATT), lambda b, j, pt: (b, 0, 1)),
                  pl.BlockSpec((1, SAMPLE_ROWS, D_ATT), lambda b, j, pt: (b, 0, 2)),
                  pl.BlockSpec(bias_rows.shape, lambda b, j, pt: (0, 0)),
                  pl.BlockSpec(uo.shape, lambda b, j, pt: (0, 0)),
                  pl.BlockSpec(memory_space=pl.ANY), pl.BlockSpec(memory_space=pl.ANY)],
        out_specs=pl.BlockSpec((1, SAMPLE_ROWS, D_ATT), lambda b, j, pt: (b, 0, 0)),
        scratch_shapes=[pltpu.VMEM((ATT_HEADS * SAMPLE_ROWS, D_ATT), BF16),
                        pltpu.VMEM((PAGE_SIZE, D_ATT), F32),
                        pltpu.VMEM((PAGE_SIZE, D_ATT), F32),
                        pltpu.VMEM((ATT_HEADS * SAMPLE_ROWS, D_ATT), F32),
                        pltpu.VMEM((ATT_HEADS * SAMPLE_ROWS, LANES), F32),
                        page_buf, page_buf, pltpu.SemaphoreType.DMA((2, PAGE_SLOTS))])
    return pl.pallas_call(
        functools.partial(_attn_sample_kernel, nseq, steps),
        grid_spec=grid_spec,
        out_shape=jax.ShapeDtypeStruct((nseq, SAMPLE_ROWS, D_ATT), F32),
        compiler_params=pltpu.CompilerParams(dimension_semantics=("arbitrary", "arbitrary"),
                                             vmem_limit_bytes=VMEM_LIMIT),
        name="attn_sample",
    )(page_table, projs3, projs3, projs3, bias_rows, uo, cache_k, cache_v)


def _outproj_kernel(y_ref, o_ref, x_ref, g1_ref, sh2_ref, sc2_ref, gpost_ref, gpre_ref, wy_ref, wo_ref, x1_ref, h2_ref):
    tm = x_ref.shape[0]
    chunk = min(tm, 256)
    for r in range(0, tm, chunk):
        rows = slice(r, r + chunk)
        g1, sh2, sc2 = (v if v.shape[0] == 1 else v[rows] for v in (g1_ref[0], sh2_ref[0], sc2_ref[0]))
        m = _dot(y_ref[rows, :], wy_ref[...]) + _dot(o_ref[rows, :].astype(BF16), wo_ref[...])
        x1 = x_ref[rows, :] + g1 * _rms(m, gpost_ref[...])
        x1_ref[rows, :] = x1
        h2 = _rms(x1, gpre_ref[...]) * (1.0 + sc2) + sh2
        h2_ref[rows, :] = h2.astype(BF16)


def _outproj(y, o, x, mods, gpost, gpre, w_out, tm, tiles_per_group):
    m = x.shape[0]
    rows = mods.shape[1]
    half = w_out.shape[0] // 2
    row = lambda i: (i, 0)
    return pl.pallas_call(
        _outproj_kernel,
        grid=(m // tm,),
        in_specs=[pl.BlockSpec((tm, D_SSD), row), pl.BlockSpec((tm, D_ATT), row), pl.BlockSpec((tm, D_MODEL), row),
                  _mod_spec(2, rows, tiles_per_group), _mod_spec(3, rows, tiles_per_group),
                  _mod_spec(4, rows, tiles_per_group),
                  _const_spec(gpost.shape), _const_spec(gpre.shape),
                  pl.BlockSpec((half, D_MODEL), lambda i: (0, 0)),
                  pl.BlockSpec((half, D_MODEL), lambda i: (1, 0))],
        out_specs=[pl.BlockSpec((tm, D_MODEL), row), pl.BlockSpec((tm, D_MODEL), row)],
        out_shape=[jax.ShapeDtypeStruct((m, D_MODEL), F32), jax.ShapeDtypeStruct((m, D_MODEL), BF16)],
        compiler_params=pltpu.CompilerParams(dimension_semantics=("arbitrary",), vmem_limit_bytes=VMEM_LIMIT),
        name="out_proj",
    )(y, o, x, mods, mods, mods, gpost, gpre, w_out, w_out)


FFN_CHUNK = 256


def _ffn_kernel(sample, tiles_per_group, *refs):
    if sample:
        (h_ref, x1_hbm, g2_ref, gpost_ref, wg_ref, wv_ref, wo_ref, cwg_ref, cwv_ref, cbg_ref, cbv_ref,
         injg_ref, injv_ref, y_ref, ug_ref, uv_ref, x1_buf, x1_sem) = refs
    else:
        (h_ref, x1_hbm, g2_ref, gpost_ref, wg_ref, wv_ref, wo_ref, cwg_ref, cwv_ref, cbg_ref, cbv_ref,
         y_ref, ug_ref, uv_ref, x1_buf, x1_sem, halog, halov) = refs
    i = pl.program_id(0)
    j = pl.program_id(1)
    tm = h_ref.shape[0]
    tf = wg_ref.shape[1]
    hn = h_ref[...]
    x1_copy = pltpu.make_async_copy(x1_hbm.at[pl.ds(pl.multiple_of(i * tm, tm), tm), :], x1_buf, x1_sem)

    @pl.when(j == 0)
    def _():
        x1_copy.start()
        y_ref[...] = jnp.zeros_like(y_ref)

    if sample:
        is_ctx = (lax.broadcasted_iota(jnp.int32, (tm, 1), 0) % SAMPLE_ROWS) >= SAMPLE_ROWS - (FFN_CONV - 1)
    else:
        @pl.when((i == 0) & (j == 0))
        def _():
            halog[...] = jnp.zeros_like(halog)
            halov[...] = jnp.zeros_like(halov)

        first = lax.rem(i, tiles_per_group) == 0

    def half(w_ref, cw_ref, cb_ref, u_ref, inj_ref, halo, cols):
        u = jnp.concatenate([_dot(hn[r:r + 128], w_ref[:, cols]) for r in range(0, tm, 128)], axis=0)
        if sample:
            u_ref[:, cols] = u
            prev = inj_ref[0:8, cols]
            u = jnp.where(is_ctx, inj_ref[8:, cols], u)
        else:
            prev = jnp.where(first, 0.0, halo[j, :, cols])
            halo[j, :, cols] = u[tm - 8:, :]
            u_ref[0, :, cols] = u[tm - 8:, :]
        return _conv_rows(u, prev, cw_ref[:, cols], cb_ref[:, cols])

    acts = []
    for c in range(tf // FFN_CHUNK):
        cols = slice(c * FFN_CHUNK, (c + 1) * FFN_CHUNK)
        gate = half(wg_ref, cwg_ref, cbg_ref, ug_ref, None if not sample else injg_ref, None if sample else halog, cols)
        val = half(wv_ref, cwv_ref, cbv_ref, uv_ref, None if not sample else injv_ref, None if sample else halov, cols)
        acts.append((_silu(gate) * val).astype(BF16))
    y_ref[...] += _dot(jnp.concatenate(acts, axis=1), wo_ref[...])

    @pl.when(j == pl.num_programs(1) - 1)
    def _():
        x1_copy.wait()
        chunk = min(tm, 256)
        for r in range(0, tm, chunk):
            rows = slice(r, r + chunk)
            g2 = g2_ref[0]
            g2 = g2 if g2.shape[0] == 1 else g2[rows]
            y_ref[rows, :] = x1_buf[rows, :] + g2 * _rms(y_ref[rows, :], gpost_ref[...])


def _ffn(h2, x1, mods, gpost, w_in, w_out, cw, cb, tm, tiles_per_group, inj=None):
    m = h2.shape[0]
    rows = mods.shape[1]
    tf = 512
    nf = D_FF // tf
    sample = inj is not None
    row = lambda i, j: (i, 0)
    gate_col = lambda i, j: (0, j)
    val_col = lambda i, j: (0, nf + j)
    in_specs = [pl.BlockSpec((tm, D_MODEL), row), pl.BlockSpec(memory_space=pl.ANY),
                _mod_spec(5, rows, tiles_per_group), _const_spec(gpost.shape),
                pl.BlockSpec((D_MODEL, tf), gate_col), pl.BlockSpec((D_MODEL, tf), val_col),
                pl.BlockSpec((tf, D_MODEL), lambda i, j: (j, 0)),
                pl.BlockSpec((FFN_CONV, tf), gate_col), pl.BlockSpec((FFN_CONV, tf), val_col),
                pl.BlockSpec((1, tf), gate_col), pl.BlockSpec((1, tf), val_col)]
    args = [h2, x1, mods, gpost, w_in, w_in, w_out, cw, cw, cb, cb]
    scratch = [pltpu.VMEM((tm, D_MODEL), F32), pltpu.SemaphoreType.DMA(())]
    if sample:
        in_specs += [pl.BlockSpec((m + 8, tf), gate_col), pl.BlockSpec((m + 8, tf), val_col)]
        args += [inj, inj]
        u_specs = [pl.BlockSpec((tm, tf), lambda i, j: (i, j))] * 2
        u_shapes = [jax.ShapeDtypeStruct((m, D_FF), F32)] * 2
    else:
        u_specs = [pl.BlockSpec((1, 8, tf), lambda i, j: (i, 0, j))] * 2
        u_shapes = [jax.ShapeDtypeStruct((m // tm, 8, D_FF), F32)] * 2
        scratch += [pltpu.VMEM((nf, 8, tf), F32), pltpu.VMEM((nf, 8, tf), F32)]
    return pl.pallas_call(
        functools.partial(_ffn_kernel, sample, tiles_per_group),
        grid=(m // tm, nf),
        in_specs=in_specs,
        out_specs=[pl.BlockSpec((tm, D_MODEL), row)] + u_specs,
        out_shape=[jax.ShapeDtypeStruct((m, D_MODEL), F32)] + u_shapes,
        scratch_shapes=scratch,
        compiler_params=pltpu.CompilerParams(dimension_semantics=("arbitrary", "arbitrary"),
                                             vmem_limit_bytes=VMEM_LIMIT),
        name="ffn_sample" if sample else "ffn_prompt",
    )(*args)


def _ssd_constants(block_rows):
    r = jnp.arange(SSD_CHUNK)[:, None]
    c = jnp.arange(SSD_CHUNK)[None, :]
    same = (r // block_rows) == (c // block_rows)
    tri = ((c <= r) & same).astype(BF16)
    sel = (c == (r // block_rows) * block_rows + block_rows - 1).astype(BF16)
    hrow = jnp.arange(LANES)[:, None]
    e1 = (hrow == jnp.arange(D_SSD)[None, :] // SSD_HEAD_DIM).astype(BF16)
    e2 = (hrow == jnp.arange(SSD_HEADS * LANES)[None, :] // LANES).astype(BF16)
    return tri, sel, jnp.concatenate([e1, e2], axis=1)


def _suffix_matrix():
    j = jnp.arange(LANES)[:, None]
    s = jnp.arange(LANES)[None, :]
    u = jnp.concatenate([(j > s).astype(BF16), jnp.ones((LANES, LANES), BF16)], axis=1)
    return jnp.concatenate([u, u], axis=0)


def _inject_rows(ctx, nctx):
    nseq, _, c = ctx.shape
    body = jnp.concatenate([ctx[1:], jnp.zeros((1, nctx, c), F32)], axis=0)
    body = jnp.pad(body, ((0, 0), (SAMPLE_ROWS - nctx, 0), (0, 0))).reshape(nseq * SAMPLE_ROWS, c)
    head = jnp.pad(ctx[0], ((8 - nctx, 0), (0, 0)))
    return jnp.concatenate([head, body], axis=0)


def kernel(x_prompt, x_sample, c_prompt, c_sample, cache_k, cache_v, page_table, state_ssm, state_conv, state_ffn_conv, w_ada, b_ada, g_pre_mix, g_post_mix, g_pre_ffn, g_post_ffn, w_in, conv_w, conv_b, dt_bias, a_log, d_skip, g_ssd_norm, sb_bias, w_out, w_ffn_in, ffn_conv_w, ffn_conv_b, w_ffn_out):
    assert w_ada.shape[0] == 1, "one layer"
    nb, seq, d = x_prompt.shape
    ns, ls, _ = x_sample.shape
    n_pool = cache_k.shape[1]
    mp = nb * seq
    ms = ns * SAMPLE_ROWS

    w = w_in[0].astype(BF16)
    o_bc, o_q = 2 * D_SSD, 2 * D_SSD + BC_DIM + SSD_HEADS
    w_zx = w
    w_qkv = w[:, o_q:]
    w_tail = jnp.pad(w[:, o_bc:o_q], ((0, 0), (0, DT_PAD - SSD_HEADS)))
    row = lambda v: v.reshape(1, -1)
    dtb = jnp.pad(dt_bias[0], (0, DT_PAD - SSD_HEADS)).reshape(1, DT_PAD)
    alog = jnp.pad(a_log[0], (0, DT_PAD - SSD_HEADS)).reshape(1, DT_PAD)
    dskip = jnp.repeat(d_skip[0], SSD_HEAD_DIM).reshape(1, D_SSD)
    gnorm = row(g_ssd_norm[0])
    cw, cb = conv_w[0], row(conv_b[0])
    fcw, fcb = ffn_conv_w[0], row(ffn_conv_b[0])
    uo = _suffix_matrix()

    n_c = nb + ns
    c_all = jnp.concatenate([c_prompt, c_sample, jnp.zeros((-n_c % 16, d), F32)], axis=0)
    mod = _ada(c_all, w_ada[0], row(b_ada[0]))
    mods_p = mod[:nb].reshape(nb, 1, 6 * d)
    mods_s = jnp.repeat(mod[nb:n_c], SAMPLE_ROWS, axis=0).reshape(1, ms, 6 * d)

    tm_p = 512
    tpg_p = seq // tm_p
    xp = x_prompt.reshape(mp, d)
    hn_p, tail_p = _prenorm(xp, mods_p, row(g_pre_mix[0]), w_tail, tm_p, tpg_p)
    zx_p, qkv_p, k_p, v_p = _inproj(hn_p, w_zx, w_qkv, 1024, BF16)
    zx3 = zx_p.reshape(nb, seq, 2 * D_SSD)
    tail3 = tail_p.reshape(nb, seq, PROJ_TAIL)
    tri, sel, emat = _ssd_constants(SSD_CHUNK)
    y_ssd, ssm_p = _ssd_prompt(zx3, tail3, cw, cb, dtb, alog, dskip, gnorm, tri, sel, emat)
    o_att, w_out_b, w_ffn_in_b, w_ffn_out_b = _attn_prompt(qkv_p.reshape(nb, seq, 3 * D_ATT), sb_bias[0], uo,
                                                           [w_out[0], w_ffn_in[0], w_ffn_out[0]])
    tm_e = 512
    x1_p, h2_p = _outproj(y_ssd.reshape(mp, D_SSD), o_att.reshape(mp, D_ATT), xp, mods_p, row(g_post_mix[0]),
                          row(g_pre_ffn[0]), w_out_b, tm_e, seq // tm_e)
    tm_f = 1024
    yp, ctxg_p, ctxv_p = _ffn(h2_p, x1_p, mods_p, row(g_post_ffn[0]), w_ffn_in_b, w_ffn_out_b, fcw, fcb,
                              tm_f, seq // tm_f)

    xs_pad = jnp.pad(x_sample, ((0, 0), (0, SAMPLE_ROWS - ls), (0, 0))).reshape(ms, d)
    hn_s, tail_s = _prenorm(xs_pad, mods_s, row(g_pre_mix[0]), w_tail, ms, 1)
    zx_s, qkv_s, k_s, v_s = _inproj(hn_s, w_zx, w_qkv, ms, F32)
    seq_per_tile = SSD_CHUNK // SAMPLE_ROWS
    conv_ctx = state_conv[0].reshape(ns // seq_per_tile, seq_per_tile, SSD_CONV - 1, -1)
    ctx_inj = jax.vmap(lambda c: _inject_rows(c, SSD_CONV - 1))(conv_ctx)
    tri_s, sel_s, _ = _ssd_constants(SAMPLE_ROWS)
    y_ssd_s, ssm_s = _ssd_sample(zx_s, tail_s, ctx_inj, state_ssm[0].reshape(ns, D_SSD, SSD_STATE), cw, cb, dtb, alog,
                                 dskip, gnorm, tri_s, sel_s, emat)
    bias_rows = jnp.broadcast_to(jnp.repeat(sb_bias[0], SAMPLE_ROWS)[:, None], (ATT_HEADS * SAMPLE_ROWS, LANES))
    o_att_s = _attn_sample(page_table, qkv_s.reshape(ns, SAMPLE_ROWS, 3 * D_ATT),
                           cache_k[0].reshape(n_pool, PAGE_SIZE * ATT_HEADS, ATT_HEAD_DIM),
                           cache_v[0].reshape(n_pool, PAGE_SIZE * ATT_HEADS, ATT_HEAD_DIM), bias_rows, uo)
    x1_s, h2_s = _outproj(y_ssd_s, o_att_s.reshape(ms, D_ATT), xs_pad, mods_s, row(g_post_mix[0]),
                          row(g_pre_ffn[0]), w_out_b, ms, 1)
    inj_f = _inject_rows(state_ffn_conv[0], FFN_CONV - 1)
    ys_pad, ug_s, uv_s = _ffn(h2_s, x1_s, mods_s, row(g_post_ffn[0]), w_ffn_in_b, w_ffn_out_b, fcw, fcb,
                              ms, 1, inj=inj_f)

    heads = (ATT_HEADS, ATT_HEAD_DIM)
    k_p = k_p.reshape(1, nb, seq, *heads)
    v_p = v_p.reshape(1, nb, seq, *heads)
    tail = slice(seq - (SSD_CONV - 1), seq)
    conv_p = jnp.concatenate([zx3[:, tail, D_SSD:], tail3[:, tail, :BC_DIM]], axis=-1)[None]
    ffn_p = jnp.concatenate([ctxg_p, ctxv_p], axis=-1)[seq // tm_f - 1::seq // tm_f, 8 - (FFN_CONV - 1):][None]
    k_s = k_s.reshape(ns, SAMPLE_ROWS, *heads)[None, :, :ls]
    v_s = v_s.reshape(ns, SAMPLE_ROWS, *heads)[None, :, :ls]
    raw_s = jnp.concatenate([zx_s.reshape(ns, SAMPLE_ROWS, -1)[:, :ls, D_SSD:],
                             tail_s.reshape(ns, SAMPLE_ROWS, -1)[:, :ls, :BC_DIM]], axis=-1)
    conv_s = jnp.concatenate([state_conv[0], raw_s], axis=1)[:, -(SSD_CONV - 1):][None]
    keep = FFN_CONV - 1
    u_rows = [u.reshape(ns, SAMPLE_ROWS, D_FF)[:, max(ls - keep, 0):ls] for u in (ug_s, uv_s)]
    ffn_s = jnp.concatenate([state_ffn_conv[0][:, ls:], jnp.concatenate(u_rows, axis=-1)], axis=1)[None]
    return (yp.reshape(nb, seq, d), ys_pad.reshape(ns, SAMPLE_ROWS, d)[:, :ls],
            k_p, v_p, ssm_p.reshape(1, nb, SSD_HEADS, SSD_HEAD_DIM, SSD_STATE), conv_p, ffn_p,
            k_s, v_s, ssm_s.reshape(1, ns, SSD_HEADS, SSD_HEAD_DIM, SSD_STATE), conv_s, ffn_s)
```

```python
import functools

import jax
import jax.numpy as jnp
from jax import lax
from jax.experimental import pallas as pl
from jax.experimental.pallas import tpu as pltpu

F32 = jnp.float32
BF16 = jnp.bfloat16

D_MODEL = 2048
D_SSD = 1024
SSD_HEAD_DIM = 64
SSD_HEADS = 16
SSD_GROUPS = 2
SSD_STATE = 128
SSD_CONV = 4
SSD_CHUNK = 128
SSD_STEP_CHUNKS = 4
SSD_SAMPLE_SEQS = 2
BC_DIM = 2 * SSD_GROUPS * SSD_STATE
D_ATT = 1024
ATT_HEAD_DIM = 128
ATT_HEADS = 8
D_FF = 5632
FFN_CONV = 3
PAGE_SIZE = 128
EPS = 1e-6
ATT_SCALE = ATT_HEAD_DIM ** -0.5

PROJ_TN = 1024
PROJ_MAIN = 5 * PROJ_TN
DT_PAD = 128
PROJ_TAIL = BC_DIM + DT_PAD

SAMPLE_ROWS = 8
LANES = 128
VMEM_LIMIT = 56 * 1024 * 1024


def _dot(a, b):
    return jnp.dot(a, b, preferred_element_type=F32)


def _dot_nt(a, b):
    return lax.dot_general(a, b, (((1,), (1,)), ((), ())), preferred_element_type=F32)


def _sigmoid(x):
    return 1.0 / (1.0 + jnp.exp(-x))


def _silu(x):
    return x * _sigmoid(x)


def _softplus(x):
    return jnp.maximum(x, 0.0) + jnp.log1p(jnp.exp(-jnp.abs(x)))


def _rms(x, g):
    ms = jnp.mean(x * x, axis=-1, keepdims=True)
    return x * lax.rsqrt(ms + EPS) * g


def _split3(x):
    h = x.astype(BF16)
    r = x - h.astype(F32)
    m = r.astype(BF16)
    l = (r - m.astype(F32)).astype(BF16)
    return h, m, l


def _dot3_right(x, mat):
    return _dot(jnp.concatenate(_split3(x), axis=1), jnp.concatenate([mat] * 3, axis=0))


def _dot3_left(mat, x):
    return _dot(jnp.concatenate([mat] * 3, axis=1), jnp.concatenate(_split3(x), axis=0))


def _ada_kernel(c_ref, w_ref, b_ref, o_ref):
    s = _silu(c_ref[...]).astype(BF16)
    o_ref[...] = _dot(s, w_ref[...].astype(BF16)) + b_ref[...]


def _ada(c_all, w_ada, b_ada):
    m, d = c_all.shape
    n = w_ada.shape[1]
    tn = 1024
    return pl.pallas_call(
        _ada_kernel,
        grid=(n // tn,),
        in_specs=[pl.BlockSpec((m, d), lambda j: (0, 0)),
                  pl.BlockSpec((d, tn), lambda j: (0, j)),
                  pl.BlockSpec((1, tn), lambda j: (0, j))],
        out_specs=pl.BlockSpec((m, tn), lambda j: (0, j)),
        out_shape=jax.ShapeDtypeStruct((m, n), F32),
        compiler_params=pltpu.CompilerParams(dimension_semantics=("arbitrary",), vmem_limit_bytes=VMEM_LIMIT),
        name="ada_mod",
    )(c_all, w_ada, b_ada)


def _mod_spec(idx, rows, tiles_per_group):
    return pl.BlockSpec((1, rows, D_MODEL), lambda i, *_: (i // tiles_per_group, 0, idx))


def _prenorm_kernel(x_ref, sh_ref, sc_ref, g_ref, wt_ref, hn_ref, tail_ref):
    tm = x_ref.shape[0]
    chunk = min(tm, 128)
    g = g_ref[...]
    for r in range(0, tm, chunk):
        sc, sh = sc_ref[0], sh_ref[0]
        if sc.shape[0] != 1:
            sc, sh = sc[r:r + chunk], sh[r:r + chunk]
        h = (_rms(x_ref[r:r + chunk, :], g) * (1.0 + sc) + sh).astype(BF16)
        hn_ref[r:r + chunk, :] = h
        tail_ref[r:r + chunk, :] = _dot(h, wt_ref[...])


def _prenorm(x, mods, g, wt, tm, tiles_per_group):
    m = x.shape[0]
    rows = mods.shape[1]
    return pl.pallas_call(
        _prenorm_kernel,
        grid=(m // tm,),
        in_specs=[pl.BlockSpec((tm, D_MODEL), lambda i: (i, 0)),
                  _mod_spec(0, rows, tiles_per_group),
                  _mod_spec(1, rows, tiles_per_group),
                  _const_spec(g.shape), _const_spec(wt.shape)],
        out_specs=[pl.BlockSpec((tm, D_MODEL), lambda i: (i, 0)), pl.BlockSpec((tm, PROJ_TAIL), lambda i: (i, 0))],
        out_shape=[jax.ShapeDtypeStruct((m, D_MODEL), BF16), jax.ShapeDtypeStruct((m, PROJ_TAIL), F32)],
        compiler_params=pltpu.CompilerParams(dimension_semantics=("arbitrary",), vmem_limit_bytes=VMEM_LIMIT),
        name="pre_norm",
    )(x, mods, mods, g, wt)


def _inproj_kernel(hn_ref, wzx_ref, w_ref, zx_ref, qkv_ref, k_ref, v_ref):
    j = pl.program_id(1)
    tm = hn_ref.shape[0]

    def project(state_ref):
        res = _dot(hn_ref[...], w_ref[...])
        qkv_ref[...] = res.astype(qkv_ref.dtype)
        if state_ref is not None:
            for h in range(ATT_HEADS):
                state_ref[pl.ds(h, tm, stride=ATT_HEADS), :] = res[:, h * ATT_HEAD_DIM:(h + 1) * ATT_HEAD_DIM]

    @pl.when(j < 2)
    def _():
        zx_ref[...] = _dot(hn_ref[...], wzx_ref[...])

    @pl.when(j == 2)
    def _():
        project(None)

    @pl.when(j == 3)
    def _():
        project(k_ref)

    @pl.when(j == 4)
    def _():
        project(v_ref)


def _inproj(hn, w_zx, w_qkv, tm, qkv_dtype):
    m = hn.shape[0]
    kv_spec = pl.BlockSpec((tm * ATT_HEADS, ATT_HEAD_DIM), lambda i, j: (i, 0))
    kv_shape = jax.ShapeDtypeStruct((m * ATT_HEADS, ATT_HEAD_DIM), F32)
    return pl.pallas_call(
        _inproj_kernel,
        grid=(m // tm, PROJ_MAIN // PROJ_TN),
        in_specs=[pl.BlockSpec((tm, D_MODEL), lambda i, j: (i, 0)),
                  pl.BlockSpec((D_MODEL, PROJ_TN), lambda i, j: (0, jnp.minimum(j, 1))),
                  pl.BlockSpec((D_MODEL, PROJ_TN), lambda i, j: (0, jnp.clip(j - 2, 0, 2)))],
        out_specs=[pl.BlockSpec((tm, PROJ_TN), lambda i, j: (i, jnp.minimum(j, 1))),
                   pl.BlockSpec((tm, PROJ_TN), lambda i, j: (i, jnp.clip(j - 2, 0, 2))),
                   kv_spec, kv_spec],
        out_shape=[jax.ShapeDtypeStruct((m, 2 * PROJ_TN), F32),
                   jax.ShapeDtypeStruct((m, 3 * PROJ_TN), qkv_dtype), kv_shape, kv_shape],
        compiler_params=pltpu.CompilerParams(dimension_semantics=("arbitrary", "arbitrary"),
                                             vmem_limit_bytes=VMEM_LIMIT),
        name="in_proj",
    )(hn, w_zx, w_qkv)


def _shift_rows(u, prev8, s):
    r = pltpu.roll(u, s, 0)
    head = jnp.where(lax.broadcasted_iota(jnp.int32, (8, 1), 0) < s, pltpu.roll(prev8, s, 0), r[:8])
    return jnp.concatenate([head, r[8:]], axis=0)


def _conv_rows(u, prev8, w, b):
    taps = w.shape[0]
    y = _shift_rows(u, prev8, taps - 1) * w[0:1]
    for t in range(1, taps):
        s = taps - 1 - t
        y = y + (_shift_rows(u, prev8, s) if s else u) * w[t:t + 1]
    return y + b


def _ssd_tile(xs, bc, dtr, row_valid, causal, tri, sel, emat, dtb, alog):
    dt = _softplus(dtr + dtb)
    if row_valid is not None:
        dt = jnp.where(row_valid, dt, 0.0)
    a = -jnp.exp(alog)
    acum = _dot3_left(tri, dt * a)
    alast = _dot3_left(sel, acum)
    acx = _dot3_right(acum, emat)
    acum_x = acx[:, :D_SSD]
    acum_x2 = acx[:, D_SSD:]
    e1 = emat[:, :D_SSD]
    dt_x = _dot3_right(dt, e1)
    alast_x = _dot3_right(alast, e1)
    acum_t = acum.T

    xdt = xs * dt_x
    xdt_bf = xdt.astype(BF16)
    xdtd_bf = (xdt * jnp.exp(alast_x - acum_x)).astype(BF16)
    bm = bc[:, :SSD_GROUPS * SSD_STATE]
    cm = bc[:, SSD_GROUPS * SSD_STATE:]
    lane = lax.broadcasted_iota(jnp.int32, (SSD_CHUNK, LANES), 1)
    first_head = lane < SSD_HEAD_DIM

    pieces = []
    heads_per_group = SSD_HEADS // SSD_GROUPS
    for g in range(SSD_GROUPS):
        bg = bm[:, g * SSD_STATE:(g + 1) * SSD_STATE].astype(BF16)
        cg = cm[:, g * SSD_STATE:(g + 1) * SSD_STATE].astype(BF16)
        cb = _dot_nt(cg, bg)
        for pair in range(heads_per_group // 2):
            h0 = g * heads_per_group + 2 * pair
            ys = []
            for hh in (h0, h0 + 1):
                seg = acum_x2[:, hh * LANES:(hh + 1) * LANES] - acum_t[hh:hh + 1, :]
                decay = jnp.where(causal, jnp.exp(jnp.where(causal, seg, 0.0)), 0.0)
                sc = (cb * decay).astype(BF16)
                ys.append(_dot(sc, xdt_bf[:, (h0 // 2) * LANES:(h0 // 2 + 1) * LANES]))
            pieces.append(jnp.where(first_head, ys[0], ys[1]))
    y_diag = jnp.concatenate(pieces, axis=1)
    return y_diag, xdtd_bf, acum_x, bm, cm


def _ssd_finish(y, xs, z, dskip, gnorm):
    y = y + dskip * xs
    y = y * _silu(z)
    half = D_SSD // SSD_GROUPS
    outs = []
    for g in range(SSD_GROUPS):
        outs.append(_rms(y[:, g * half:(g + 1) * half], gnorm[:, g * half:(g + 1) * half]))
    return jnp.concatenate(outs, axis=1)


def _ssd_prompt_kernel(z_ref, xs_ref, bc_ref, dt_ref, cw_ref, cb_ref, dtb_ref, alog_ref, dskip_ref, gnorm_ref,
                       tri_ref, sel_ref, emat_ref, y_ref, st_ref, ht_ref, prev_xs, prev_bc):
    c = pl.program_id(1)
    rows = SSD_CHUNK

    @pl.when(c == 0)
    def _():
        ht_ref[...] = jnp.zeros_like(ht_ref)
        prev_xs[...] = jnp.zeros_like(prev_xs)
        prev_bc[...] = jnp.zeros_like(prev_bc)

    cw = cw_ref[...]
    cb = cb_ref[...]
    r = lax.broadcasted_iota(jnp.int32, (rows, rows), 0)
    s = lax.broadcasted_iota(jnp.int32, (rows, rows), 1)
    causal = s <= r
    half = D_SSD // SSD_GROUPS
    px = prev_xs[...]
    pb = prev_bc[...]
    ht_new = ht_ref[...]
    for sub in range(xs_ref.shape[1] // rows):
        rs = slice(sub * rows, (sub + 1) * rows)
        xr = xs_ref[0, rs, :]
        br = bc_ref[0, rs, :]
        xs = _silu(_conv_rows(xr, px, cw[:, :D_SSD], cb[:, :D_SSD]))
        bc = _silu(_conv_rows(br, pb, cw[:, D_SSD:], cb[:, D_SSD:]))
        px = xr[rows - 8:, :]
        pb = br[rows - 8:, :]
        y_diag, xdtd_bf, acum_x, bm, cm = _ssd_tile(xs, bc, dt_ref[0, rs, :], None, causal, tri_ref[...], sel_ref[...],
                                                     emat_ref[...], dtb_ref[...], alog_ref[...])
        ht = ht_new
        y_off, states = [], []
        for g in range(SSD_GROUPS):
            bg = bm[:, g * SSD_STATE:(g + 1) * SSD_STATE]
            cg = cm[:, g * SSD_STATE:(g + 1) * SSD_STATE].astype(BF16)
            y_off.append(_dot(cg, ht[:, g * half:(g + 1) * half].astype(BF16)))
            states.append(_dot(bg.T.astype(BF16), xdtd_bf[:, g * half:(g + 1) * half]))
        y = y_diag + jnp.concatenate(y_off, axis=1) * jnp.exp(acum_x)
        ht_new = jnp.exp(acum_x[rows - 1:rows, :]) * ht + jnp.concatenate(states, axis=1)
        y_ref[0, rs, :] = _ssd_finish(y, xs, z_ref[0, rs, :], dskip_ref[...], gnorm_ref[...]).astype(BF16)
    prev_xs[...] = px
    prev_bc[...] = pb
    ht_ref[...] = ht_new

    @pl.when(c == pl.num_programs(1) - 1)
    def _():
        for k in range(D_SSD // LANES):
            st_ref[0, k * LANES:(k + 1) * LANES, :] = ht_new[:, k * LANES:(k + 1) * LANES].T


def _const_spec(shape):
    nd = len(shape)
    return pl.BlockSpec(shape, lambda *_: (0,) * nd)


def _ssd_prompt(zx3, tail3, cw, cb, dtb, alog, dskip, gnorm, tri, sel, emat):
    nb, seq, _ = zx3.shape
    step = SSD_STEP_CHUNKS * SSD_CHUNK
    nc = seq // step
    return pl.pallas_call(
        _ssd_prompt_kernel,
        grid=(nb, nc),
        in_specs=[pl.BlockSpec((1, step, D_SSD), lambda b, c: (b, c, 0)),
                  pl.BlockSpec((1, step, D_SSD), lambda b, c: (b, c, 1)),
                  pl.BlockSpec((1, step, BC_DIM), lambda b, c: (b, c, 0)),
                  pl.BlockSpec((1, step, DT_PAD), lambda b, c: (b, c, BC_DIM // DT_PAD)),
                  _const_spec(cw.shape), _const_spec(cb.shape), _const_spec(dtb.shape), _const_spec(alog.shape),
                  _const_spec(dskip.shape), _const_spec(gnorm.shape), _const_spec(tri.shape),
                  _const_spec(sel.shape), _const_spec(emat.shape)],
        out_specs=[pl.BlockSpec((1, step, D_SSD), lambda b, c: (b, c, 0)),
                   pl.BlockSpec((1, D_SSD, SSD_STATE), lambda b, c: (b, 0, 0))],
        out_shape=[jax.ShapeDtypeStruct((nb, seq, D_SSD), BF16),
                   jax.ShapeDtypeStruct((nb, D_SSD, SSD_STATE), F32)],
        scratch_shapes=[pltpu.VMEM((SSD_STATE, D_SSD), F32),
                        pltpu.VMEM((8, D_SSD), F32),
                        pltpu.VMEM((8, BC_DIM), F32)],
        compiler_params=pltpu.CompilerParams(dimension_semantics=("arbitrary", "arbitrary"),
                                             vmem_limit_bytes=VMEM_LIMIT),
        name="ssd_prompt",
    )(zx3, zx3, tail3, tail3, cw, cb, dtb, alog, dskip, gnorm, tri, sel, emat)


def _ssd_sample_kernel(z_ref, xs_ref, bc_ref, dt_ref, ctx_ref, h0_ref, cw_ref, cb_ref, dtb_ref, alog_ref, dskip_ref,
                       gnorm_ref, tri_ref, sel_ref, emat_ref, y_ref, st_ref,
                       ybase, yoff, xs_s, xdtd_s, acumx_s, bm_s, cm_s):
    s_id = pl.program_id(1)
    rows = SSD_CHUNK
    nseq = rows // SAMPLE_ROWS
    rr = lax.broadcasted_iota(jnp.int32, (rows, 1), 0)

    @pl.when(s_id == 0)
    def _():
        ctx = ctx_ref[0]
        is_ctx = (rr % SAMPLE_ROWS) >= SAMPLE_ROWS - (SSD_CONV - 1)
        xr = jnp.where(is_ctx, ctx[8:, :D_SSD], xs_ref[...])
        br = jnp.where(is_ctx, ctx[8:, D_SSD:], bc_ref[...])
        cw = cw_ref[...]
        cb = cb_ref[...]
        xs = _silu(_conv_rows(xr, ctx[0:8, :D_SSD], cw[:, :D_SSD], cb[:, :D_SSD]))
        bc = _silu(_conv_rows(br, ctx[0:8, D_SSD:], cw[:, D_SSD:], cb[:, D_SSD:]))
        r = lax.broadcasted_iota(jnp.int32, (rows, rows), 0)
        c = lax.broadcasted_iota(jnp.int32, (rows, rows), 1)
        causal = (c <= r) & ((c // SAMPLE_ROWS) == (r // SAMPLE_ROWS))
        row_valid = (rr % SAMPLE_ROWS) < (SAMPLE_ROWS // 2)
        y_diag, xdtd_bf, acum_x, bm, cm = _ssd_tile(xs, bc, dt_ref[...], row_valid, causal, tri_ref[...],
                                                     sel_ref[...], emat_ref[...], dtb_ref[...], alog_ref[...])
        ybase[...] = y_diag
        yoff[...] = jnp.zeros_like(yoff)
        xs_s[...] = xs
        xdtd_s[...] = xdtd_bf
        acumx_s[...] = acum_x
        bm_s[...] = bm
        cm_s[...] = cm

    half = D_SSD // SSD_GROUPS
    add = None
    for q in range(SSD_SAMPLE_SEQS):
        seq = s_id * SSD_SAMPLE_SEQS + q
        in_seq_rows = (rr // SAMPLE_ROWS) == seq
        in_seq_lanes = (lax.broadcasted_iota(jnp.int32, (1, rows), 1) // SAMPLE_ROWS) == seq
        h0 = h0_ref[q]
        ht = jnp.concatenate([h0[k * LANES:(k + 1) * LANES, :].T for k in range(D_SSD // LANES)], axis=1)
        y_off, states = [], []
        for g in range(SSD_GROUPS):
            bg_t = jnp.where(in_seq_lanes, bm_s[:, g * SSD_STATE:(g + 1) * SSD_STATE].T, 0.0).astype(BF16)
            cg = cm_s[:, g * SSD_STATE:(g + 1) * SSD_STATE].astype(BF16)
            y_off.append(_dot(cg, ht[:, g * half:(g + 1) * half].astype(BF16)))
            states.append(_dot(bg_t, xdtd_s[:, g * half:(g + 1) * half]))
        part = jnp.where(in_seq_rows, jnp.concatenate(y_off, axis=1), 0.0)
        add = part if add is None else add + part
        last = pl.multiple_of(seq * SAMPLE_ROWS, SAMPLE_ROWS) + (SAMPLE_ROWS - 1)
        ht_new = jnp.exp(acumx_s[pl.ds(last, 1), :]) * ht + jnp.concatenate(states, axis=1)
        for k in range(D_SSD // LANES):
            st_ref[q, k * LANES:(k + 1) * LANES, :] = ht_new[:, k * LANES:(k + 1) * LANES].T
    yoff[...] = yoff[...] + add

    @pl.when(s_id == nseq // SSD_SAMPLE_SEQS - 1)
    def _():
        y = ybase[...] + yoff[...] * jnp.exp(acumx_s[...])
        y_ref[...] = _ssd_finish(y, xs_s[...], z_ref[...], dskip_ref[...], gnorm_ref[...]).astype(BF16)


def _ssd_sample(zx, tail, ctx, h0, cw, cb, dtb, alog, dskip, gnorm, tri, sel, emat):
    m = zx.shape[0]
    nt = m // SSD_CHUNK
    nseq = SSD_CHUNK // SAMPLE_ROWS
    return pl.pallas_call(
        _ssd_sample_kernel,
        grid=(nt, nseq // SSD_SAMPLE_SEQS),
        in_specs=[pl.BlockSpec((SSD_CHUNK, D_SSD), lambda t, s: (t, 0)),
                  pl.BlockSpec((SSD_CHUNK, D_SSD), lambda t, s: (t, 1)),
                  pl.BlockSpec((SSD_CHUNK, BC_DIM), lambda t, s: (t, 0)),
                  pl.BlockSpec((SSD_CHUNK, DT_PAD), lambda t, s: (t, BC_DIM // DT_PAD)),
                  pl.BlockSpec((1, SSD_CHUNK + 8, D_SSD + BC_DIM), lambda t, s: (t, 0, 0)),
                  pl.BlockSpec((SSD_SAMPLE_SEQS, D_SSD, SSD_STATE), lambda t, s: (t * (nseq // SSD_SAMPLE_SEQS) + s, 0, 0)),
                  _const_spec(cw.shape), _const_spec(cb.shape), _const_spec(dtb.shape), _const_spec(alog.shape),
                  _const_spec(dskip.shape), _const_spec(gnorm.shape), _const_spec(tri.shape),
                  _const_spec(sel.shape), _const_spec(emat.shape)],
        out_specs=[pl.BlockSpec((SSD_CHUNK, D_SSD), lambda t, s: (t, 0)),
                   pl.BlockSpec((SSD_SAMPLE_SEQS, D_SSD, SSD_STATE), lambda t, s: (t * (nseq // SSD_SAMPLE_SEQS) + s, 0, 0))],
        out_shape=[jax.ShapeDtypeStruct((m, D_SSD), BF16),
                   jax.ShapeDtypeStruct(h0.shape, F32)],
        scratch_shapes=[pltpu.VMEM((SSD_CHUNK, D_SSD), F32),
                        pltpu.VMEM((SSD_CHUNK, D_SSD), F32),
                        pltpu.VMEM((SSD_CHUNK, D_SSD), F32),
                        pltpu.VMEM((SSD_CHUNK, D_SSD), BF16),
                        pltpu.VMEM((SSD_CHUNK, D_SSD), F32),
                        pltpu.VMEM((SSD_CHUNK, SSD_GROUPS * SSD_STATE), F32),
                        pltpu.VMEM((SSD_CHUNK, SSD_GROUPS * SSD_STATE), F32)],
        compiler_params=pltpu.CompilerParams(dimension_semantics=("arbitrary", "arbitrary"),
                                             vmem_limit_bytes=VMEM_LIMIT),
        name="ssd_sample",
    )(zx, zx, tail, tail, ctx, h0, cw, cb, dtb, alog, dskip, gnorm, tri, sel, emat)


def _sb_block(qb, kb, vb, bias, uo, run, mask, nearest_first):
    z = _dot_nt(qb, kb) * ATT_SCALE + bias
    nz = -z
    lg = jnp.log(1.0 + jnp.exp(jnp.minimum(z, nz)))
    lm = jnp.minimum(nz, 0.0) - lg
    ls_pos = lm + z
    if mask is not None:
        lm = jnp.where(mask, lm, 0.0)
    hi = lm.astype(BF16)
    lo = (lm - hi.astype(F32)).astype(BF16)
    nsub = z.shape[1] // LANES
    tails = [None] * nsub
    for c in (range(nsub) if nearest_first else reversed(range(nsub))):
        cols = slice(c * LANES, (c + 1) * LANES)
        st = _dot(jnp.concatenate([hi[:, cols], lo[:, cols]], axis=1), uo)
        tails[c] = run + st[:, :LANES]
        run = run + st[:, LANES:]
    w = jnp.exp(ls_pos + jnp.concatenate(tails, axis=1))
    if mask is not None:
        w = jnp.where(mask, w, 0.0)
    return _dot(w.astype(BF16), vb), run


ATT_TQ = 512
ATT_TK = ATT_TQ
ATT_HEADS_PER_STEP = 1


def _attn_prompt_kernel(bias_ref, q_ref, k_ref, v_ref, uo_ref, *refs):
    nw = (len(refs) - 1) // 2
    o_ref = refs[nw]

    def convert_weights():
        for w32_ref, wbf_ref in zip(refs[:nw], refs[nw + 1:]):
            wbf_ref[...] = w32_ref[...].astype(BF16)

    hg = pl.program_id(1)
    qi = pl.program_id(2)
    uo = uo_ref[...]
    heads = range(ATT_HEADS_PER_STEP)
    cols = [slice(h * ATT_HEAD_DIM, (h + 1) * ATT_HEAD_DIM) for h in heads]
    qbs = [q_ref[0, :, cols[h]] for h in heads]
    biases = [bias_ref[hg * ATT_HEADS_PER_STEP + h] for h in heads]
    q0 = pl.multiple_of(qi * ATT_TQ, ATT_TQ)

    def keys_values(h, start, n):
        return k_ref[0, pl.ds(start, n), cols[h]], v_ref[0, pl.ds(start, n), cols[h]]

    def diagonal(h):
        accs, runs = [], []
        for r in range(ATT_TQ // LANES):
            nk = (r + 1) * LANES
            row = lax.broadcasted_iota(jnp.int32, (LANES, nk), 0) + r * LANES
            col = lax.broadcasted_iota(jnp.int32, (LANES, nk), 1)
            kb, vb = keys_values(h, q0, nk)
            a, rn = _sb_block(qbs[h][r * LANES:(r + 1) * LANES], kb, vb, biases[h], uo,
                              jnp.zeros((LANES, LANES), F32), col < row, False)
            accs.append(a)
            runs.append(rn)
        return jnp.concatenate(accs, axis=0), jnp.concatenate(runs, axis=0)

    def block(kb_index, carry):
        start = pl.multiple_of(kb_index * ATT_TK, ATT_TK)
        out = []
        for h in heads:
            acc, run = carry[2 * h], carry[2 * h + 1]
            kb, vb = keys_values(h, start, ATT_TK)
            pv, run = _sb_block(qbs[h], kb, vb, biases[h], uo, run, None, False)
            out += [acc + pv, run]
        return tuple(out)

    def store(carry):
        for h in heads:
            o_ref[0, :, cols[h]] = carry[2 * h].astype(BF16)

    @pl.when(qi == 0)
    def _():
        convert_weights()
        store(sum((diagonal(h) for h in heads), ()))

    @pl.when(qi > 0)
    def _():
        convert_weights()
        carry = block(qi - 1, sum((diagonal(h) for h in heads), ()))
        store(lax.fori_loop(0, qi - 1, lambda i, c: block(qi - 2 - i, c), carry))


def _attn_prompt(qkv3, sb_bias, uo, weights):
    nb, seq, _ = qkv3.shape
    tq = ATT_TQ
    width = ATT_HEADS_PER_STEP * ATT_HEAD_DIM
    groups = ATT_HEADS // ATT_HEADS_PER_STEP
    nq = seq // tq
    steps = nb * groups * nq
    w_specs = []
    for w in weights:
        r, c = w.shape
        if r % (16 * steps) == 0:
            w_specs.append(pl.BlockSpec((r // steps, c), lambda b, g, i: ((b * groups + g) * nq + i, 0)))
        else:
            cb = c // LANES // 2
            rb = steps // cb
            assert r % (16 * rb) == 0 and rb * cb == steps, (w.shape, steps)
            w_specs.append(pl.BlockSpec((r // rb, c // cb),
                                        lambda b, g, i, cb=cb: (((b * groups + g) * nq + i) // cb,
                                                                ((b * groups + g) * nq + i) % cb)))
    return pl.pallas_call(
        _attn_prompt_kernel,
        grid=(nb, groups, nq),
        in_specs=[pl.BlockSpec(memory_space=pltpu.SMEM),
                  pl.BlockSpec((1, tq, width), lambda b, g, i: (b, i, g)),
                  pl.BlockSpec((1, seq, width), lambda b, g, i: (b, 0, groups + g)),
                  pl.BlockSpec((1, seq, width), lambda b, g, i: (b, 0, 2 * groups + g)),
                  _const_spec(uo.shape)] + w_specs,
        out_specs=[pl.BlockSpec((1, tq, width), lambda b, g, i: (b, i, g))] + w_specs,
        out_shape=[jax.ShapeDtypeStruct((nb, seq, D_ATT), BF16)]
                  + [jax.ShapeDtypeStruct(w.shape, BF16) for w in weights],
        compiler_params=pltpu.CompilerParams(dimension_semantics=("arbitrary", "arbitrary", "arbitrary"),
                                             vmem_limit_bytes=VMEM_LIMIT),
        name="attn_prompt",
    )(sb_bias, qkv3, qkv3, qkv3, uo, *weights)


PAGES_PER_STEP = 8
PAGE_SLOTS = 3


def _page_by_head(ref):
    heads = [ref[pl.ds(h, PAGE_SIZE, stride=ATT_HEADS), :] for h in range(ATT_HEADS)]
    return jnp.concatenate(heads, axis=1).astype(BF16)


def _attn_sample_kernel(nseq, steps, pt_ref, q_ref, kn_ref, vn_ref, bias_ref, uo_ref, ck_hbm, cv_hbm, o_ref,
                        qbd, knew, vnew, acc, run, kbuf, vbuf, sems):
    b = pl.program_id(0)
    j = pl.program_id(1)
    total = nseq * steps
    t = b * steps + j
    nrow = ATT_HEADS * SAMPLE_ROWS
    bias = bias_ref[...]
    uo = uo_ref[...]

    def page_copies(step):
        sb = step // steps
        sj = step - sb * steps
        slot = step % PAGE_SLOTS
        copies = []
        for g in range(PAGES_PER_STEP):
            page = pt_ref[sb, steps * PAGES_PER_STEP - 1 - (sj * PAGES_PER_STEP + g)]
            copies.append(pltpu.make_async_copy(ck_hbm.at[page], kbuf.at[slot, g], sems.at[0, slot]))
            copies.append(pltpu.make_async_copy(cv_hbm.at[page], vbuf.at[slot, g], sems.at[1, slot]))
        return copies

    @pl.when(t == 0)
    def _():
        for step in range(PAGE_SLOTS - 1):
            for c in page_copies(step):
                c.start()

    @pl.when(t + (PAGE_SLOTS - 1) < total)
    def _():
        for c in page_copies(t + (PAGE_SLOTS - 1)):
            c.start()

    @pl.when((b == 0) & (j == 0))
    def _():
        knew[...] = jnp.zeros_like(knew)
        vnew[...] = jnp.zeros_like(vnew)

    @pl.when(j == 0)
    def _():
        q8 = q_ref[0]
        rr = lax.broadcasted_iota(jnp.int32, (nrow, D_ATT), 0)
        cc = lax.broadcasted_iota(jnp.int32, (nrow, D_ATT), 1)
        qt = jnp.concatenate([q8] * ATT_HEADS, axis=0)
        qbd[...] = jnp.where((rr // SAMPLE_ROWS) == (cc // ATT_HEAD_DIM), qt, 0.0).astype(BF16)
        knew[0:SAMPLE_ROWS, :] = kn_ref[0]
        vnew[0:SAMPLE_ROWS, :] = vn_ref[0]
        qq = lax.broadcasted_iota(jnp.int32, (nrow, LANES), 0) % SAMPLE_ROWS
        kk = lax.broadcasted_iota(jnp.int32, (nrow, LANES), 1)
        mask = (kk < qq) & (kk < SAMPLE_ROWS // 2)
        pv, r = _sb_block(qbd[...], knew[...].astype(BF16), vnew[...].astype(BF16), bias, uo,
                          jnp.zeros((nrow, LANES), F32), mask, True)
        acc[...] = pv
        run[...] = r

    for c in page_copies(t):
        c.wait()
    slot = t % PAGE_SLOTS
    kcat = jnp.concatenate([_page_by_head(kbuf.at[slot, g]) for g in range(PAGES_PER_STEP)], axis=0)
    vcat = jnp.concatenate([_page_by_head(vbuf.at[slot, g]) for g in range(PAGES_PER_STEP)], axis=0)
    pv, r = _sb_block(qbd[...], kcat, vcat, jnp.concatenate([bias] * PAGES_PER_STEP, axis=1), uo, run[...], None, True)
    a = acc[...] + pv
    acc[...] = a
    run[...] = r

    @pl.when(j == steps - 1)
    def _():
        for h in range(ATT_HEADS):
            o_ref[0, :, h * ATT_HEAD_DIM:(h + 1) * ATT_HEAD_DIM] = (
                a[h * SAMPLE_ROWS:(h + 1) * SAMPLE_ROWS, h * ATT_HEAD_DIM:(h + 1) * ATT_HEAD_DIM])


def _attn_sample(page_table, projs3, cache_k, cache_v, bias_rows, uo):
    nseq, npages = page_table.shape
    steps = npages // PAGES_PER_STEP
    assert steps * PAGES_PER_STEP == npages and nseq * steps >= PAGE_SLOTS - 1
    page_buf = pltpu.VMEM((PAGE_SLOTS, PAGES_PER_STEP, PAGE_SIZE * ATT_HEADS, ATT_HEAD_DIM), F32)

    grid_spec = pltpu.PrefetchScalarGridSpec(
        num_scalar_prefetch=1,
        grid=(nseq, steps),
        in_specs=[pl.BlockSpec((1, SAMPLE_ROWS, D_ATT), lambda b, j, pt: (b, 0, 0)),
                  pl.BlockSpec((1, SAMPLE_ROWS, D_ATT), lambda b, j, pt: (b, 0, 1)),
                  pl.BlockSpec((1, SAMPLE_ROWS, D_ATT), lambda b, j, pt: (b, 0, 2)),
                  pl.BlockSpec(bias_rows.shape, lambda b, j, pt: (0, 0)),
                  pl.BlockSpec(uo.shape, lambda b, j, pt: (0, 0)),
                  pl.BlockSpec(memory_space=pl.ANY), pl.BlockSpec(memory_space=pl.ANY)],
        out_specs=pl.BlockSpec((1, SAMPLE_ROWS, D_ATT), lambda b, j, pt: (b, 0, 0)),
        scratch_shapes=[pltpu.VMEM((ATT_HEADS * SAMPLE_ROWS, D_ATT), BF16),
                        pltpu.VMEM((PAGE_SIZE, D_ATT), F32),
                        pltpu.VMEM((PAGE_SIZE, D_ATT), F32),
                        pltpu.VMEM((ATT_HEADS * SAMPLE_ROWS, D_ATT), F32),
                        pltpu.VMEM((ATT_HEADS * SAMPLE_ROWS, LANES), F32),
                        page_buf, page_buf, pltpu.SemaphoreType.DMA((2, PAGE_SLOTS))])
    return pl.pallas_call(
        functools.partial(_attn_sample_kernel, nseq, steps),
        grid_spec=grid_spec,
        out_shape=jax.ShapeDtypeStruct((nseq, SAMPLE_ROWS, D_ATT), F32),
        compiler_params=pltpu.CompilerParams(dimension_semantics=("arbitrary", "arbitrary"),
                                             vmem_limit_bytes=VMEM_LIMIT),
        name="attn_sample",
    )(page_table, projs3, projs3, projs3, bias_rows, uo, cache_k, cache_v)


def _outproj_kernel(y_ref, o_ref, x_ref, g1_ref, sh2_ref, sc2_ref, gpost_ref, gpre_ref, wy_ref, wo_ref, x1_ref, h2_ref):
    tm = x_ref.shape[0]
    chunk = min(tm, 256)
    for r in range(0, tm, chunk):
        rows = slice(r, r + chunk)
        g1, sh2, sc2 = (v if v.shape[0] == 1 else v[rows] for v in (g1_ref[0], sh2_ref[0], sc2_ref[0]))
        m = _dot(y_ref[rows, :], wy_ref[...]) + _dot(o_ref[rows, :].astype(BF16), wo_ref[...])
        x1 = x_ref[rows, :] + g1 * _rms(m, gpost_ref[...])
        x1_ref[rows, :] = x1
        h2 = _rms(x1, gpre_ref[...]) * (1.0 + sc2) + sh2
        h2_ref[rows, :] = h2.astype(BF16)


def _outproj(y, o, x, mods, gpost, gpre, w_out, tm, tiles_per_group):
    m = x.shape[0]
    rows = mods.shape[1]
    half = w_out.shape[0] // 2
    row = lambda i: (i, 0)
    return pl.pallas_call(
        _outproj_kernel,
        grid=(m // tm,),
        in_specs=[pl.BlockSpec((tm, D_SSD), row), pl.BlockSpec((tm, D_ATT), row), pl.BlockSpec((tm, D_MODEL), row),
                  _mod_spec(2, rows, tiles_per_group), _mod_spec(3, rows, tiles_per_group),
                  _mod_spec(4, rows, tiles_per_group),
                  _const_spec(gpost.shape), _const_spec(gpre.shape),
                  pl.BlockSpec((half, D_MODEL), lambda i: (0, 0)),
                  pl.BlockSpec((half, D_MODEL), lambda i: (1, 0))],
        out_specs=[pl.BlockSpec((tm, D_MODEL), row), pl.BlockSpec((tm, D_MODEL), row)],
        out_shape=[jax.ShapeDtypeStruct((m, D_MODEL), F32), jax.ShapeDtypeStruct((m, D_MODEL), BF16)],
        compiler_params=pltpu.CompilerParams(dimension_semantics=("arbitrary",), vmem_limit_bytes=VMEM_LIMIT),
        name="out_proj",
    )(y, o, x, mods, mods, mods, gpost, gpre, w_out, w_out)


FFN_CHUNK = 256


def _ffn_kernel(sample, tiles_per_group, *refs):
    if sample:
        (h_ref, x1_hbm, g2_ref, gpost_ref, wg_ref, wv_ref, wo_ref, cwg_ref, cwv_ref, cbg_ref, cbv_ref,
         injg_ref, injv_ref, y_ref, ug_ref, uv_ref, x1_buf, x1_sem) = refs
    else:
        (h_ref, x1_hbm, g2_ref, gpost_ref, wg_ref, wv_ref, wo_ref, cwg_ref, cwv_ref, cbg_ref, cbv_ref,
         y_ref, ug_ref, uv_ref, x1_buf, x1_sem, halog, halov) = refs
    i = pl.program_id(0)
    j = pl.program_id(1)
    tm = h_ref.shape[0]
    tf = wg_ref.shape[1]
    hn = h_ref[...]
    x1_copy = pltpu.make_async_copy(x1_hbm.at[pl.ds(pl.multiple_of(i * tm, tm), tm), :], x1_buf, x1_sem)

    @pl.when(j == 0)
    def _():
        x1_copy.start()
        y_ref[...] = jnp.zeros_like(y_ref)

    if sample:
        is_ctx = (lax.broadcasted_iota(jnp.int32, (tm, 1), 0) % SAMPLE_ROWS) >= SAMPLE_ROWS - (FFN_CONV - 1)
    else:
        @pl.when((i == 0) & (j == 0))
        def _():
            halog[...] = jnp.zeros_like(halog)
            halov[...] = jnp.zeros_like(halov)

        first = lax.rem(i, tiles_per_group) == 0

    def half(w_ref, cw_ref, cb_ref, u_ref, inj_ref, halo, cols):
        u = jnp.concatenate([_dot(hn[r:r + 128], w_ref[:, cols]) for r in range(0, tm, 128)], axis=0)
        if sample:
            u_ref[:, cols] = u
            prev = inj_ref[0:8, cols]
            u = jnp.where(is_ctx, inj_ref[8:, cols], u)
        else:
            prev = jnp.where(first, 0.0, halo[j, :, cols])
            halo[j, :, cols] = u[tm - 8:, :]
            u_ref[0, :, cols] = u[tm - 8:, :]
        return _conv_rows(u, prev, cw_ref[:, cols], cb_ref[:, cols])

    acts = []
    for c in range(tf // FFN_CHUNK):
        cols = slice(c * FFN_CHUNK, (c + 1) * FFN_CHUNK)
        gate = half(wg_ref, cwg_ref, cbg_ref, ug_ref, None if not sample else injg_ref, None if sample else halog, cols)
        val = half(wv_ref, cwv_ref, cbv_ref, uv_ref, None if not sample else injv_ref, None if sample else halov, cols)
        acts.append((_silu(gate) * val).astype(BF16))
    y_ref[...] += _dot(jnp.concatenate(acts, axis=1), wo_ref[...])

    @pl.when(j == pl.num_programs(1) - 1)
    def _():
        x1_copy.wait()
        chunk = min(tm, 256)
        for r in range(0, tm, chunk):
            rows = slice(r, r + chunk)
            g2 = g2_ref[0]
            g2 = g2 if g2.shape[0] == 1 else g2[rows]
            y_ref[rows, :] = x1_buf[rows, :] + g2 * _rms(y_ref[rows, :], gpost_ref[...])


def _ffn(h2, x1, mods, gpost, w_in, w_out, cw, cb, tm, tiles_per_group, inj=None):
    m = h2.shape[0]
    rows = mods.shape[1]
    tf = 512
    nf = D_FF // tf
    sample = inj is not None
    row = lambda i, j: (i, 0)
    gate_col = lambda i, j: (0, j)
    val_col = lambda i, j: (0, nf + j)
    in_specs = [pl.BlockSpec((tm, D_MODEL), row), pl.BlockSpec(memory_space=pl.ANY),
                _mod_spec(5, rows, tiles_per_group), _const_spec(gpost.shape),
                pl.BlockSpec((D_MODEL, tf), gate_col), pl.BlockSpec((D_MODEL, tf), val_col),
                pl.BlockSpec((tf, D_MODEL), lambda i, j: (j, 0)),
                pl.BlockSpec((FFN_CONV, tf), gate_col), pl.BlockSpec((FFN_CONV, tf), val_col),
                pl.BlockSpec((1, tf), gate_col), pl.BlockSpec((1, tf), val_col)]
    args = [h2, x1, mods, gpost, w_in, w_in, w_out, cw, cw, cb, cb]
    scratch = [pltpu.VMEM((tm, D_MODEL), F32), pltpu.SemaphoreType.DMA(())]
    if sample:
        in_specs += [pl.BlockSpec((m + 8, tf), gate_col), pl.BlockSpec((m + 8, tf), val_col)]
        args += [inj, inj]
        u_specs = [pl.BlockSpec((tm, tf), lambda i, j: (i, j))] * 2
        u_shapes = [jax.ShapeDtypeStruct((m, D_FF), F32)] * 2
    else:
        u_specs = [pl.BlockSpec((1, 8, tf), lambda i, j: (i, 0, j))] * 2
        u_shapes = [jax.ShapeDtypeStruct((m // tm, 8, D_FF), F32)] * 2
        scratch += [pltpu.VMEM((nf, 8, tf), F32), pltpu.VMEM((nf, 8, tf), F32)]
    return pl.pallas_call(
        functools.partial(_ffn_kernel, sample, tiles_per_group),
        grid=(m // tm, nf),
        in_specs=in_specs,
        out_specs=[pl.BlockSpec((tm, D_MODEL), row)] + u_specs,
        out_shape=[jax.ShapeDtypeStruct((m, D_MODEL), F32)] + u_shapes,
        scratch_shapes=scratch,
        compiler_params=pltpu.CompilerParams(dimension_semantics=("arbitrary", "arbitrary"),
                                             vmem_limit_bytes=VMEM_LIMIT),
        name="ffn_sample" if sample else "ffn_prompt",
    )(*args)


def _ssd_constants(block_rows):
    r = jnp.arange(SSD_CHUNK)[:, None]
    c = jnp.arange(SSD_CHUNK)[None, :]
    same = (r // block_rows) == (c // block_rows)
    tri = ((c <= r) & same).astype(BF16)
    sel = (c == (r // block_rows) * block_rows + block_rows - 1).astype(BF16)
    hrow = jnp.arange(LANES)[:, None]
    e1 = (hrow == jnp.arange(D_SSD)[None, :] // SSD_HEAD_DIM).astype(BF16)
    e2 = (hrow == jnp.arange(SSD_HEADS * LANES)[None, :] // LANES).astype(BF16)
    return tri, sel, jnp.concatenate([e1, e2], axis=1)


def _suffix_matrix():
    j = jnp.arange(LANES)[:, None]
    s = jnp.arange(LANES)[None, :]
    u = jnp.concatenate([(j > s).astype(BF16), jnp.ones((LANES, LANES), BF16)], axis=1)
    return jnp.concatenate([u, u], axis=0)


def _inject_rows(ctx, nctx):
    nseq, _, c = ctx.shape
    body = jnp.concatenate([ctx[1:], jnp.zeros((1, nctx, c), F32)], axis=0)
    body = jnp.pad(body, ((0, 0), (SAMPLE_ROWS - nctx, 0), (0, 0))).reshape(nseq * SAMPLE_ROWS, c)
    head = jnp.pad(ctx[0], ((8 - nctx, 0), (0, 0)))
    return jnp.concatenate([head, body], axis=0)


def kernel(x_prompt, x_sample, c_prompt, c_sample, cache_k, cache_v, page_table, state_ssm, state_conv, state_ffn_conv, w_ada, b_ada, g_pre_mix, g_post_mix, g_pre_ffn, g_post_ffn, w_in, conv_w, conv_b, dt_bias, a_log, d_skip, g_ssd_norm, sb_bias, w_out, w_ffn_in, ffn_conv_w, ffn_conv_b, w_ffn_out):
    assert w_ada.shape[0] == 1, "one layer"
    nb, seq, d = x_prompt.shape
    ns, ls, _ = x_sample.shape
    n_pool = cache_k.shape[1]
    mp = nb * seq
    ms = ns * SAMPLE_ROWS

    w = w_in[0].astype(BF16)
    o_bc, o_q = 2 * D_SSD, 2 * D_SSD + BC_DIM + SSD_HEADS
    w_zx = w
    w_qkv = w[:, o_q:]
    w_tail = jnp.pad(w[:, o_bc:o_q], ((0, 0), (0, DT_PAD - SSD_HEADS)))
    row = lambda v: v.reshape(1, -1)
    dtb = jnp.pad(dt_bias[0], (0, DT_PAD - SSD_HEADS)).reshape(1, DT_PAD)
    alog = jnp.pad(a_log[0], (0, DT_PAD - SSD_HEADS)).reshape(1, DT_PAD)
    dskip = jnp.repeat(d_skip[0], SSD_HEAD_DIM).reshape(1, D_SSD)
    gnorm = row(g_ssd_norm[0])
    cw, cb = conv_w[0], row(conv_b[0])
    fcw, fcb = ffn_conv_w[0], row(ffn_conv_b[0])
    uo = _suffix_matrix()

    n_c = nb + ns
    c_all = jnp.concatenate([c_prompt, c_sample, jnp.zeros((-n_c % 16, d), F32)], axis=0)
    mod = _ada(c_all, w_ada[0], row(b_ada[0]))
    mods_p = mod[:nb].reshape(nb, 1, 6 * d)
    mods_s = jnp.repeat(mod[nb:n_c], SAMPLE_ROWS, axis=0).reshape(1, ms, 6 * d)

    tm_p = 512
    tpg_p = seq // tm_p
    xp = x_prompt.reshape(mp, d)
    hn_p, tail_p = _prenorm(xp, mods_p, row(g_pre_mix[0]), w_tail, tm_p, tpg_p)
    zx_p, qkv_p, k_p, v_p = _inproj(hn_p, w_zx, w_qkv, 1024, BF16)
    zx3 = zx_p.reshape(nb, seq, 2 * D_SSD)
    tail3 = tail_p.reshape(nb, seq, PROJ_TAIL)
    tri, sel, emat = _ssd_constants(SSD_CHUNK)
    y_ssd, ssm_p = _ssd_prompt(zx3, tail3, cw, cb, dtb, alog, dskip, gnorm, tri, sel, emat)
    o_att, w_out_b, w_ffn_in_b, w_ffn_out_b = _attn_prompt(qkv_p.reshape(nb, seq, 3 * D_ATT), sb_bias[0], uo,
                                                           [w_out[0], w_ffn_in[0], w_ffn_out[0]])
    tm_e = 512
    x1_p, h2_p = _outproj(y_ssd.reshape(mp, D_SSD), o_att.reshape(mp, D_ATT), xp, mods_p, row(g_post_mix[0]),
                          row(g_pre_ffn[0]), w_out_b, tm_e, seq // tm_e)
    tm_f = 1024
    yp, ctxg_p, ctxv_p = _ffn(h2_p, x1_p, mods_p, row(g_post_ffn[0]), w_ffn_in_b, w_ffn_out_b, fcw, fcb,
                              tm_f, seq // tm_f)

    xs_pad = jnp.pad(x_sample, ((0, 0), (0, SAMPLE_ROWS - ls), (0, 0))).reshape(ms, d)
    hn_s, tail_s = _prenorm(xs_pad, mods_s, row(g_pre_mix[0]), w_tail, ms, 1)
    zx_s, qkv_s, k_s, v_s = _inproj(hn_s, w_zx, w_qkv, ms, F32)
    seq_per_tile = SSD_CHUNK // SAMPLE_ROWS
    conv_ctx = state_conv[0].reshape(ns // seq_per_tile, seq_per_tile, SSD_CONV - 1, -1)
    ctx_inj = jax.vmap(lambda c: _inject_rows(c, SSD_CONV - 1))(conv_ctx)
    tri_s, sel_s, _ = _ssd_constants(SAMPLE_ROWS)
    y_ssd_s, ssm_s = _ssd_sample(zx_s, tail_s, ctx_inj, state_ssm[0].reshape(ns, D_SSD, SSD_STATE), cw, cb, dtb, alog,
                                 dskip, gnorm, tri_s, sel_s, emat)
    bias_rows = jnp.broadcast_to(jnp.repeat(sb_bias[0], SAMPLE_ROWS)[:, None], (ATT_HEADS * SAMPLE_ROWS, LANES))
    o_att_s = _attn_sample(page_table, qkv_s.reshape(ns, SAMPLE_ROWS, 3 * D_ATT),
                           cache_k[0].reshape(n_pool, PAGE_SIZE * ATT_HEADS, ATT_HEAD_DIM),
                           cache_v[0].reshape(n_pool, PAGE_SIZE * ATT_HEADS, ATT_HEAD_DIM), bias_rows, uo)
    x1_s, h2_s = _outproj(y_ssd_s, o_att_s.reshape(ms, D_ATT), xs_pad, mods_s, row(g_post_mix[0]),
                          row(g_pre_ffn[0]), w_out_b, ms, 1)
    inj_f = _inject_rows(state_ffn_conv[0], FFN_CONV - 1)
    ys_pad, ug_s, uv_s = _ffn(h2_s, x1_s, mods_s, row(g_post_ffn[0]), w_ffn_in_b, w_ffn_out_b, fcw, fcb,
                              ms, 1, inj=inj_f)

    heads = (ATT_HEADS, ATT_HEAD_DIM)
    k_p = k_p.reshape(1, nb, seq, *heads)
    v_p = v_p.reshape(1, nb, seq, *heads)
    tail = slice(seq - (SSD_CONV - 1), seq)
    conv_p = jnp.concatenate([zx3[:, tail, D_SSD:], tail3[:, tail, :BC_DIM]], axis=-1)[None]
    ffn_p = jnp.concatenate([ctxg_p, ctxv_p], axis=-1)[seq // tm_f - 1::seq // tm_f, 8 - (FFN_CONV - 1):][None]
    k_s = k_s.reshape(ns, SAMPLE_ROWS, *heads)[None, :, :ls]
    v_s = v_s.reshape(ns, SAMPLE_ROWS, *heads)[None, :, :ls]
    raw_s = jnp.concatenate([zx_s.reshape(ns, SAMPLE_ROWS, -1)[:, :ls, D_SSD:],
                             tail_s.reshape(ns, SAMPLE_ROWS, -1)[:, :ls, :BC_DIM]], axis=-1)
    conv_s = jnp.concatenate([state_conv[0], raw_s], axis=1)[:, -(SSD_CONV - 1):][None]
    keep = FFN_CONV - 1
    u_rows = [u.reshape(ns, SAMPLE_ROWS, D_FF)[:, max(ls - keep, 0):ls] for u in (ug_s, uv_s)]
    ffn_s = jnp.concatenate([state_ffn_conv[0][:, ls:], jnp.concatenate(u_rows, axis=-1)], axis=1)[None]
    return (yp.reshape(nb, seq, d), ys_pad.reshape(ns, SAMPLE_ROWS, d)[:, :ls],
            k_p, v_p, ssm_p.reshape(1, nb, SSD_HEADS, SSD_HEAD_DIM, SSD_STATE), conv_p, ffn_p,
            k_s, v_s, ssm_s.reshape(1, ns, SSD_HEADS, SSD_HEAD_DIM, SSD_STATE), conv_s, ffn_s)
```

```python
import functools

import jax
import jax.numpy as jnp
from jax import lax
from jax.experimental import pallas as pl
from jax.experimental.pallas import tpu as pltpu

F32 = jnp.float32
BF16 = jnp.bfloat16

D_MODEL = 2048
D_SSD = 1024
SSD_HEAD_DIM = 64
SSD_HEADS = 16
SSD_GROUPS = 2
SSD_STATE = 128
SSD_CONV = 4
SSD_CHUNK = 128
SSD_STEP_CHUNKS = 4
SSD_SAMPLE_SEQS = 2
BC_DIM = 2 * SSD_GROUPS * SSD_STATE
D_ATT = 1024
ATT_HEAD_DIM = 128
ATT_HEADS = 8
D_FF = 5632
FFN_CONV = 3
PAGE_SIZE = 128
EPS = 1e-6
ATT_SCALE = ATT_HEAD_DIM ** -0.5

PROJ_TN = 1024
PROJ_MAIN = 5 * PROJ_TN
DT_PAD = 128
PROJ_TAIL = BC_DIM + DT_PAD

SAMPLE_ROWS = 8
LANES = 128
VMEM_LIMIT = 56 * 1024 * 1024


def _dot(a, b):
    return jnp.dot(a, b, preferred_element_type=F32)


def _dot_nt(a, b):
    return lax.dot_general(a, b, (((1,), (1,)), ((), ())), preferred_element_type=F32)


def _sigmoid(x):
    return 1.0 / (1.0 + jnp.exp(-x))


def _silu(x):
    return x * _sigmoid(x)


def _softplus(x):
    return jnp.maximum(x, 0.0) + jnp.log1p(jnp.exp(-jnp.abs(x)))


def _rms(x, g):
    ms = jnp.mean(x * x, axis=-1, keepdims=True)
    return x * lax.rsqrt(ms + EPS) * g


def _split3(x):
    h = x.astype(BF16)
    r = x - h.astype(F32)
    m = r.astype(BF16)
    l = (r - m.astype(F32)).astype(BF16)
    return h, m, l


def _dot3_right(x, mat):
    return _dot(jnp.concatenate(_split3(x), axis=1), jnp.concatenate([mat] * 3, axis=0))


def _dot3_left(mat, x):
    return _dot(jnp.concatenate([mat] * 3, axis=1), jnp.concatenate(_split3(x), axis=0))


def _ada_kernel(c_ref, w_ref, b_ref, o_ref):
    s = _silu(c_ref[...]).astype(BF16)
    o_ref[...] = _dot(s, w_ref[...].astype(BF16)) + b_ref[...]


def _ada(c_all, w_ada, b_ada):
    m, d = c_all.shape
    n = w_ada.shape[1]
    tn = 1024
    return pl.pallas_call(
        _ada_kernel,
        grid=(n // tn,),
        in_specs=[pl.BlockSpec((m, d), lambda j: (0, 0)),
                  pl.BlockSpec((d, tn), lambda j: (0, j)),
                  pl.BlockSpec((1, tn), lambda j: (0, j))],
        out_specs=pl.BlockSpec((m, tn), lambda j: (0, j)),
        out_shape=jax.ShapeDtypeStruct((m, n), F32),
        compiler_params=pltpu.CompilerParams(dimension_semantics=("arbitrary",), vmem_limit_bytes=VMEM_LIMIT),
        name="ada_mod",
    )(c_all, w_ada, b_ada)


def _mod_spec(idx, rows, tiles_per_group):
    return pl.BlockSpec((1, rows, D_MODEL), lambda i, *_: (i // tiles_per_group, 0, idx))


def _prenorm_kernel(x_ref, sh_ref, sc_ref, g_ref, wt_ref, hn_ref, tail_ref):
    tm = x_ref.shape[0]
    chunk = min(tm, 128)
    g = g_ref[...]
    for r in range(0, tm, chunk):
        sc, sh = sc_ref[0], sh_ref[0]
        if sc.shape[0] != 1:
            sc, sh = sc[r:r + chunk], sh[r:r + chunk]
        h = (_rms(x_ref[r:r + chunk, :], g) * (1.0 + sc) + sh).astype(BF16)
        hn_ref[r:r + chunk, :] = h
        tail_ref[r:r + chunk, :] = _dot(h, wt_ref[...])


def _prenorm(x, mods, g, wt, tm, tiles_per_group):
    m = x.shape[0]
    rows = mods.shape[1]
    return pl.pallas_call(
        _prenorm_kernel,
        grid=(m // tm,),
        in_specs=[pl.BlockSpec((tm, D_MODEL), lambda i: (i, 0)),
                  _mod_spec(0, rows, tiles_per_group),
                  _mod_spec(1, rows, tiles_per_group),
                  _const_spec(g.shape), _const_spec(wt.shape)],
        out_specs=[pl.BlockSpec((tm, D_MODEL), lambda i: (i, 0)), pl.BlockSpec((tm, PROJ_TAIL), lambda i: (i, 0))],
        out_shape=[jax.ShapeDtypeStruct((m, D_MODEL), BF16), jax.ShapeDtypeStruct((m, PROJ_TAIL), F32)],
        compiler_params=pltpu.CompilerParams(dimension_semantics=("arbitrary",), vmem_limit_bytes=VMEM_LIMIT),
        name="pre_norm",
    )(x, mods, mods, g, wt)


def _inproj_kernel(hn_ref, wzx_ref, w_ref, zx_ref, qkv_ref, k_ref, v_ref):
    j = pl.program_id(1)
    tm = hn_ref.shape[0]

    def project(state_ref):
        res = _dot(hn_ref[...], w_ref[...])
        qkv_ref[...] = res.astype(qkv_ref.dtype)
        if state_ref is not None:
            for h in range(ATT_HEADS):
                state_ref[pl.ds(h, tm, stride=ATT_HEADS), :] = res[:, h * ATT_HEAD_DIM:(h + 1) * ATT_HEAD_DIM]

    @pl.when(j < 2)
    def _():
        zx_ref[...] = _dot(hn_ref[...], wzx_ref[...])

    @pl.when(j == 2)
    def _():
        project(None)

    @pl.when(j == 3)
    def _():
        project(k_ref)

    @pl.when(j == 4)
    def _():
        project(v_ref)


def _inproj(hn, w_zx, w_qkv, tm, qkv_dtype):
    m = hn.shape[0]
    kv_spec = pl.BlockSpec((tm * ATT_HEADS, ATT_HEAD_DIM), lambda i, j: (i, 0))
    kv_shape = jax.ShapeDtypeStruct((m * ATT_HEADS, ATT_HEAD_DIM), F32)
    return pl.pallas_call(
        _inproj_kernel,
        grid=(m // tm, PROJ_MAIN // PROJ_TN),
        in_specs=[pl.BlockSpec((tm, D_MODEL), lambda i, j: (i, 0)),
                  pl.BlockSpec((D_MODEL, PROJ_TN), lambda i, j: (0, jnp.minimum(j, 1))),
                  pl.BlockSpec((D_MODEL, PROJ_TN), lambda i, j: (0, jnp.clip(j - 2, 0, 2)))],
        out_specs=[pl.BlockSpec((tm, PROJ_TN), lambda i, j: (i, jnp.minimum(j, 1))),
                   pl.BlockSpec((tm, PROJ_TN), lambda i, j: (i, jnp.clip(j - 2, 0, 2))),
                   kv_spec, kv_spec],
        out_shape=[jax.ShapeDtypeStruct((m, 2 * PROJ_TN), F32),
                   jax.ShapeDtypeStruct((m, 3 * PROJ_TN), qkv_dtype), kv_shape, kv_shape],
        compiler_params=pltpu.CompilerParams(dimension_semantics=("arbitrary", "arbitrary"),
                                             vmem_limit_bytes=VMEM_LIMIT),
        name="in_proj",
    )(hn, w_zx, w_qkv)


def _shift_rows(u, prev8, s):
    r = pltpu.roll(u, s, 0)
    head = jnp.where(lax.broadcasted_iota(jnp.int32, (8, 1), 0) < s, pltpu.roll(prev8, s, 0), r[:8])
    return jnp.concatenate([head, r[8:]], axis=0)


def _conv_rows(u, prev8, w, b):
    taps = w.shape[0]
    y = _shift_rows(u, prev8, taps - 1) * w[0:1]
    for t in range(1, taps):
        s = taps - 1 - t
        y = y + (_shift_rows(u, prev8, s) if s else u) * w[t:t + 1]
    return y + b


def _ssd_tile(xs, bc, dtr, row_valid, causal, tri, sel, emat, dtb, alog):
    dt = _softplus(dtr + dtb)
    if row_valid is not None:
        dt = jnp.where(row_valid, dt, 0.0)
    a = -jnp.exp(alog)
    acum = _dot3_left(tri, dt * a)
    alast = _dot3_left(sel, acum)
    acx = _dot3_right(acum, emat)
    acum_x = acx[:, :D_SSD]
    acum_x2 = acx[:, D_SSD:]
    e1 = emat[:, :D_SSD]
    dt_x = _dot3_right(dt, e1)
    alast_x = _dot3_right(alast, e1)
    acum_t = acum.T

    xdt = xs * dt_x
    xdt_bf = xdt.astype(BF16)
    xdtd_bf = (xdt * jnp.exp(alast_x - acum_x)).astype(BF16)
    bm = bc[:, :SSD_GROUPS * SSD_STATE]
    cm = bc[:, SSD_GROUPS * SSD_STATE:]
    lane = lax.broadcasted_iota(jnp.int32, (SSD_CHUNK, LANES), 1)
    first_head = lane < SSD_HEAD_DIM

    pieces = []
    heads_per_group = SSD_HEADS // SSD_GROUPS
    for g in range(SSD_GROUPS):
        bg = bm[:, g * SSD_STATE:(g + 1) * SSD_STATE].astype(BF16)
        cg = cm[:, g * SSD_STATE:(g + 1) * SSD_STATE].astype(BF16)
        cb = _dot_nt(cg, bg)
        for pair in range(heads_per_group // 2):
            h0 = g * heads_per_group + 2 * pair
            ys = []
            for hh in (h0, h0 + 1):
                seg = acum_x2[:, hh * LANES:(hh + 1) * LANES] - acum_t[hh:hh + 1, :]
                decay = jnp.where(causal, jnp.exp(jnp.where(causal, seg, 0.0)), 0.0)
                sc = (cb * decay).astype(BF16)
                ys.append(_dot(sc, xdt_bf[:, (h0 // 2) * LANES:(h0 // 2 + 1) * LANES]))
            pieces.append(jnp.where(first_head, ys[0], ys[1]))
    y_diag = jnp.concatenate(pieces, axis=1)
    return y_diag, xdtd_bf, acum_x, bm, cm


def _ssd_finish(y, xs, z, dskip, gnorm):
    y = y + dskip * xs
    y = y * _silu(z)
    half = D_SSD // SSD_GROUPS
    outs = []
    for g in range(SSD_GROUPS):
        outs.append(_rms(y[:, g * half:(g + 1) * half], gnorm[:, g * half:(g + 1) * half]))
    return jnp.concatenate(outs, axis=1)


def _ssd_prompt_kernel(z_ref, xs_ref, bc_ref, dt_ref, cw_ref, cb_ref, dtb_ref, alog_ref, dskip_ref, gnorm_ref,
                       tri_ref, sel_ref, emat_ref, y_ref, st_ref, ht_ref, prev_xs, prev_bc):
    c = pl.program_id(1)
    rows = SSD_CHUNK

    @pl.when(c == 0)
    def _():
        ht_ref[...] = jnp.zeros_like(ht_ref)
        prev_xs[...] = jnp.zeros_like(prev_xs)
        prev_bc[...] = jnp.zeros_like(prev_bc)

    cw = cw_ref[...]
    cb = cb_ref[...]
    r = lax.broadcasted_iota(jnp.int32, (rows, rows), 0)
    s = lax.broadcasted_iota(jnp.int32, (rows, rows), 1)
    causal = s <= r
    half = D_SSD // SSD_GROUPS
    px = prev_xs[...]
    pb = prev_bc[...]
    ht_new = ht_ref[...]
    for sub in range(xs_ref.shape[1] // rows):
        rs = slice(sub * rows, (sub + 1) * rows)
        xr = xs_ref[0, rs, :]
        br = bc_ref[0, rs, :]
        xs = _silu(_conv_rows(xr, px, cw[:, :D_SSD], cb[:, :D_SSD]))
        bc = _silu(_conv_rows(br, pb, cw[:, D_SSD:], cb[:, D_SSD:]))
        px = xr[rows - 8:, :]
        pb = br[rows - 8:, :]
        y_diag, xdtd_bf, acum_x, bm, cm = _ssd_tile(xs, bc, dt_ref[0, rs, :], None, causal, tri_ref[...], sel_ref[...],
                                                     emat_ref[...], dtb_ref[...], alog_ref[...])
        ht = ht_new
        y_off, states = [], []
        for g in range(SSD_GROUPS):
            bg = bm[:, g * SSD_STATE:(g + 1) * SSD_STATE]
            cg = cm[:, g * SSD_STATE:(g + 1) * SSD_STATE].astype(BF16)
            y_off.append(_dot(cg, ht[:, g * half:(g + 1) * half].astype(BF16)))
            states.append(_dot(bg.T.astype(BF16), xdtd_bf[:, g * half:(g + 1) * half]))
        y = y_diag + jnp.concatenate(y_off, axis=1) * jnp.exp(acum_x)
        ht_new = jnp.exp(acum_x[rows - 1:rows, :]) * ht + jnp.concatenate(states, axis=1)
        y_ref[0, rs, :] = _ssd_finish(y, xs, z_ref[0, rs, :], dskip_ref[...], gnorm_ref[...]).astype(BF16)
    prev_xs[...] = px
    prev_bc[...] = pb
    ht_ref[...] = ht_new

    @pl.when(c == pl.num_programs(1) - 1)
    def _():
        for k in range(D_SSD // LANES):
            st_ref[0, k * LANES:(k + 1) * LANES, :] = ht_new[:, k * LANES:(k + 1) * LANES].T


def _const_spec(shape):
    nd = len(shape)
    return pl.BlockSpec(shape, lambda *_: (0,) * nd)


def _ssd_prompt(zx3, tail3, cw, cb, dtb, alog, dskip, gnorm, tri, sel, emat):
    nb, seq, _ = zx3.shape
    step = SSD_STEP_CHUNKS * SSD_CHUNK
    nc = seq // step
    return pl.pallas_call(
        _ssd_prompt_kernel,
        grid=(nb, nc),
        in_specs=[pl.BlockSpec((1, step, D_SSD), lambda b, c: (b, c, 0)),
                  pl.BlockSpec((1, step, D_SSD), lambda b, c: (b, c, 1)),
                  pl.BlockSpec((1, step, BC_DIM), lambda b, c: (b, c, 0)),
                  pl.BlockSpec((1, step, DT_PAD), lambda b, c: (b, c, BC_DIM // DT_PAD)),
                  _const_spec(cw.shape), _const_spec(cb.shape), _const_spec(dtb.shape), _const_spec(alog.shape),
                  _const_spec(dskip.shape), _const_spec(gnorm.shape), _const_spec(tri.shape),
                  _const_spec(sel.shape), _const_spec(emat.shape)],
        out_specs=[pl.BlockSpec((1, step, D_SSD), lambda b, c: (b, c, 0)),
                   pl.BlockSpec((1, D_SSD, SSD_STATE), lambda b, c: (b, 0, 0))],
        out_shape=[jax.ShapeDtypeStruct((nb, seq, D_SSD), BF16),
                   jax.ShapeDtypeStruct((nb, D_SSD, SSD_STATE), F32)],
        scratch_shapes=[pltpu.VMEM((SSD_STATE, D_SSD), F32),
                        pltpu.VMEM((8, D_SSD), F32),
                        pltpu.VMEM((8, BC_DIM), F32)],
        compiler_params=pltpu.CompilerParams(dimension_semantics=("arbitrary", "arbitrary"),
                                             vmem_limit_bytes=VMEM_LIMIT),
        name="ssd_prompt",
    )(zx3, zx3, tail3, tail3, cw, cb, dtb, alog, dskip, gnorm, tri, sel, emat)


def _ssd_sample_kernel(z_ref, xs_ref, bc_ref, dt_ref, ctx_ref, h0_ref, cw_ref, cb_ref, dtb_ref, alog_ref, dskip_ref,
                       gnorm_ref, tri_ref, sel_ref, emat_ref, y_ref, st_ref,
                       ybase, yoff, xs_s, xdtd_s, acumx_s, bm_s, cm_s):
    s_id = pl.program_id(1)
    rows = SSD_CHUNK
    nseq = rows // SAMPLE_ROWS
    rr = lax.broadcasted_iota(jnp.int32, (rows, 1), 0)

    @pl.when(s_id == 0)
    def _():
        ctx = ctx_ref[0]
        is_ctx = (rr % SAMPLE_ROWS) >= SAMPLE_ROWS - (SSD_CONV - 1)
        xr = jnp.where(is_ctx, ctx[8:, :D_SSD], xs_ref[...])
        br = jnp.where(is_ctx, ctx[8:, D_SSD:], bc_ref[...])
        cw = cw_ref[...]
        cb = cb_ref[...]
        xs = _silu(_conv_rows(xr, ctx[0:8, :D_SSD], cw[:, :D_SSD], cb[:, :D_SSD]))
        bc = _silu(_conv_rows(br, ctx[0:8, D_SSD:], cw[:, D_SSD:], cb[:, D_SSD:]))
        r = lax.broadcasted_iota(jnp.int32, (rows, rows), 0)
        c = lax.broadcasted_iota(jnp.int32, (rows, rows), 1)
        causal = (c <= r) & ((c // SAMPLE_ROWS) == (r // SAMPLE_ROWS))
        row_valid = (rr % SAMPLE_ROWS) < (SAMPLE_ROWS // 2)
        y_diag, xdtd_bf, acum_x, bm, cm = _ssd_tile(xs, bc, dt_ref[...], row_valid, causal, tri_ref[...],
                                                     sel_ref[...], emat_ref[...], dtb_ref[...], alog_ref[...])
        ybase[...] = y_diag
        yoff[...] = jnp.zeros_like(yoff)
        xs_s[...] = xs
        xdtd_s[...] = xdtd_bf
        acumx_s[...] = acum_x
        bm_s[...] = bm
        cm_s[...] = cm

    half = D_SSD // SSD_GROUPS
    add = None
    for q in range(SSD_SAMPLE_SEQS):
        seq = s_id * SSD_SAMPLE_SEQS + q
        in_seq_rows = (rr // SAMPLE_ROWS) == seq
        in_seq_lanes = (lax.broadcasted_iota(jnp.int32, (1, rows), 1) // SAMPLE_ROWS) == seq
        h0 = h0_ref[q]
        ht = jnp.concatenate([h0[k * LANES:(k + 1) * LANES, :].T for k in range(D_SSD // LANES)], axis=1)
        y_off, states = [], []
        for g in range(SSD_GROUPS):
            bg_t = jnp.where(in_seq_lanes, bm_s[:, g * SSD_STATE:(g + 1) * SSD_STATE].T, 0.0).astype(BF16)
            cg = cm_s[:, g * SSD_STATE:(g + 1) * SSD_STATE].astype(BF16)
            y_off.append(_dot(cg, ht[:, g * half:(g + 1) * half].astype(BF16)))
            states.append(_dot(bg_t, xdtd_s[:, g * half:(g + 1) * half]))
        part = jnp.where(in_seq_rows, jnp.concatenate(y_off, axis=1), 0.0)
        add = part if add is None else add + part
        last = pl.multiple_of(seq * SAMPLE_ROWS, SAMPLE_ROWS) + (SAMPLE_ROWS - 1)
        ht_new = jnp.exp(acumx_s[pl.ds(last, 1), :]) * ht + jnp.concatenate(states, axis=1)
        for k in range(D_SSD // LANES):
            st_ref[q, k * LANES:(k + 1) * LANES, :] = ht_new[:, k * LANES:(k + 1) * LANES].T
    yoff[...] = yoff[...] + add

    @pl.when(s_id == nseq // SSD_SAMPLE_SEQS - 1)
    def _():
        y = ybase[...] + yoff[...] * jnp.exp(acumx_s[...])
        y_ref[...] = _ssd_finish(y, xs_s[...], z_ref[...], dskip_ref[...], gnorm_ref[...]).astype(BF16)


def _ssd_sample(zx, tail, ctx, h0, cw, cb, dtb, alog, dskip, gnorm, tri, sel, emat):
    m = zx.shape[0]
    nt = m // SSD_CHUNK
    nseq = SSD_CHUNK // SAMPLE_ROWS
    return pl.pallas_call(
        _ssd_sample_kernel,
        grid=(nt, nseq // SSD_SAMPLE_SEQS),
        in_specs=[pl.BlockSpec((SSD_CHUNK, D_SSD), lambda t, s: (t, 0)),
                  pl.BlockSpec((SSD_CHUNK, D_SSD), lambda t, s: (t, 1)),
                  pl.BlockSpec((SSD_CHUNK, BC_DIM), lambda t, s: (t, 0)),
                  pl.BlockSpec((SSD_CHUNK, DT_PAD), lambda t, s: (t, BC_DIM // DT_PAD)),
                  pl.BlockSpec((1, SSD_CHUNK + 8, D_SSD + BC_DIM), lambda t, s: (t, 0, 0)),
                  pl.BlockSpec((SSD_SAMPLE_SEQS, D_SSD, SSD_STATE), lambda t, s: (t * (nseq // SSD_SAMPLE_SEQS) + s, 0, 0)),
                  _const_spec(cw.shape), _const_spec(cb.shape), _const_spec(dtb.shape), _const_spec(alog.shape),
                  _const_spec(dskip.shape), _const_spec(gnorm.shape), _const_spec(tri.shape),
                  _const_spec(sel.shape), _const_spec(emat.shape)],
        out_specs=[pl.BlockSpec((SSD_CHUNK, D_SSD), lambda t, s: (t, 0)),
                   pl.BlockSpec((SSD_SAMPLE_SEQS, D_SSD, SSD_STATE), lambda t, s: (t * (nseq // SSD_SAMPLE_SEQS) + s, 0, 0))],
        out_shape=[jax.ShapeDtypeStruct((m, D_SSD), BF16),
                   jax.ShapeDtypeStruct(h0.shape, F32)],
        scratch_shapes=[pltpu.VMEM((SSD_CHUNK, D_SSD), F32),
                        pltpu.VMEM((SSD_CHUNK, D_SSD), F32),
                        pltpu.VMEM((SSD_CHUNK, D_SSD), F32),
                        pltpu.VMEM((SSD_CHUNK, D_SSD), BF16),
                        pltpu.VMEM((SSD_CHUNK, D_SSD), F32),
                        pltpu.VMEM((SSD_CHUNK, SSD_GROUPS * SSD_STATE), F32),
                        pltpu.VMEM((SSD_CHUNK, SSD_GROUPS * SSD_STATE), F32)],
        compiler_params=pltpu.CompilerParams(dimension_semantics=("arbitrary", "arbitrary"),
                                             vmem_limit_bytes=VMEM_LIMIT),
        name="ssd_sample",
    )(zx, zx, tail, tail, ctx, h0, cw, cb, dtb, alog, dskip, gnorm, tri, sel, emat)


def _sb_block(qb, kb, vb, bias, uo, run, mask, nearest_first):
    z = _dot_nt(qb, kb) * ATT_SCALE + bias
    nz = -z
    lg = jnp.log(1.0 + jnp.exp(jnp.minimum(z, nz)))
    lm = jnp.minimum(nz, 0.0) - lg
    ls_pos = lm + z
    if mask is not None:
        lm = jnp.where(mask, lm, 0.0)
    hi = lm.astype(BF16)
    lo = (lm - hi.astype(F32)).astype(BF16)
    nsub = z.shape[1] // LANES
    tails = [None] * nsub
    for c in (range(nsub) if nearest_first else reversed(range(nsub))):
        cols = slice(c * LANES, (c + 1) * LANES)
        st = _dot(jnp.concatenate([hi[:, cols], lo[:, cols]], axis=1), uo)
        tails[c] = run + st[:, :LANES]
        run = run + st[:, LANES:]
    w = jnp.exp(ls_pos + jnp.concatenate(tails, axis=1))
    if mask is not None:
        w = jnp.where(mask, w, 0.0)
    return _dot(w.astype(BF16), vb), run


ATT_TQ = 512
ATT_TK = ATT_TQ
ATT_HEADS_PER_STEP = 1
ATT_DIAG_ROWS = 256


def _attn_prompt_kernel(bias_ref, q_ref, k_ref, v_ref, uo_ref, *refs):
    nw = (len(refs) - 1) // 2
    o_ref = refs[nw]

    def convert_weights():
        for w32_ref, wbf_ref in zip(refs[:nw], refs[nw + 1:]):
            wbf_ref[...] = w32_ref[...].astype(BF16)

    hg = pl.program_id(1)
    qi = pl.program_id(2)
    uo = uo_ref[...]
    heads = range(ATT_HEADS_PER_STEP)
    cols = [slice(h * ATT_HEAD_DIM, (h + 1) * ATT_HEAD_DIM) for h in heads]
    qbs = [q_ref[0, :, cols[h]] for h in heads]
    biases = [bias_ref[hg * ATT_HEADS_PER_STEP + h] for h in heads]
    q0 = pl.multiple_of(qi * ATT_TQ, ATT_TQ)

    def keys_values(h, start, n):
        return k_ref[0, pl.ds(start, n), cols[h]], v_ref[0, pl.ds(start, n), cols[h]]

    def diagonal(h):
        accs, runs = [], []
        for r in range(ATT_TQ // ATT_DIAG_ROWS):
            nk = (r + 1) * ATT_DIAG_ROWS
            row = lax.broadcasted_iota(jnp.int32, (ATT_DIAG_ROWS, nk), 0) + r * ATT_DIAG_ROWS
            col = lax.broadcasted_iota(jnp.int32, (ATT_DIAG_ROWS, nk), 1)
            kb, vb = keys_values(h, q0, nk)
            a, rn = _sb_block(qbs[h][r * ATT_DIAG_ROWS:(r + 1) * ATT_DIAG_ROWS], kb, vb, biases[h], uo,
                              jnp.zeros((ATT_DIAG_ROWS, LANES), F32), col < row, False)
            accs.append(a)
            runs.append(rn)
        return jnp.concatenate(accs, axis=0), jnp.concatenate(runs, axis=0)

    def block(kb_index, carry):
        start = pl.multiple_of(kb_index * ATT_TK, ATT_TK)
        out = []
        for h in heads:
            acc, run = carry[2 * h], carry[2 * h + 1]
            kb, vb = keys_values(h, start, ATT_TK)
            pv, run = _sb_block(qbs[h], kb, vb, biases[h], uo, run, None, False)
            out += [acc + pv, run]
        return tuple(out)

    def store(carry):
        for h in heads:
            o_ref[0, :, cols[h]] = carry[2 * h].astype(BF16)

    @pl.when(qi == 0)
    def _():
        convert_weights()
        store(sum((diagonal(h) for h in heads), ()))

    @pl.when(qi > 0)
    def _():
        convert_weights()
        carry = block(qi - 1, sum((diagonal(h) for h in heads), ()))
        store(lax.fori_loop(0, qi - 1, lambda i, c: block(qi - 2 - i, c), carry))


def _attn_prompt(qkv3, sb_bias, uo, weights):
    nb, seq, _ = qkv3.shape
    tq = ATT_TQ
    width = ATT_HEADS_PER_STEP * ATT_HEAD_DIM
    groups = ATT_HEADS // ATT_HEADS_PER_STEP
    nq = seq // tq
    steps = nb * groups * nq
    w_specs = []
    for w in weights:
        r, c = w.shape
        if r % (16 * steps) == 0:
            w_specs.append(pl.BlockSpec((r // steps, c), lambda b, g, i: ((b * groups + g) * nq + i, 0)))
        else:
            cb = c // LANES // 2
            rb = steps // cb
            assert r % (16 * rb) == 0 and rb * cb == steps, (w.shape, steps)
            w_specs.append(pl.BlockSpec((r // rb, c // cb),
                                        lambda b, g, i, cb=cb: (((b * groups + g) * nq + i) // cb,
                                                                ((b * groups + g) * nq + i) % cb)))
    return pl.pallas_call(
        _attn_prompt_kernel,
        grid=(nb, groups, nq),
        in_specs=[pl.BlockSpec(memory_space=pltpu.SMEM),
                  pl.BlockSpec((1, tq, width), lambda b, g, i: (b, i, g)),
                  pl.BlockSpec((1, seq, width), lambda b, g, i: (b, 0, groups + g)),
                  pl.BlockSpec((1, seq, width), lambda b, g, i: (b, 0, 2 * groups + g)),
                  _const_spec(uo.shape)] + w_specs,
        out_specs=[pl.BlockSpec((1, tq, width), lambda b, g, i: (b, i, g))] + w_specs,
        out_shape=[jax.ShapeDtypeStruct((nb, seq, D_ATT), BF16)]
                  + [jax.ShapeDtypeStruct(w.shape, BF16) for w in weights],
        compiler_params=pltpu.CompilerParams(dimension_semantics=("arbitrary", "arbitrary", "arbitrary"),
                                             vmem_limit_bytes=VMEM_LIMIT),
        name="attn_prompt",
    )(sb_bias, qkv3, qkv3, qkv3, uo, *weights)


PAGES_PER_STEP = 8
PAGE_SLOTS = 3


def _page_by_head(ref):
    heads = [ref[pl.ds(h, PAGE_SIZE, stride=ATT_HEADS), :] for h in range(ATT_HEADS)]
    return jnp.concatenate(heads, axis=1).astype(BF16)


def _attn_sample_kernel(nseq, steps, pt_ref, q_ref, kn_ref, vn_ref, bias_ref, uo_ref, ck_hbm, cv_hbm, o_ref,
                        qbd, knew, vnew, acc, run, kbuf, vbuf, sems):
    b = pl.program_id(0)
    j = pl.program_id(1)
    total = nseq * steps
    t = b * steps + j
    nrow = ATT_HEADS * SAMPLE_ROWS
    bias = bias_ref[...]
    uo = uo_ref[...]

    def page_copies(step):
        sb = step // steps
        sj = step - sb * steps
        slot = step % PAGE_SLOTS
        copies = []
        for g in range(PAGES_PER_STEP):
            page = pt_ref[sb, steps * PAGES_PER_STEP - 1 - (sj * PAGES_PER_STEP + g)]
            copies.append(pltpu.make_async_copy(ck_hbm.at[page], kbuf.at[slot, g], sems.at[0, slot]))
            copies.append(pltpu.make_async_copy(cv_hbm.at[page], vbuf.at[slot, g], sems.at[1, slot]))
        return copies

    @pl.when(t == 0)
    def _():
        for step in range(PAGE_SLOTS - 1):
            for c in page_copies(step):
                c.start()

    @pl.when(t + (PAGE_SLOTS - 1) < total)
    def _():
        for c in page_copies(t + (PAGE_SLOTS - 1)):
            c.start()

    @pl.when((b == 0) & (j == 0))
    def _():
        knew[...] = jnp.zeros_like(knew)
        vnew[...] = jnp.zeros_like(vnew)

    @pl.when(j == 0)
    def _():
        q8 = q_ref[0]
        rr = lax.broadcasted_iota(jnp.int32, (nrow, D_ATT), 0)
        cc = lax.broadcasted_iota(jnp.int32, (nrow, D_ATT), 1)
        qt = jnp.concatenate([q8] * ATT_HEADS, axis=0)
        qbd[...] = jnp.where((rr // SAMPLE_ROWS) == (cc // ATT_HEAD_DIM), qt, 0.0).astype(BF16)
        knew[0:SAMPLE_ROWS, :] = kn_ref[0]
        vnew[0:SAMPLE_ROWS, :] = vn_ref[0]
        qq = lax.broadcasted_iota(jnp.int32, (nrow, LANES), 0) % SAMPLE_ROWS
        kk = lax.broadcasted_iota(jnp.int32, (nrow, LANES), 1)
        mask = (kk < qq) & (kk < SAMPLE_ROWS // 2)
        pv, r = _sb_block(qbd[...], knew[...].astype(BF16), vnew[...].astype(BF16), bias, uo,
                          jnp.zeros((nrow, LANES), F32), mask, True)
        acc[...] = pv
        run[...] = r

    for c in page_copies(t):
        c.wait()
    slot = t % PAGE_SLOTS
    kcat = jnp.concatenate([_page_by_head(kbuf.at[slot, g]) for g in range(PAGES_PER_STEP)], axis=0)
    vcat = jnp.concatenate([_page_by_head(vbuf.at[slot, g]) for g in range(PAGES_PER_STEP)], axis=0)
    pv, r = _sb_block(qbd[...], kcat, vcat, jnp.concatenate([bias] * PAGES_PER_STEP, axis=1), uo, run[...], None, True)
    a = acc[...] + pv
    acc[...] = a
    run[...] = r

    @pl.when(j == steps - 1)
    def _():
        for h in range(ATT_HEADS):
            o_ref[0, :, h * ATT_HEAD_DIM:(h + 1) * ATT_HEAD_DIM] = (
                a[h * SAMPLE_ROWS:(h + 1) * SAMPLE_ROWS, h * ATT_HEAD_DIM:(h + 1) * ATT_HEAD_DIM])


def _attn_sample(page_table, projs3, cache_k, cache_v, bias_rows, uo):
    nseq, npages = page_table.shape
    steps = npages // PAGES_PER_STEP
    assert steps * PAGES_PER_STEP == npages and nseq * steps >= PAGE_SLOTS - 1
    page_buf = pltpu.VMEM((PAGE_SLOTS, PAGES_PER_STEP, PAGE_SIZE * ATT_HEADS, ATT_HEAD_DIM), F32)

    grid_spec = pltpu.PrefetchScalarGridSpec(
        num_scalar_prefetch=1,
        grid=(nseq, steps),
        in_specs=[pl.BlockSpec((1, SAMPLE_ROWS, D_ATT), lambda b, j, pt: (b, 0, 0)),
                  pl.BlockSpec((1, SAMPLE_ROWS, D_ATT), lambda b, j, pt: (b, 0, 1)),
                  pl.BlockSpec((1, SAMPLE_ROWS, D_ATT), lambda b, j, pt: (b, 0, 2)),
                  pl.BlockSpec(bias_rows.shape, lambda b, j, pt: (0, 0)),
                  pl.BlockSpec(uo.shape, lambda b, j, pt: (0, 0)),
                  pl.BlockSpec(memory_space=pl.ANY), pl.BlockSpec(memory_space=pl.ANY)],
        out_specs=pl.BlockSpec((1, SAMPLE_ROWS, D_ATT), lambda b, j, pt: (b, 0, 0)),
        scratch_shapes=[pltpu.VMEM((ATT_HEADS * SAMPLE_ROWS, D_ATT), BF16),
                        pltpu.VMEM((PAGE_SIZE, D_ATT), F32),
                        pltpu.VMEM((PAGE_SIZE, D_ATT), F32),
                        pltpu.VMEM((ATT_HEADS * SAMPLE_ROWS, D_ATT), F32),
                        pltpu.VMEM((ATT_HEADS * SAMPLE_ROWS, LANES), F32),
                        page_buf, page_buf, pltpu.SemaphoreType.DMA((2, PAGE_SLOTS))])
    return pl.pallas_call(
        functools.partial(_attn_sample_kernel, nseq, steps),
        grid_spec=grid_spec,
        out_shape=jax.ShapeDtypeStruct((nseq, SAMPLE_ROWS, D_ATT), F32),
        compiler_params=pltpu.CompilerParams(dimension_semantics=("arbitrary", "arbitrary"),
                                             vmem_limit_bytes=VMEM_LIMIT),
        name="attn_sample",
    )(page_table, projs3, projs3, projs3, bias_rows, uo, cache_k, cache_v)


def _outproj_kernel(y_ref, o_ref, x_ref, g1_ref, sh2_ref, sc2_ref, gpost_ref, gpre_ref, wy_ref, wo_ref, x1_ref, h2_ref):
    tm = x_ref.shape[0]
    chunk = min(tm, 256)
    for r in range(0, tm, chunk):
        rows = slice(r, r + chunk)
        g1, sh2, sc2 = (v if v.shape[0] == 1 else v[rows] for v in (g1_ref[0], sh2_ref[0], sc2_ref[0]))
        m = _dot(y_ref[rows, :], wy_ref[...]) + _dot(o_ref[rows, :].astype(BF16), wo_ref[...])
        x1 = x_ref[rows, :] + g1 * _rms(m, gpost_ref[...])
        x1_ref[rows, :] = x1
        h2 = _rms(x1, gpre_ref[...]) * (1.0 + sc2) + sh2
        h2_ref[rows, :] = h2.astype(BF16)


def _outproj(y, o, x, mods, gpost, gpre, w_out, tm, tiles_per_group):
    m = x.shape[0]
    rows = mods.shape[1]
    half = w_out.shape[0] // 2
    row = lambda i: (i, 0)
    return pl.pallas_call(
        _outproj_kernel,
        grid=(m // tm,),
        in_specs=[pl.BlockSpec((tm, D_SSD), row), pl.BlockSpec((tm, D_ATT), row), pl.BlockSpec((tm, D_MODEL), row),
                  _mod_spec(2, rows, tiles_per_group), _mod_spec(3, rows, tiles_per_group),
                  _mod_spec(4, rows, tiles_per_group),
                  _const_spec(gpost.shape), _const_spec(gpre.shape),
                  pl.BlockSpec((half, D_MODEL), lambda i: (0, 0)),
                  pl.BlockSpec((half, D_MODEL), lambda i: (1, 0))],
        out_specs=[pl.BlockSpec((tm, D_MODEL), row), pl.BlockSpec((tm, D_MODEL), row)],
        out_shape=[jax.ShapeDtypeStruct((m, D_MODEL), F32), jax.ShapeDtypeStruct((m, D_MODEL), BF16)],
        compiler_params=pltpu.CompilerParams(dimension_semantics=("arbitrary",), vmem_limit_bytes=VMEM_LIMIT),
        name="out_proj",
    )(y, o, x, mods, mods, mods, gpost, gpre, w_out, w_out)


FFN_CHUNK = 256


def _ffn_kernel(sample, tiles_per_group, *refs):
    if sample:
        (h_ref, x1_hbm, g2_ref, gpost_ref, wg_ref, wv_ref, wo_ref, cwg_ref, cwv_ref, cbg_ref, cbv_ref,
         injg_ref, injv_ref, y_ref, ug_ref, uv_ref, x1_buf, x1_sem) = refs
    else:
        (h_ref, x1_hbm, g2_ref, gpost_ref, wg_ref, wv_ref, wo_ref, cwg_ref, cwv_ref, cbg_ref, cbv_ref,
         y_ref, ug_ref, uv_ref, x1_buf, x1_sem, halog, halov) = refs
    i = pl.program_id(0)
    j = pl.program_id(1)
    tm = h_ref.shape[0]
    tf = wg_ref.shape[1]
    hn = h_ref[...]
    x1_copy = pltpu.make_async_copy(x1_hbm.at[pl.ds(pl.multiple_of(i * tm, tm), tm), :], x1_buf, x1_sem)

    @pl.when(j == 0)
    def _():
        x1_copy.start()
        y_ref[...] = jnp.zeros_like(y_ref)

    if sample:
        is_ctx = (lax.broadcasted_iota(jnp.int32, (tm, 1), 0) % SAMPLE_ROWS) >= SAMPLE_ROWS - (FFN_CONV - 1)
    else:
        @pl.when((i == 0) & (j == 0))
        def _():
            halog[...] = jnp.zeros_like(halog)
            halov[...] = jnp.zeros_like(halov)

        first = lax.rem(i, tiles_per_group) == 0

    def half(w_ref, cw_ref, cb_ref, u_ref, inj_ref, halo, cols):
        u = jnp.concatenate([_dot(hn[r:r + 128], w_ref[:, cols]) for r in range(0, tm, 128)], axis=0)
        if sample:
            u_ref[:, cols] = u
            prev = inj_ref[0:8, cols]
            u = jnp.where(is_ctx, inj_ref[8:, cols], u)
        else:
            prev = jnp.where(first, 0.0, halo[j, :, cols])
            halo[j, :, cols] = u[tm - 8:, :]
            u_ref[0, :, cols] = u[tm - 8:, :]
        return _conv_rows(u, prev, cw_ref[:, cols], cb_ref[:, cols])

    acts = []
    for c in range(tf // FFN_CHUNK):
        cols = slice(c * FFN_CHUNK, (c + 1) * FFN_CHUNK)
        gate = half(wg_ref, cwg_ref, cbg_ref, ug_ref, None if not sample else injg_ref, None if sample else halog, cols)
        val = half(wv_ref, cwv_ref, cbv_ref, uv_ref, None if not sample else injv_ref, None if sample else halov, cols)
        acts.append((_silu(gate) * val).astype(BF16))
    y_ref[...] += _dot(jnp.concatenate(acts, axis=1), wo_ref[...])

    @pl.when(j == pl.num_programs(1) - 1)
    def _():
        x1_copy.wait()
        chunk = min(tm, 256)
        for r in range(0, tm, chunk):
            rows = slice(r, r + chunk)
            g2 = g2_ref[0]
            g2 = g2 if g2.shape[0] == 1 else g2[rows]
            y_ref[rows, :] = x1_buf[rows, :] + g2 * _rms(y_ref[rows, :], gpost_ref[...])


def _ffn(h2, x1, mods, gpost, w_in, w_out, cw, cb, tm, tiles_per_group, inj=None):
    m = h2.shape[0]
    rows = mods.shape[1]
    tf = 512
    nf = D_FF // tf
    sample = inj is not None
    row = lambda i, j: (i, 0)
    gate_col = lambda i, j: (0, j)
    val_col = lambda i, j: (0, nf + j)
    in_specs = [pl.BlockSpec((tm, D_MODEL), row), pl.BlockSpec(memory_space=pl.ANY),
                _mod_spec(5, rows, tiles_per_group), _const_spec(gpost.shape),
                pl.BlockSpec((D_MODEL, tf), gate_col), pl.BlockSpec((D_MODEL, tf), val_col),
                pl.BlockSpec((tf, D_MODEL), lambda i, j: (j, 0)),
                pl.BlockSpec((FFN_CONV, tf), gate_col), pl.BlockSpec((FFN_CONV, tf), val_col),
                pl.BlockSpec((1, tf), gate_col), pl.BlockSpec((1, tf), val_col)]
    args = [h2, x1, mods, gpost, w_in, w_in, w_out, cw, cw, cb, cb]
    scratch = [pltpu.VMEM((tm, D_MODEL), F32), pltpu.SemaphoreType.DMA(())]
    if sample:
        in_specs += [pl.BlockSpec((m + 8, tf), gate_col), pl.BlockSpec((m + 8, tf), val_col)]
        args += [inj, inj]
        u_specs = [pl.BlockSpec((tm, tf), lambda i, j: (i, j))] * 2
        u_shapes = [jax.ShapeDtypeStruct((m, D_FF), F32)] * 2
    else:
        u_specs = [pl.BlockSpec((1, 8, tf), lambda i, j: (i, 0, j))] * 2
        u_shapes = [jax.ShapeDtypeStruct((m // tm, 8, D_FF), F32)] * 2
        scratch += [pltpu.VMEM((nf, 8, tf), F32), pltpu.VMEM((nf, 8, tf), F32)]
    return pl.pallas_call(
        functools.partial(_ffn_kernel, sample, tiles_per_group),
        grid=(m // tm, nf),
        in_specs=in_specs,
        out_specs=[pl.BlockSpec((tm, D_MODEL), row)] + u_specs,
        out_shape=[jax.ShapeDtypeStruct((m, D_MODEL), F32)] + u_shapes,
        scratch_shapes=scratch,
        compiler_params=pltpu.CompilerParams(dimension_semantics=("arbitrary", "arbitrary"),
                                             vmem_limit_bytes=VMEM_LIMIT),
        name="ffn_sample" if sample else "ffn_prompt",
    )(*args)


def _ssd_constants(block_rows):
    r = jnp.arange(SSD_CHUNK)[:, None]
    c = jnp.arange(SSD_CHUNK)[None, :]
    same = (r // block_rows) == (c // block_rows)
    tri = ((c <= r) & same).astype(BF16)
    sel = (c == (r // block_rows) * block_rows + block_rows - 1).astype(BF16)
    hrow = jnp.arange(LANES)[:, None]
    e1 = (hrow == jnp.arange(D_SSD)[None, :] // SSD_HEAD_DIM).astype(BF16)
    e2 = (hrow == jnp.arange(SSD_HEADS * LANES)[None, :] // LANES).astype(BF16)
    return tri, sel, jnp.concatenate([e1, e2], axis=1)


def _suffix_matrix():
    j = jnp.arange(LANES)[:, None]
    s = jnp.arange(LANES)[None, :]
    u = jnp.concatenate([(j > s).astype(BF16), jnp.ones((LANES, LANES), BF16)], axis=1)
    return jnp.concatenate([u, u], axis=0)


def _inject_rows(ctx, nctx):
    nseq, _, c = ctx.shape
    body = jnp.concatenate([ctx[1:], jnp.zeros((1, nctx, c), F32)], axis=0)
    body = jnp.pad(body, ((0, 0), (SAMPLE_ROWS - nctx, 0), (0, 0))).reshape(nseq * SAMPLE_ROWS, c)
    head = jnp.pad(ctx[0], ((8 - nctx, 0), (0, 0)))
    return jnp.concatenate([head, body], axis=0)


def kernel(x_prompt, x_sample, c_prompt, c_sample, cache_k, cache_v, page_table, state_ssm, state_conv, state_ffn_conv, w_ada, b_ada, g_pre_mix, g_post_mix, g_pre_ffn, g_post_ffn, w_in, conv_w, conv_b, dt_bias, a_log, d_skip, g_ssd_norm, sb_bias, w_out, w_ffn_in, ffn_conv_w, ffn_conv_b, w_ffn_out):
    assert w_ada.shape[0] == 1, "one layer"
    nb, seq, d = x_prompt.shape
    ns, ls, _ = x_sample.shape
    n_pool = cache_k.shape[1]
    mp = nb * seq
    ms = ns * SAMPLE_ROWS

    w = w_in[0].astype(BF16)
    o_bc, o_q = 2 * D_SSD, 2 * D_SSD + BC_DIM + SSD_HEADS
    w_zx = w
    w_qkv = w[:, o_q:]
    w_tail = jnp.pad(w[:, o_bc:o_q], ((0, 0), (0, DT_PAD - SSD_HEADS)))
    row = lambda v: v.reshape(1, -1)
    dtb = jnp.pad(dt_bias[0], (0, DT_PAD - SSD_HEADS)).reshape(1, DT_PAD)
    alog = jnp.pad(a_log[0], (0, DT_PAD - SSD_HEADS)).reshape(1, DT_PAD)
    dskip = jnp.repeat(d_skip[0], SSD_HEAD_DIM).reshape(1, D_SSD)
    gnorm = row(g_ssd_norm[0])
    cw, cb = conv_w[0], row(conv_b[0])
    fcw, fcb = ffn_conv_w[0], row(ffn_conv_b[0])
    uo = _suffix_matrix()

    n_c = nb + ns
    c_all = jnp.concatenate([c_prompt, c_sample, jnp.zeros((-n_c % 16, d), F32)], axis=0)
    mod = _ada(c_all, w_ada[0], row(b_ada[0]))
    mods_p = mod[:nb].reshape(nb, 1, 6 * d)
    mods_s = jnp.repeat(mod[nb:n_c], SAMPLE_ROWS, axis=0).reshape(1, ms, 6 * d)

    tm_p = 512
    tpg_p = seq // tm_p
    xp = x_prompt.reshape(mp, d)
    hn_p, tail_p = _prenorm(xp, mods_p, row(g_pre_mix[0]), w_tail, tm_p, tpg_p)
    zx_p, qkv_p, k_p, v_p = _inproj(hn_p, w_zx, w_qkv, 1024, BF16)
    zx3 = zx_p.reshape(nb, seq, 2 * D_SSD)
    tail3 = tail_p.reshape(nb, seq, PROJ_TAIL)
    tri, sel, emat = _ssd_constants(SSD_CHUNK)
    y_ssd, ssm_p = _ssd_prompt(zx3, tail3, cw, cb, dtb, alog, dskip, gnorm, tri, sel, emat)
    o_att, w_out_b, w_ffn_in_b, w_ffn_out_b = _attn_prompt(qkv_p.reshape(nb, seq, 3 * D_ATT), sb_bias[0], uo,
                                                           [w_out[0], w_ffn_in[0], w_ffn_out[0]])
    tm_e = 512
    x1_p, h2_p = _outproj(y_ssd.reshape(mp, D_SSD), o_att.reshape(mp, D_ATT), xp, mods_p, row(g_post_mix[0]),
                          row(g_pre_ffn[0]), w_out_b, tm_e, seq // tm_e)
    tm_f = 1024
    yp, ctxg_p, ctxv_p = _ffn(h2_p, x1_p, mods_p, row(g_post_ffn[0]), w_ffn_in_b, w_ffn_out_b, fcw, fcb,
                              tm_f, seq // tm_f)

    xs_pad = jnp.pad(x_sample, ((0, 0), (0, SAMPLE_ROWS - ls), (0, 0))).reshape(ms, d)
    hn_s, tail_s = _prenorm(xs_pad, mods_s, row(g_pre_mix[0]), w_tail, ms, 1)
    zx_s, qkv_s, k_s, v_s = _inproj(hn_s, w_zx, w_qkv, ms, F32)
    seq_per_tile = SSD_CHUNK // SAMPLE_ROWS
    conv_ctx = state_conv[0].reshape(ns // seq_per_tile, seq_per_tile, SSD_CONV - 1, -1)
    ctx_inj = jax.vmap(lambda c: _inject_rows(c, SSD_CONV - 1))(conv_ctx)
    tri_s, sel_s, _ = _ssd_constants(SAMPLE_ROWS)
    y_ssd_s, ssm_s = _ssd_sample(zx_s, tail_s, ctx_inj, state_ssm[0].reshape(ns, D_SSD, SSD_STATE), cw, cb, dtb, alog,
                                 dskip, gnorm, tri_s, sel_s, emat)
    bias_rows = jnp.broadcast_to(jnp.repeat(sb_bias[0], SAMPLE_ROWS)[:, None], (ATT_HEADS * SAMPLE_ROWS, LANES))
    o_att_s = _attn_sample(page_table, qkv_s.reshape(ns, SAMPLE_ROWS, 3 * D_ATT),
                           cache_k[0].reshape(n_pool, PAGE_SIZE * ATT_HEADS, ATT_HEAD_DIM),
                           cache_v[0].reshape(n_pool, PAGE_SIZE * ATT_HEADS, ATT_HEAD_DIM), bias_rows, uo)
    x1_s, h2_s = _outproj(y_ssd_s, o_att_s.reshape(ms, D_ATT), xs_pad, mods_s, row(g_post_mix[0]),
                          row(g_pre_ffn[0]), w_out_b, ms, 1)
    inj_f = _inject_rows(state_ffn_conv[0], FFN_CONV - 1)
    ys_pad, ug_s, uv_s = _ffn(h2_s, x1_s, mods_s, row(g_post_ffn[0]), w_ffn_in_b, w_ffn_out_b, fcw, fcb,
                              ms, 1, inj=inj_f)

    heads = (ATT_HEADS, ATT_HEAD_DIM)
    k_p = k_p.reshape(1, nb, seq, *heads)
    v_p = v_p.reshape(1, nb, seq, *heads)
    tail = slice(seq - (SSD_CONV - 1), seq)
    conv_p = jnp.concatenate([zx3[:, tail, D_SSD:], tail3[:, tail, :BC_DIM]], axis=-1)[None]
    ffn_p = jnp.concatenate([ctxg_p, ctxv_p], axis=-1)[seq // tm_f - 1::seq // tm_f, 8 - (FFN_CONV - 1):][None]
    k_s = k_s.reshape(ns, SAMPLE_ROWS, *heads)[None, :, :ls]
    v_s = v_s.reshape(ns, SAMPLE_ROWS, *heads)[None, :, :ls]
    raw_s = jnp.concatenate([zx_s.reshape(ns, SAMPLE_ROWS, -1)[:, :ls, D_SSD:],
                             tail_s.reshape(ns, SAMPLE_ROWS, -1)[:, :ls, :BC_DIM]], axis=-1)
    conv_s = jnp.concatenate([state_conv[0], raw_s], axis=1)[:, -(SSD_CONV - 1):][None]
    keep = FFN_CONV - 1
    u_rows = [u.reshape(ns, SAMPLE_ROWS, D_FF)[:, max(ls - keep, 0):ls] for u in (ug_s, uv_s)]
    ffn_s = jnp.concatenate([state_ffn_conv[0][:, ls:], jnp.concatenate(u_rows, axis=-1)], axis=1)[None]
    return (yp.reshape(nb, seq, d), ys_pad.reshape(ns, SAMPLE_ROWS, d)[:, :ls],
            k_p, v_p, ssm_p.reshape(1, nb, SSD_HEADS, SSD_HEAD_DIM, SSD_STATE), conv_p, ffn_p,
            k_s, v_s, ssm_s.reshape(1, ns, SSD_HEADS, SSD_HEAD_DIM, SSD_STATE), conv_s, ffn_s)
```
